```python
import jax, jax.numpy as jnp
from jax import lax
import numpy as np

D_MODEL = 4096
BATCH = 2
SEQ = 4096
DEPTH = 2

RMS_EPS = 1e-6
NEG_INF = -1e30
MLA_HEADS = 16
MLA_Q_LORA = 1024
MLA_KV_LORA = 512
MLA_NOPE_DIM = 128
MLA_ROPE_DIM = 64
MLA_V_DIM = 128
MLA_QK_DIM = MLA_NOPE_DIM + MLA_ROPE_DIM
ROPE_THETA = 10000.0
Q_BLOCK = 128
GLA_HEADS = 8
GLA_DK = 128
GLA_DV = 256
GLA_GATE_RANK = 16
GLA_GATE_TAU = 16.0
GLA_CHUNK = 64
SWA_Q_HEADS = 32
SWA_KV_HEADS = 4
SWA_HEAD_DIM = 64
SWA_WINDOW = 128
SWA_BLOCK = SWA_WINDOW
MLA_WIDTH = MLA_HEADS * MLA_V_DIM
GLA_WIDTH = GLA_HEADS * GLA_DV
SWA_WIDTH = SWA_Q_HEADS * SWA_HEAD_DIM
N_BRANCHES = 3
N_GROUPS = 4
EXPERTS_PER_GROUP = 8
N_EXPERTS = N_GROUPS * EXPERTS_PER_GROUP
TOP_K_IN_GROUP = 2
EXPERT_FF = 768
IN_SPLITS = (MLA_Q_LORA, MLA_KV_LORA, MLA_ROPE_DIM,
             GLA_HEADS * GLA_DK, GLA_HEADS * GLA_DK, GLA_WIDTH, GLA_GATE_RANK, GLA_WIDTH,
             SWA_WIDTH, SWA_KV_HEADS * SWA_HEAD_DIM, SWA_KV_HEADS * SWA_HEAD_DIM,
             N_BRANCHES * D_MODEL)
IN_DIM = sum(IN_SPLITS)

kernel_name = "hybrid_mla_gla_swa_hmoe"


def rms_norm(x, gain):
    xf = x.astype(jnp.float32)
    y = xf * lax.rsqrt(jnp.mean(xf * xf, axis=-1, keepdims=True) + RMS_EPS)
    return (y * gain.astype(jnp.float32)).astype(x.dtype)


def split_columns(z):
    offsets = []
    acc = 0
    for n in IN_SPLITS[:-1]:
        acc += n
        offsets.append(acc)
    return jnp.split(z, offsets, axis=-1)


def rope(x, cos, sin):
    x1, x2 = jnp.split(x, 2, axis=-1)
    return jnp.concatenate([x1 * cos - x2 * sin, x1 * sin + x2 * cos], axis=-1)


def mla_attention(c_q, c_kv, k_r, positions, q_norm, w_uq, kv_norm, w_ukv):
    B, S, _ = c_q.shape
    dt = c_q.dtype
    f32 = jnp.float32
    q = (rms_norm(c_q, q_norm) @ w_uq).reshape(B, S, MLA_HEADS, MLA_QK_DIM)
    kv = (rms_norm(c_kv, kv_norm) @ w_ukv).reshape(B, S, MLA_HEADS, MLA_NOPE_DIM + MLA_V_DIM)
    k_nope, v = jnp.split(kv, [MLA_NOPE_DIM], axis=-1)
    inv_freq = ROPE_THETA ** (-jnp.arange(0, MLA_ROPE_DIM, 2, dtype=f32) / MLA_ROPE_DIM)
    ang = positions.astype(f32)[..., None] * inv_freq
    cos, sin = jnp.cos(ang), jnp.sin(ang)
    q_nope, q_rot = jnp.split(q, [MLA_NOPE_DIM], axis=-1)
    q_rot = rope(q_rot.astype(f32), cos[:, :, None], sin[:, :, None]).astype(dt)
    k_rot = rope(k_r.astype(f32), cos, sin).astype(dt)
    q = jnp.concatenate([q_nope, q_rot], axis=-1)
    k = jnp.concatenate([k_nope, jnp.broadcast_to(k_rot[:, :, None, :], (B, S, MLA_HEADS, MLA_ROPE_DIM))], axis=-1)
    scale = MLA_QK_DIM ** -0.5
    n_blk = S // Q_BLOCK
    q_blocks = q.reshape(B, n_blk, Q_BLOCK, MLA_HEADS, MLA_QK_DIM).transpose(1, 0, 2, 3, 4)
    key_idx = jnp.arange(S)

    def block(args):
        qb, n = args
        s = jnp.einsum('bqhd,bkhd->bhqk', qb, k, preferred_element_type=f32) * scale
        q_idx = n * Q_BLOCK + jnp.arange(Q_BLOCK)
        s = jnp.where(key_idx[None, :] <= q_idx[:, None], s, NEG_INF)
        p = jax.nn.softmax(s, axis=-1).astype(dt)
        return jnp.einsum('bhqk,bkhd->bqhd', p, v)

    o = lax.map(block, (q_blocks, jnp.arange(n_blk)))
    return o.transpose(1, 0, 2, 3, 4).reshape(B, S, MLA_WIDTH)


def gla_attention(q, k, v, a_lr, g, w_a2, b_a, out_norm):
    B, S, _ = q.shape
    dt = q.dtype
    f32 = jnp.float32
    H, C = GLA_HEADS, GLA_CHUNK
    nc = S // C
    log_a = jax.nn.log_sigmoid((a_lr @ w_a2 + b_a).astype(f32)) / GLA_GATE_TAU

    def chunks(t, d):
        return t.astype(f32).reshape(B, nc, C, H, d).transpose(1, 0, 3, 2, 4)

    qc = chunks(q, GLA_DK) * (GLA_DK ** -0.5)
    kc = chunks(k, GLA_DK)
    vc = chunks(v, GLA_DV)
    ac = chunks(log_a, GLA_DK)
    causal = jnp.tril(jnp.ones((C, C), dtype=bool))

    def step(state, inp):
        qi, ki, vi, ai = inp
        b = jnp.cumsum(ai, axis=2)
        o_inter = jnp.einsum('bhcd,bhde->bhce', qi * jnp.exp(b), state)
        diff = b[:, :, :, None, :] - b[:, :, None, :, :]
        decay = jnp.exp(jnp.where(causal[:, :, None], diff, NEG_INF))
        att = jnp.einsum('bhid,bhijd->bhij', qi, decay * ki[:, :, None, :, :])
        o = o_inter + jnp.einsum('bhij,bhje->bhie', att, vi)
        b_last = b[:, :, -1:, :]
        k_dec = ki * jnp.exp(b_last - b)
        new_state = jnp.exp(b_last[:, :, 0, :])[..., None] * state + jnp.einsum('bhcd,bhce->bhde', k_dec, vi)
        return new_state, o

    s0 = jnp.zeros((B, H, GLA_DK, GLA_DV), f32)
    _, o = lax.scan(step, s0, (qc, kc, vc, ac))
    o = o.transpose(1, 0, 3, 2, 4).reshape(B, S, H, GLA_DV)
    o = rms_norm(o, out_norm).reshape(B, S, GLA_WIDTH) * jax.nn.silu(g.astype(f32))
    return o.astype(dt)


def swa_attention(q, k, v, sinks):
    B, S, _ = q.shape
    dt = q.dtype
    f32 = jnp.float32
    W = SWA_BLOCK
    nb = S // W
    Hk = SWA_KV_HEADS
    G = SWA_Q_HEADS // SWA_KV_HEADS
    dh = SWA_HEAD_DIM
    qb = q.reshape(B, nb, W, Hk, G, dh)

    def banded(t):
        t = t.reshape(B, nb, W, Hk, dh)
        prev = jnp.concatenate([jnp.zeros_like(t[:, :1]), t[:, :-1]], axis=1)
        return jnp.concatenate([prev, t], axis=2)

    kb, vb = banded(k), banded(v)
    s = jnp.einsum('bnqhgd,bnkhd->bnhgqk', qb, kb, preferred_element_type=f32) * (dh ** -0.5)
    q_pos = jnp.arange(nb)[:, None, None] * W + jnp.arange(W)[None, :, None]
    k_pos = jnp.arange(nb)[:, None, None] * W - W + jnp.arange(2 * W)[None, None, :]
    dist = q_pos - k_pos
    mask = (dist >= 0) & (dist < SWA_WINDOW) & (k_pos >= 0)
    s = jnp.where(mask[None, :, None, None], s, NEG_INF)
    sink = sinks.astype(f32).reshape(Hk, G)[None, None, :, :, None, None]
    m = jnp.maximum(jnp.max(s, axis=-1, keepdims=True), sink)
    p = jnp.exp(s - m)
    denom = jnp.sum(p, axis=-1, keepdims=True) + jnp.exp(sink - m)
    o = jnp.einsum('bnhgqk,bnkhd->bnqhgd', (p / denom).astype(dt), vb)
    return o.reshape(B, S, SWA_WIDTH)


def hier_moe(h, w_rg, b_rg, w_re, b_re, w_gate, w_up, w_down):
    B, S, D = h.shape
    dt = h.dtype
    f32 = jnp.float32
    t = h.reshape(B * S, D)
    g_prob = jax.nn.softmax((t @ w_rg).astype(f32) + b_rg.astype(f32), axis=-1)
    g_top_p, g_top = lax.top_k(g_prob, 1)
    e_logits = ((t @ w_re).astype(f32) + b_re.astype(f32)).reshape(-1, N_GROUPS, EXPERTS_PER_GROUP)
    e_in = jnp.take_along_axis(e_logits, g_top[:, :, None], axis=1)[:, 0]
    e_val, e_idx = lax.top_k(e_in, TOP_K_IN_GROUP)
    e_w = jax.nn.softmax(e_val, axis=-1) * g_top_p
    local_gate = jnp.sum(jax.nn.one_hot(e_idx, EXPERTS_PER_GROUP, dtype=f32) * e_w[..., None], axis=1)
    gate = (jax.nn.one_hot(g_top[:, 0], N_GROUPS, dtype=f32)[:, :, None] * local_gate[:, None, :]).astype(dt)
    out = jnp.zeros_like(t)
    for grp in range(N_GROUPS):
        sl = slice(grp * EXPERTS_PER_GROUP, (grp + 1) * EXPERTS_PER_GROUP)
        hid = jax.nn.silu(jnp.einsum('nd,edf->enf', t, w_gate[sl])) * jnp.einsum('nd,edf->enf', t, w_up[sl])
        hid = hid * gate[:, grp].T[:, :, None]
        out = out + jnp.einsum('enf,efd->nd', hid, w_down[sl])
    return out.reshape(B, S, D)


def setup_inputs(seed: int = 0) -> dict:
    key = jax.random.key(seed)
    ks = jax.random.split(key, 32)
    f32 = jnp.float32
    L, D = DEPTH, D_MODEL

    def nrm(k, shape, scale):
        return jax.random.normal(k, shape, f32) * scale

    def gain(k, shape):
        return 1.0 + 0.02 * jax.random.normal(k, shape, f32)

    x = jax.random.normal(ks[0], (BATCH, SEQ, D), f32)
    positions = (jnp.arange(SEQ, dtype=jnp.int32)[None, :]
                 + jax.random.randint(ks[1], (BATCH, 1), 0, 1024, dtype=jnp.int32))
    return {
        "x": x,
        "positions": positions,
        "attn_norm": gain(ks[2], (L, D)),
        "w_in": nrm(ks[3], (L, D, IN_DIM), D ** -0.5),
        "gate_bias": nrm(ks[4], (L, N_BRANCHES, D), 0.02),
        "mla_q_norm": gain(ks[5], (L, MLA_Q_LORA)),
        "mla_w_uq": nrm(ks[6], (L, MLA_Q_LORA, MLA_HEADS * MLA_QK_DIM), MLA_Q_LORA ** -0.5),
        "mla_kv_norm": gain(ks[7], (L, MLA_KV_LORA)),
        "mla_w_ukv": nrm(ks[8], (L, MLA_KV_LORA, MLA_HEADS * (MLA_NOPE_DIM + MLA_V_DIM)), MLA_KV_LORA ** -0.5),
        "gla_w_a2": nrm(ks[9], (L, GLA_GATE_RANK, GLA_HEADS * GLA_DK), GLA_GATE_RANK ** -0.5),
        "gla_b_a": nrm(ks[10], (L, GLA_HEADS * GLA_DK), 0.02),
        "gla_out_norm": gain(ks[11], (L, GLA_DV)),
        "swa_sinks": nrm(ks[12], (L, SWA_Q_HEADS), 0.5),
        "w_br_mla": nrm(ks[13], (L, MLA_WIDTH, D), MLA_WIDTH ** -0.5),
        "w_br_gla": nrm(ks[14], (L, GLA_WIDTH, D), GLA_WIDTH ** -0.5),
        "w_br_swa": nrm(ks[15], (L, SWA_WIDTH, D), SWA_WIDTH ** -0.5),
        "w_out": nrm(ks[16], (L, D, D), D ** -0.5),
        "ffn_norm": gain(ks[17], (L, D)),
        "w_router_group": nrm(ks[18], (L, D, N_GROUPS), D ** -0.5),
        "b_router_group": nrm(ks[19], (L, N_GROUPS), 0.01),
        "w_router_expert": nrm(ks[20], (L, D, N_EXPERTS), D ** -0.5),
        "b_router_expert": nrm(ks[21], (L, N_EXPERTS), 0.01),
        "w_gate": nrm(ks[22], (L, N_EXPERTS, D, EXPERT_FF), D ** -0.5),
        "w_up": nrm(ks[23], (L, N_EXPERTS, D, EXPERT_FF), D ** -0.5),
        "w_down": nrm(ks[24], (L, N_EXPERTS, EXPERT_FF, D), EXPERT_FF ** -0.5),
        "final_norm": gain(ks[25], (D,)),
    }


def reference(x, positions, attn_norm, w_in, gate_bias, mla_q_norm, mla_w_uq, mla_kv_norm, mla_w_ukv,
              gla_w_a2, gla_b_a, gla_out_norm, swa_sinks, w_br_mla, w_br_gla, w_br_swa, w_out,
              ffn_norm, w_router_group, b_router_group, w_router_expert, b_router_expert,
              w_gate, w_up, w_down, final_norm):
    B, S, D = x.shape
    for l in range(DEPTH):
        h = rms_norm(x, attn_norm[l])
        (mla_cq, mla_ckv, mla_kr, gla_q, gla_k, gla_v, gla_a, gla_g,
         swa_q, swa_k, swa_v, gate_logits) = split_columns(h @ w_in[l])
        y_mla = mla_attention(mla_cq, mla_ckv, mla_kr, positions, mla_q_norm[l], mla_w_uq[l],
                              mla_kv_norm[l], mla_w_ukv[l])
        y_gla = gla_attention(gla_q, gla_k, gla_v, gla_a, gla_g, gla_w_a2[l], gla_b_a[l], gla_out_norm[l])
        y_swa = swa_attention(swa_q, swa_k, swa_v, swa_sinks[l])
        gates = jax.nn.sigmoid(gate_logits.reshape(B, S, N_BRANCHES, D) + gate_bias[l])
        merged = (gates[:, :, 0] * (y_mla @ w_br_mla[l])
                  + gates[:, :, 1] * (y_gla @ w_br_gla[l])
                  + gates[:, :, 2] * (y_swa @ w_br_swa[l]))
        x = x + merged @ w_out[l]
        x = x + hier_moe(rms_norm(x, ffn_norm[l]), w_router_group[l], b_router_group[l],
                         w_router_expert[l], b_router_expert[l], w_gate[l], w_up[l], w_down[l])
    return rms_norm(x, final_norm)
```

```python
import functools
import math

import numpy as np
import jax
import jax.numpy as jnp
from jax import lax
from jax.experimental import pallas as pl
from jax.experimental.pallas import tpu as pltpu

F32 = jnp.float32
BF16 = jnp.bfloat16
HIGHEST = lax.Precision.HIGHEST

RMS_EPS = 1e-6
NEG_INF = -1e30
MLA_HEADS = 16
MLA_Q_LORA = 1024
MLA_KV_LORA = 512
MLA_NOPE_DIM = 128
MLA_ROPE_DIM = 64
MLA_V_DIM = 128
ROPE_THETA = 10000.0
GLA_HEADS = 8
GLA_DK = 128
GLA_DV = 256
GLA_GATE_RANK = 16
GLA_GATE_TAU = 16.0
SWA_Q_HEADS = 32
SWA_KV_HEADS = 4
SWA_HEAD_DIM = 64
SWA_WINDOW = 128
N_BRANCHES = 3
N_GROUPS = 4
EXPERTS_PER_GROUP = 8
TOP_K_IN_GROUP = 2

LANES = 128
MLA_HEAD_PAD = 256
GLA_SUB = 16
VMEM_LIMIT = 60 * 1024 * 1024


def _cparams(sem, vmem=VMEM_LIMIT):
    return pltpu.CompilerParams(dimension_semantics=sem, vmem_limit_bytes=vmem)


def _pick(n, cap, mult=LANES):
    best = None
    for t in range(mult, min(n, cap) + 1, mult):
        if n % t == 0:
            best = t
    assert best is not None, (n, cap, mult)
    return best


def _rmsnorm_body(x_ref, g_ref, o_ref):
    x = x_ref[...]
    y = x * lax.rsqrt(jnp.mean(x * x, axis=-1, keepdims=True) + RMS_EPS)
    o_ref[...] = (y * g_ref[...]).astype(o_ref.dtype)


def rmsnorm(x, gain, out_dtype):
    m, d = x.shape
    tm = _pick(m, 512, 8)
    return pl.pallas_call(
        _rmsnorm_body,
        out_shape=jax.ShapeDtypeStruct((m, d), out_dtype),
        grid=(m // tm,),
        in_specs=[pl.BlockSpec((tm, d), lambda i: (i, 0)),
                  pl.BlockSpec((1, d), lambda i: (0, 0))],
        out_specs=pl.BlockSpec((tm, d), lambda i: (i, 0)),
        compiler_params=_cparams(("parallel",)),
        name="rmsnorm",
    )(x, gain.reshape(1, d))


def _matmul_body(a_ref, w_ref, o_ref):
    o_ref[...] = jnp.dot(a_ref[...], w_ref[...], preferred_element_type=F32).astype(o_ref.dtype)


def _matmul_res_body(a_ref, w_ref, r_ref, o_ref):
    acc = jnp.dot(a_ref[...], w_ref[...], preferred_element_type=F32)
    o_ref[...] = (r_ref[...] + acc).astype(o_ref.dtype)


def matmul(a, w, out_dtype, residual=None, tm_cap=512, tn_cap=1024, name="matmul"):
    m, k = a.shape
    n = w.shape[1]
    tm = _pick(m, tm_cap, 16)
    tn = _pick(n, tn_cap)
    in_specs = [pl.BlockSpec((tm, k), lambda j, i: (i, 0)),
                pl.BlockSpec((k, tn), lambda j, i: (0, j))]
    args = [a, w]
    body = _matmul_body
    if residual is not None:
        in_specs.append(pl.BlockSpec((tm, tn), lambda j, i: (i, j)))
        args.append(residual)
        body = _matmul_res_body
    return pl.pallas_call(
        body,
        out_shape=jax.ShapeDtypeStruct((m, n), out_dtype),
        grid=(n // tn, m // tm),
        in_specs=in_specs,
        out_specs=pl.BlockSpec((tm, tn), lambda j, i: (i, j)),
        compiler_params=_cparams(("parallel", "parallel")),
        name=name,
    )(*args)


def _rope_tables(pos_ref, freq_ref):
    ang = pos_ref[...].astype(F32) * freq_ref[...]
    return jnp.cos(ang), jnp.sin(ang)


def _mla_q_body(c_ref, pos_ref, g_ref, freq_ref, w1_ref, w2_ref, q_ref, *, heads, scale):
    c = c_ref[...]
    cn = (c * lax.rsqrt(jnp.mean(c * c, axis=-1, keepdims=True) + RMS_EPS) * g_ref[...]).astype(BF16)
    q = jnp.dot(cn, w1_ref[...], preferred_element_type=F32)
    qp = jnp.dot(cn, w2_ref[...], preferred_element_type=F32)
    cs, sn = _rope_tables(pos_ref, freq_ref)
    for h in range(heads):
        a = h * MLA_HEAD_PAD
        q_ref[:, a:a + LANES] = (q[:, a:a + LANES] * scale).astype(BF16)
        rot = q[:, a + LANES:a + 2 * LANES] * cs + qp[:, h * LANES:(h + 1) * LANES] * sn
        q_ref[:, a + LANES:a + 2 * LANES] = (rot * scale).astype(BF16)


def _mla_kv_body(c_ref, kr_ref, krp_ref, pos_ref, g_ref, freq_ref, w_ref, k_ref, v_ref, *, heads):
    c = c_ref[...]
    cn = (c * lax.rsqrt(jnp.mean(c * c, axis=-1, keepdims=True) + RMS_EPS) * g_ref[...]).astype(BF16)
    kv = jnp.dot(cn, w_ref[...], preferred_element_type=F32)
    cs, sn = _rope_tables(pos_ref, freq_ref)
    krot = (kr_ref[...] * cs + krp_ref[...] * sn).astype(BF16)
    for h in range(heads):
        a = h * MLA_HEAD_PAD
        k_ref[:, a:a + LANES] = kv[:, a:a + LANES].astype(BF16)
        k_ref[:, a + LANES:a + 2 * LANES] = krot
        v_ref[:, h * LANES:(h + 1) * LANES] = kv[:, a + LANES:a + 2 * LANES].astype(BF16)


def mla_prep(z_a, pos, q_norm, kv_norm, freq, w_q1, w_q2, w_kv, heads):
    m = z_a.shape[0]
    ql, kvl = w_q1.shape[0], w_kv.shape[0]
    tm = _pick(m, 256, 8)
    scale = (MLA_NOPE_DIM + MLA_ROPE_DIM) ** -0.5
    row = lambda i: (i, 0)
    const = lambda i: (0, 0)
    q = pl.pallas_call(
        functools.partial(_mla_q_body, heads=heads, scale=scale),
        out_shape=jax.ShapeDtypeStruct((m, heads * MLA_HEAD_PAD), BF16),
        grid=(m // tm,),
        in_specs=[pl.BlockSpec((tm, ql), row),
                  pl.BlockSpec((tm, 1), row),
                  pl.BlockSpec((1, ql), const),
                  pl.BlockSpec((1, LANES), const),
                  pl.BlockSpec(w_q1.shape, const),
                  pl.BlockSpec(w_q2.shape, const)],
        out_specs=pl.BlockSpec((tm, heads * MLA_HEAD_PAD), row),
        compiler_params=_cparams(("parallel",)),
        name="mla_q_prep",
    )(z_a, pos, q_norm.reshape(1, ql), freq, w_q1, w_q2)
    kr_blk = (ql + kvl) // LANES
    k, v = pl.pallas_call(
        functools.partial(_mla_kv_body, heads=heads),
        out_shape=(jax.ShapeDtypeStruct((m, heads * MLA_HEAD_PAD), BF16),
                   jax.ShapeDtypeStruct((m, heads * LANES), BF16)),
        grid=(m // tm,),
        in_specs=[pl.BlockSpec((tm, kvl), lambda i: (i, ql // kvl)),
                  pl.BlockSpec((tm, LANES), lambda i: (i, kr_blk)),
                  pl.BlockSpec((tm, LANES), lambda i: (i, kr_blk + 1)),
                  pl.BlockSpec((tm, 1), row),
                  pl.BlockSpec((1, kvl), const),
                  pl.BlockSpec((1, LANES), const),
                  pl.BlockSpec(w_kv.shape, const)],
        out_specs=(pl.BlockSpec((tm, heads * MLA_HEAD_PAD), row),
                   pl.BlockSpec((tm, heads * LANES), row)),
        compiler_params=_cparams(("parallel",)),
        name="mla_kv_prep",
    )(z_a, z_a, z_a, pos, kv_norm.reshape(1, kvl), freq, w_kv)
    return q, k, v


def _mla_attn_body(q_ref, k_ref, v_ref, o_ref, m_ref, l_ref, acc_ref, *, tq, tk):
    qi = pl.program_id(2)
    ki = pl.program_id(3)

    @pl.when(ki == 0)
    def _():
        m_ref[...] = jnp.full_like(m_ref, NEG_INF)
        l_ref[...] = jnp.zeros_like(l_ref)
        acc_ref[...] = jnp.zeros_like(acc_ref)

    @pl.when(ki * tk <= qi * tq + tq - 1)
    def _():
        s = lax.dot_general(q_ref[...], k_ref[...], (((1,), (1,)), ((), ())),
                            preferred_element_type=F32)
        rows = qi * tq + lax.broadcasted_iota(jnp.int32, (tq, tk), 0)
        cols = ki * tk + lax.broadcasted_iota(jnp.int32, (tq, tk), 1)
        s = jnp.where(cols <= rows, s, NEG_INF)
        m_old = m_ref[...]
        m_new = jnp.maximum(m_old, jnp.max(s, axis=-1, keepdims=True))
        alpha = jnp.exp(m_old - m_new)
        p = jnp.exp(s - m_new)
        l_ref[...] = alpha * l_ref[...] + jnp.sum(p, axis=-1, keepdims=True)
        acc_ref[...] = alpha * acc_ref[...] + jnp.dot(p.astype(BF16), v_ref[...],
                                                      preferred_element_type=F32)
        m_ref[...] = m_new

    @pl.when(ki == pl.num_programs(3) - 1)
    def _():
        o_ref[...] = (acc_ref[...] / l_ref[...]).astype(o_ref.dtype)


def mla_attention(q, k, v, batch, seq, heads):
    tq = _pick(seq, 512, 16)
    tk = tq
    nq, nk = seq // tq, seq // tk

    def kv_map(b, h, qi, ki):
        last = (qi * tq + tq - 1) // tk
        return (b * nk + jnp.minimum(ki, last), h)

    return pl.pallas_call(
        functools.partial(_mla_attn_body, tq=tq, tk=tk),
        out_shape=jax.ShapeDtypeStruct((batch * seq, heads * MLA_V_DIM), BF16),
        grid=(batch, heads, nq, nk),
        in_specs=[pl.BlockSpec((tq, MLA_HEAD_PAD), lambda b, h, qi, ki: (b * nq + qi, h)),
                  pl.BlockSpec((tk, MLA_HEAD_PAD), kv_map),
                  pl.BlockSpec((tk, MLA_V_DIM), kv_map)],
        out_specs=pl.BlockSpec((tq, MLA_V_DIM), lambda b, h, qi, ki: (b * nq + qi, h)),
        scratch_shapes=[pltpu.VMEM((tq, 1), F32), pltpu.VMEM((tq, 1), F32),
                        pltpu.VMEM((tq, MLA_V_DIM), F32)],
        compiler_params=_cparams(("parallel", "parallel", "parallel", "arbitrary")),
        name="mla_attention",
    )(q, k, v)


def _swa_body(sink_ref, q_ref, kp_ref, kc_ref, vp_ref, vc_ref, o_ref, *, group):
    w = SWA_WINDOW
    n = pl.program_id(1)
    scale = SWA_HEAD_DIM ** -0.5
    lane = lax.broadcasted_iota(jnp.int32, (2 * w, LANES), 1)
    lo = lane < SWA_HEAD_DIM
    qlane_lo = lax.broadcasted_iota(jnp.int32, (w, LANES), 1) < SWA_HEAD_DIM
    ri = lax.broadcasted_iota(jnp.int32, (2 * w, 2 * w), 0) % w
    cj = lax.broadcasted_iota(jnp.int32, (2 * w, 2 * w), 1)
    valid = (cj > ri) & (cj <= ri + w) & ((cj >= w) | (n > 0))
    row_first = lax.broadcasted_iota(jnp.int32, (2 * w, 1), 0) < w
    kfull = jnp.concatenate([kp_ref[...], kc_ref[...]], axis=0).astype(F32)
    vfull = jnp.concatenate([vp_ref[...], vc_ref[...]], axis=0).astype(F32)
    n_chunks = kfull.shape[1] // LANES
    for c in range(n_chunks):
        kc = kfull[:, c * LANES:(c + 1) * LANES]
        vc = vfull[:, c * LANES:(c + 1) * LANES]
        kr = pltpu.roll(kc, SWA_HEAD_DIM, axis=1)
        vr = pltpu.roll(vc, SWA_HEAD_DIM, axis=1)
        for sub in range(2):
            hk = 2 * c + sub
            k2 = (jnp.where(lo, kc, kr) if sub == 0 else jnp.where(lo, kr, kc)).astype(BF16)
            v2 = (jnp.where(lo, vc, vr) if sub == 0 else jnp.where(lo, vr, vc)).astype(BF16)
            for j in range(group // 2):
                chunk = hk * (group // 2) + j
                hq = hk * group + 2 * j
                qp = q_ref[:, chunk * LANES:(chunk + 1) * LANES].astype(F32)
                qa = jnp.where(qlane_lo, qp, 0.0)
                qb = jnp.where(qlane_lo, 0.0, qp)
                qs = jnp.concatenate([qa, qb], axis=0).astype(BF16)
                s = lax.dot_general(qs, k2, (((1,), (1,)), ((), ())),
                                    preferred_element_type=F32) * scale
                s = jnp.where(valid, s, NEG_INF)
                sink = jnp.where(row_first, sink_ref[hq], sink_ref[hq + 1])
                mx = jnp.maximum(jnp.max(s, axis=-1, keepdims=True), sink)
                p = jnp.exp(s - mx)
                denom = jnp.sum(p, axis=-1, keepdims=True) + jnp.exp(sink - mx)
                o2 = jnp.dot((p / denom).astype(BF16), v2, preferred_element_type=F32)
                o_ref[:, chunk * LANES:(chunk + 1) * LANES] = jnp.where(
                    qlane_lo, o2[:w], o2[w:]).astype(o_ref.dtype)


def swa_attention(z_s, sinks, batch, seq, q_heads, kv_heads):
    w = SWA_WINDOW
    nb = seq // w
    qw = q_heads * SWA_HEAD_DIM
    kw = kv_heads * SWA_HEAD_DIM
    assert qw % kw == 0 and kw % LANES == 0 and (q_heads // kv_heads) % 2 == 0
    kb = qw // kw
    cur = lambda blk: (lambda b, n, s: (b * nb + n, blk))
    prev = lambda blk: (lambda b, n, s: (b * nb + jnp.maximum(n - 1, 0), blk))
    return pl.pallas_call(
        functools.partial(_swa_body, group=q_heads // kv_heads),
        out_shape=jax.ShapeDtypeStruct((batch * seq, qw), BF16),
        grid_spec=pltpu.PrefetchScalarGridSpec(
            num_scalar_prefetch=1,
            grid=(batch, nb),
            in_specs=[pl.BlockSpec((w, qw), cur(0)),
                      pl.BlockSpec((w, kw), prev(kb)),
                      pl.BlockSpec((w, kw), cur(kb)),
                      pl.BlockSpec((w, kw), prev(kb + 1)),
                      pl.BlockSpec((w, kw), cur(kb + 1))],
            out_specs=pl.BlockSpec((w, qw), cur(0))),
        compiler_params=_cparams(("parallel", "arbitrary")),
        name="swa_attention",
    )(sinks, z_s, z_s, z_s, z_s, z_s)


def _gla_body(q_ref, k_ref, v_ref, g_ref, a_ref, wa_ref, ba_ref, gn_ref, o_ref,
              state_ref, b_ref, *, chunk, tokens):
    c_sz = chunk
    t = pl.program_id(2)

    @pl.when(t == 0)
    def _():
        state_ref[...] = jnp.zeros_like(state_ref)

    z = jnp.dot(a_ref[...], wa_ref[...], precision=HIGHEST, preferred_element_type=F32) + ba_ref[...]
    log_a = (jnp.minimum(z, 0.0) - jnp.log(1.0 + jnp.exp(-jnp.abs(z)))) * (1.0 / GLA_GATE_TAU)
    ri = lax.broadcasted_iota(jnp.int32, (tokens, tokens), 0)
    ci = lax.broadcasted_iota(jnp.int32, (tokens, tokens), 1)
    tri = ((ci <= ri) & (ci // c_sz == ri // c_sz)).astype(F32)
    b_ref[...] = jnp.dot(tri, log_a, precision=HIGHEST, preferred_element_type=F32)

    qscale = GLA_DK ** -0.5
    sub = GLA_SUB
    n_sub = c_sz // sub
    rowid = lax.broadcasted_iota(jnp.int32, (sub, 1), 0)

    def chunk_step(c, carry):
        c0 = pl.multiple_of(c * c_sz, c_sz)
        b = b_ref[pl.ds(c0, c_sz), :]
        q = q_ref[pl.ds(c0, c_sz), :].astype(F32) * qscale
        k = k_ref[pl.ds(c0, c_sz), :].astype(F32)
        v = v_ref[pl.ds(c0, c_sz), :]
        vf = v.astype(F32)
        state_t = state_ref[...]
        o_inter = lax.dot_general((q * jnp.exp(b)).astype(BF16), state_t.astype(BF16),
                                  (((1,), (1,)), ((), ())), preferred_element_type=F32)
        pieces = []
        for i in range(n_sub):
            r0 = i * sub
            b_i, q_i, k_i, v_i = b[r0:r0 + sub], q[r0:r0 + sub], k[r0:r0 + sub], vf[r0:r0 + sub]
            acc = jnp.zeros((sub, vf.shape[1]), F32)
            for j in range(sub):
                wgt = jnp.exp(jnp.minimum(b_i - b_i[j:j + 1, :], 0.0))
                col = jnp.sum(q_i * k_i[j:j + 1, :] * wgt, axis=-1, keepdims=True)
                col = jnp.where(rowid >= j, col, 0.0)
                acc = acc + col * v_i[j:j + 1, :]
            if i > 0:
                ref_b = b[r0 - 1:r0, :]
                qh = (q_i * jnp.exp(b_i - ref_b)).astype(BF16)
                kh = (k[:r0] * jnp.exp(ref_b - b[:r0])).astype(BF16)
                att = lax.dot_general(qh, kh, (((1,), (1,)), ((), ())), preferred_element_type=F32)
                acc = acc + jnp.dot(att.astype(BF16), v[:r0], preferred_element_type=F32)
            pieces.append(acc)
        o = o_inter + jnp.concatenate(pieces, axis=0)
        b_last = b[c_sz - 1:c_sz, :]
        k_dec = (k * jnp.exp(b_last - b)).astype(BF16)
        upd = lax.dot_general(v, k_dec, (((0,), (0,)), ((), ())), preferred_element_type=F32)
        state_ref[...] = state_t * jnp.exp(b_last) + upd
        y = o * lax.rsqrt(jnp.mean(o * o, axis=-1, keepdims=True) + RMS_EPS) * gn_ref[...]
        g = g_ref[pl.ds(c0, c_sz), :].astype(F32)
        y = y * (g / (1.0 + jnp.exp(-g)))
        o_ref[pl.ds(c0, c_sz), :] = y.astype(o_ref.dtype)
        return carry

    lax.fori_loop(0, tokens // c_sz, chunk_step, 0)


def gla_attention(z_g, z_a, a_blk, w_a2p, b_a, out_norm, batch, seq, heads, chunk=64):
    dk, dv = GLA_DK, GLA_DV
    tokens = _pick(seq, 256, chunk)
    nt = seq // tokens
    kq = heads * dk // dk
    kv_off = 2 * heads * dk // dv
    g_off = kv_off + heads
    row = lambda blk_fn: (lambda b, h, t: (b * nt + t, blk_fn(h)))
    return pl.pallas_call(
        functools.partial(_gla_body, chunk=chunk, tokens=tokens),
        out_shape=jax.ShapeDtypeStruct((batch * seq, heads * dv), BF16),
        grid=(batch, heads, nt),
        in_specs=[pl.BlockSpec((tokens, dk), row(lambda h: h)),
                  pl.BlockSpec((tokens, dk), row(lambda h: kq + h)),
                  pl.BlockSpec((tokens, dv), row(lambda h: kv_off + h)),
                  pl.BlockSpec((tokens, dv), row(lambda h: g_off + h)),
                  pl.BlockSpec((tokens, LANES), row(lambda h: a_blk)),
                  pl.BlockSpec((LANES, dk), lambda b, h, t: (0, h)),
                  pl.BlockSpec((1, dk), lambda b, h, t: (0, h)),
                  pl.BlockSpec((1, dv), lambda b, h, t: (0, 0))],
        out_specs=pl.BlockSpec((tokens, dv), row(lambda h: h)),
        scratch_shapes=[pltpu.VMEM((dv, dk), F32), pltpu.VMEM((tokens, dk), F32)],
        compiler_params=_cparams(("parallel", "parallel", "arbitrary")),
        name="gla_attention",
    )(z_g, z_g, z_g, z_g, z_a, w_a2p, b_a.reshape(1, -1), out_norm.reshape(1, dv))


def _merge_body(h_ref, y0_ref, y1_ref, y2_ref, wg_ref, wb_ref, bias_ref, o_ref):
    h = h_ref[...]
    acc = None
    for b, y_ref in enumerate((y0_ref, y1_ref, y2_ref)):
        logit = jnp.dot(h, wg_ref[b], preferred_element_type=F32) + bias_ref[b:b + 1, :]
        gate = 1.0 / (1.0 + jnp.exp(-logit))
        term = gate * jnp.dot(y_ref[...], wb_ref[b], preferred_element_type=F32)
        acc = term if acc is None else acc + term
    o_ref[...] = acc.astype(o_ref.dtype)


def merge_branches(h, ys, w_gate3, w_br3, bias3):
    m, d = h.shape
    bw = ys[0].shape[1]
    tm = _pick(m, 512, 16)
    tn = _pick(d, 256)
    return pl.pallas_call(
        _merge_body,
        out_shape=jax.ShapeDtypeStruct((m, d), BF16),
        grid=(d // tn, m // tm),
        in_specs=[pl.BlockSpec((tm, d), lambda j, i: (i, 0))]
        + [pl.BlockSpec((tm, bw), lambda j, i: (i, 0))] * 3
        + [pl.BlockSpec((N_BRANCHES, d, tn), lambda j, i: (0, 0, j)),
           pl.BlockSpec((N_BRANCHES, bw, tn), lambda j, i: (0, 0, j)),
           pl.BlockSpec((N_BRANCHES, tn), lambda j, i: (0, j))],
        out_specs=pl.BlockSpec((tm, tn), lambda j, i: (i, j)),
        compiler_params=_cparams(("parallel", "parallel")),
        name="merge_branches",
    )(h, *ys, w_gate3, w_br3, bias3)


def _router_body(x_ref, g_ref, w_ref, b_ref, h_ref, e_ref, wt_ref, rk_ref, cnt_ref, carry_ref,
                 *, n_exp, n_grp, per_grp):
    i = pl.program_id(0)

    @pl.when(i == 0)
    def _():
        carry_ref[...] = jnp.zeros_like(carry_ref)

    x = x_ref[...]
    tm = x.shape[0]
    hn = x * lax.rsqrt(jnp.mean(x * x, axis=-1, keepdims=True) + RMS_EPS) * g_ref[...]
    h_ref[...] = hn
    logits = jnp.dot(hn, w_ref[...], precision=HIGHEST, preferred_element_type=F32) + b_ref[...]
    lane = lax.broadcasted_iota(jnp.int32, (tm, LANES), 1)
    big = jnp.int32(LANES)

    def top1(vals):
        mx = jnp.max(vals, axis=-1, keepdims=True)
        idx = jnp.min(jnp.where(vals == mx, lane, big), axis=-1, keepdims=True)
        return mx, idx

    gl = jnp.where((lane >= n_exp) & (lane < n_exp + n_grp), logits, NEG_INF)
    gmax, gidx = top1(gl)
    g_top_p = 1.0 / jnp.sum(jnp.exp(gl - gmax), axis=-1, keepdims=True)
    grp = gidx - n_exp
    el = jnp.where((lane >= grp * per_grp) & (lane < (grp + 1) * per_grp), logits, NEG_INF)
    m1, i1 = top1(el)
    m2, i2 = top1(jnp.where(lane == i1, NEG_INF, el))
    e21 = jnp.exp(m2 - m1)
    w1 = g_top_p / (1.0 + e21)
    w2 = w1 * e21
    e_ref[...] = jnp.where(lane == 0, i1, jnp.where(lane == 1, i2, 0))[:, :8]
    wt_ref[...] = jnp.where(lane == 0, w1, jnp.where(lane == 1, w2, 0.0))[:, :8]

    oh1 = (lane == i1).astype(F32)
    oh2 = (lane == i2).astype(F32)
    ri = lax.broadcasted_iota(jnp.int32, (tm, tm), 0)
    ci = lax.broadcasted_iota(jnp.int32, (tm, tm), 1)
    strict = (ci < ri).astype(BF16)
    cum1 = jnp.dot(strict, oh1.astype(BF16), preferred_element_type=F32)
    cum2 = jnp.dot(strict, oh2.astype(BF16), preferred_element_type=F32)
    tot1 = jnp.sum(oh1, axis=0, keepdims=True)
    tot2 = jnp.sum(oh2, axis=0, keepdims=True)
    carry = carry_ref[...]
    r1 = jnp.sum(oh1 * (cum1 + carry), axis=-1, keepdims=True)
    r2 = jnp.sum(oh2 * (cum2 + carry + tot1), axis=-1, keepdims=True)
    rk_ref[...] = jnp.where(lane == 0, r1, jnp.where(lane == 1, r2, 0.0))[:, :8].astype(jnp.int32)
    carry = carry + tot1 + tot2
    carry_ref[...] = carry
    cnt_ref[...] = jnp.broadcast_to(carry, cnt_ref.shape).astype(jnp.int32)


def moe_router(x, gain, w_r, b_r, n_exp, n_grp):
    m, d = x.shape
    tm = _pick(m, 256, 8)
    row = lambda i: (i, 0)
    const = lambda i: (0, 0)
    return pl.pallas_call(
        functools.partial(_router_body, n_exp=n_exp, n_grp=n_grp, per_grp=n_exp // n_grp),
        out_shape=(jax.ShapeDtypeStruct((m, d), F32),
                   jax.ShapeDtypeStruct((m, 8), jnp.int32),
                   jax.ShapeDtypeStruct((m, 8), F32),
                   jax.ShapeDtypeStruct((m, 8), jnp.int32),
                   jax.ShapeDtypeStruct((8, LANES), jnp.int32)),
        grid=(m // tm,),
        in_specs=[pl.BlockSpec((tm, d), row), pl.BlockSpec((1, d), const),
                  pl.BlockSpec((d, LANES), const), pl.BlockSpec((1, LANES), const)],
        out_specs=(pl.BlockSpec((tm, d), row), pl.BlockSpec((tm, 8), row),
                   pl.BlockSpec((tm, 8), row), pl.BlockSpec((tm, 8), row),
                   pl.BlockSpec((8, LANES), const)),
        scratch_shapes=[pltpu.VMEM((1, LANES), F32)],
        compiler_params=_cparams(("arbitrary",)),
        name="moe_router",
    )(x, gain.reshape(1, d), w_r, b_r)


def _dispatch_body(p1_ref, p2_ref, h_ref, xs_in_ref, xs_ref, sem, *, tokens):
    del xs_in_ref
    base = pl.program_id(0) * tokens

    def copies(t):
        src = h_ref.at[pl.ds(base + t, 1), :]
        return (pltpu.make_async_copy(src, xs_ref.at[pl.ds(p1_ref[base + t], 1), :], sem),
                pltpu.make_async_copy(src, xs_ref.at[pl.ds(p2_ref[base + t], 1), :], sem))

    def start(t, c):
        a, b = copies(t)
        a.start()
        b.start()
        return c

    def wait(t, c):
        a, b = copies(t)
        a.wait()
        b.wait()
        return c

    lax.fori_loop(0, tokens, start, 0)
    lax.fori_loop(0, tokens, wait, 0)


def moe_dispatch(h, pos1, pos2, rows):
    m, d = h.shape
    tokens = _pick(m, 256, 8)
    xs0 = jnp.zeros((rows, d), F32)
    return pl.pallas_call(
        functools.partial(_dispatch_body, tokens=tokens),
        out_shape=jax.ShapeDtypeStruct((rows, d), F32),
        grid_spec=pltpu.PrefetchScalarGridSpec(
            num_scalar_prefetch=2,
            grid=(m // tokens,),
            in_specs=[pl.BlockSpec(memory_space=pl.ANY), pl.BlockSpec(memory_space=pl.ANY)],
            out_specs=pl.BlockSpec(memory_space=pl.ANY),
            scratch_shapes=[pltpu.SemaphoreType.DMA(())]),
        input_output_aliases={3: 0},
        compiler_params=_cparams(("arbitrary",)),
        name="moe_dispatch",
    )(pos1, pos2, h, xs0)


def _experts_body(te_ref, tv_ref, x_ref, wg_ref, wu_ref, wd_ref, o_ref):
    i = pl.program_id(0)

    @pl.when(tv_ref[i] > 0)
    def _():
        x = x_ref[...].astype(BF16)
        a = jnp.dot(x, wg_ref[...], preferred_element_type=F32)
        u = jnp.dot(x, wu_ref[...], preferred_element_type=F32)
        hid = (a / (1.0 + jnp.exp(-a))) * u
        o_ref[...] = jnp.dot(hid.astype(BF16), wd_ref[...], preferred_element_type=F32)

    @pl.when(tv_ref[i] == 0)
    def _():
        o_ref[...] = jnp.zeros_like(o_ref)


def moe_experts(xs, tile_expert, tile_src, tile_valid, w_gate, w_up, w_down, tm):
    rows, d = xs.shape
    ff = w_gate.shape[2]
    n_tiles = rows // tm
    return pl.pallas_call(
        lambda te, ts, tv, *refs: _experts_body(te, tv, *refs),
        out_shape=jax.ShapeDtypeStruct((rows, d), F32),
        grid_spec=pltpu.PrefetchScalarGridSpec(
            num_scalar_prefetch=3,
            grid=(n_tiles,),
            in_specs=[pl.BlockSpec((tm, d), lambda i, te, ts, tv: (ts[i], 0)),
                      pl.BlockSpec((None, d, ff), lambda i, te, ts, tv: (te[i], 0, 0)),
                      pl.BlockSpec((None, d, ff), lambda i, te, ts, tv: (te[i], 0, 0)),
                      pl.BlockSpec((None, ff, d), lambda i, te, ts, tv: (te[i], 0, 0))],
            out_specs=pl.BlockSpec((tm, d), lambda i, te, ts, tv: (i, 0))),
        compiler_params=_cparams(("arbitrary",), vmem=62 * 1024 * 1024),
        name="moe_experts",
    )(tile_expert, tile_src, tile_valid, xs, w_gate, w_up, w_down)


def _combine_body(p1_ref, p2_ref, x_ref, wt_ref, g_ref, ys_ref, *rest, tokens, emit_x):
    if emit_x:
        xo_ref, no_ref, buf_ref, sem = rest
    else:
        no_ref, buf_ref, sem = rest
    base = pl.program_id(0) * tokens

    def copies(t):
        return (pltpu.make_async_copy(ys_ref.at[pl.ds(p1_ref[base + t], 1), :],
                                      buf_ref.at[0, pl.ds(t, 1), :], sem),
                pltpu.make_async_copy(ys_ref.at[pl.ds(p2_ref[base + t], 1), :],
                                      buf_ref.at[1, pl.ds(t, 1), :], sem))

    def start(t, c):
        a, b = copies(t)
        a.start()
        b.start()
        return c

    def wait(t, c):
        a, b = copies(t)
        a.wait()
        b.wait()
        return c

    lax.fori_loop(0, tokens, start, 0)
    lax.fori_loop(0, tokens, wait, 0)
    wt = wt_ref[...]
    x = x_ref[...] + wt[:, 0:1] * buf_ref[0] + wt[:, 1:2] * buf_ref[1]
    if emit_x:
        xo_ref[...] = x
    y = x * lax.rsqrt(jnp.mean(x * x, axis=-1, keepdims=True) + RMS_EPS)
    no_ref[...] = (y * g_ref[...]).astype(no_ref.dtype)


def moe_combine(x, ys, pos1, pos2, wts, gain, norm_dtype, emit_x):
    m, d = x.shape
    tokens = _pick(m, 256, 8)
    row = lambda i, p1, p2: (i, 0)
    out_shape = [jax.ShapeDtypeStruct((m, d), norm_dtype)]
    out_specs = [pl.BlockSpec((tokens, d), row)]
    if emit_x:
        out_shape.insert(0, jax.ShapeDtypeStruct((m, d), F32))
        out_specs.insert(0, pl.BlockSpec((tokens, d), row))
    return pl.pallas_call(
        functools.partial(_combine_body, tokens=tokens, emit_x=emit_x),
        out_shape=tuple(out_shape),
        grid_spec=pltpu.PrefetchScalarGridSpec(
            num_scalar_prefetch=2,
            grid=(m // tokens,),
            in_specs=[pl.BlockSpec((tokens, d), row),
                      pl.BlockSpec((tokens, 8), row),
                      pl.BlockSpec((1, d), lambda i, p1, p2: (0, 0)),
                      pl.BlockSpec(memory_space=pl.ANY)],
            out_specs=tuple(out_specs),
            scratch_shapes=[pltpu.VMEM((2, tokens, d), F32), pltpu.SemaphoreType.DMA(())]),
        compiler_params=_cparams(("arbitrary",)),
        name="moe_combine",
    )(pos1, pos2, x, wts, gain.reshape(1, d), ys)


def _rope_partner(w_rot):
    half = MLA_ROPE_DIM // 2
    return jnp.concatenate([-w_rot[..., half:], w_rot[..., :half]], axis=-1)


def _layer_weights(w_in, w_uq, heads_mla):
    d = w_in.shape[0]
    gla_qk = GLA_HEADS * GLA_DK
    gla_w = GLA_HEADS * GLA_DV
    swa_q = SWA_Q_HEADS * SWA_HEAD_DIM
    swa_kv = SWA_KV_HEADS * SWA_HEAD_DIM
    sizes = (MLA_Q_LORA, MLA_KV_LORA, MLA_ROPE_DIM, gla_qk, gla_qk, gla_w, GLA_GATE_RANK, gla_w,
             swa_q, swa_kv, swa_kv, N_BRANCHES * d)
    offs = np.concatenate([[0], np.cumsum(sizes)])
    seg = lambda i: w_in[:, offs[i]:offs[i + 1]]
    zpad = lambda n: jnp.zeros((d, n), w_in.dtype)
    kr = seg(2)
    w_a = jnp.concatenate([seg(0), seg(1), kr, zpad(LANES - MLA_ROPE_DIM),
                           _rope_partner(kr), zpad(LANES - MLA_ROPE_DIM),
                           seg(6), zpad(LANES - GLA_GATE_RANK)], axis=1).astype(BF16)
    w_g = jnp.concatenate([seg(3), seg(4), seg(5), seg(7)], axis=1).astype(BF16)
    w_s = w_in[:, offs[8]:offs[11]].astype(BF16)
    w_gate3 = jnp.transpose(seg(11).reshape(d, N_BRANCHES, d), (1, 0, 2)).astype(BF16)
    qk = MLA_NOPE_DIM + MLA_ROPE_DIM
    wq = w_uq.reshape(-1, heads_mla, qk)
    pad = jnp.zeros((wq.shape[0], heads_mla, MLA_HEAD_PAD - qk), w_uq.dtype)
    w_q1 = jnp.concatenate([wq, pad], axis=-1).reshape(wq.shape[0], -1).astype(BF16)
    rot = wq[..., MLA_NOPE_DIM:]
    pad2 = jnp.zeros((wq.shape[0], heads_mla, LANES - MLA_ROPE_DIM), w_uq.dtype)
    w_q2 = jnp.concatenate([_rope_partner(rot), pad2], axis=-1).reshape(wq.shape[0], -1).astype(BF16)
    return w_a, w_g, w_s, w_gate3, w_q1, w_q2


def _moe_layer(x, gain, w_rg, b_rg, w_re, b_re, w_gate, w_up, w_down, next_gain, norm_dtype, emit_x):
    m, d = x.shape
    n_exp, n_grp = w_re.shape[1], w_rg.shape[1]
    w_r = jnp.concatenate([w_re, w_rg, jnp.zeros((d, LANES - n_exp - n_grp), F32)], axis=1)
    b_r = jnp.concatenate([b_re, b_rg, jnp.zeros((LANES - n_exp - n_grp,), F32)]).reshape(1, LANES)
    hn, e_idx, wts, rank, counts = moe_router(x, gain, w_r, b_r, n_exp, n_grp)
    tm = 256 if m >= 2048 else 64
    cnt = counts[0, :n_exp]
    padded = ((cnt + tm - 1) // tm) * tm
    ends = jnp.cumsum(padded)
    offs = ends - padded
    n_tiles = (TOP_K_IN_GROUP * m) // tm + n_exp
    tile_start = jnp.arange(n_tiles, dtype=jnp.int32) * tm
    tile_valid = (tile_start < ends[-1]).astype(jnp.int32)
    last_tile = jnp.maximum(ends[-1] // tm - 1, 0).astype(jnp.int32)
    tile_src = jnp.minimum(jnp.arange(n_tiles, dtype=jnp.int32), last_tile)
    tile_expert = jnp.minimum(jnp.searchsorted(ends, tile_src * tm, side="right"), n_exp - 1).astype(jnp.int32)
    pos = offs[e_idx[:, :2]].astype(jnp.int32) + rank[:, :2]
    pos1, pos2 = pos[:, 0], pos[:, 1]
    xs = moe_dispatch(hn, pos1, pos2, n_tiles * tm)
    ys = moe_experts(xs, tile_expert, tile_src, tile_valid,
                     w_gate.astype(BF16), w_up.astype(BF16), w_down.astype(BF16), tm)
    return moe_combine(x, ys, pos1, pos2, wts, next_gain, norm_dtype, emit_x)


def kernel(x, positions, attn_norm, w_in, gate_bias, mla_q_norm, mla_w_uq, mla_kv_norm, mla_w_ukv,
           gla_w_a2, gla_b_a, gla_out_norm, swa_sinks, w_br_mla, w_br_gla, w_br_swa, w_out,
           ffn_norm, w_router_group, b_router_group, w_router_expert, b_router_expert,
           w_gate, w_up, w_down, final_norm):
    batch, seq, d = x.shape
    depth = w_in.shape[0]
    m = batch * seq
    heads_mla = mla_w_uq.shape[2] // (MLA_NOPE_DIM + MLA_ROPE_DIM)
    xf = x.reshape(m, d)
    pos = positions.reshape(m, 1).astype(jnp.int32)
    half = MLA_ROPE_DIM // 2
    inv_freq = ROPE_THETA ** (-jnp.arange(0, MLA_ROPE_DIM, 2, dtype=F32) / MLA_ROPE_DIM)
    freq = jnp.concatenate([inv_freq, inv_freq, jnp.zeros((LANES - 2 * half,), F32)]).reshape(1, LANES)
    a_blk = (MLA_Q_LORA + MLA_KV_LORA) // LANES + 2

    h = rmsnorm(xf, attn_norm[0], BF16)
    out = None
    for l in range(depth):
        w_a, w_g, w_s, w_gate3, w_q1, w_q2 = _layer_weights(w_in[l], mla_w_uq[l], heads_mla)
        z_a = matmul(h, w_a, F32, name="in_proj_mla")
        z_g = matmul(h, w_g, BF16, name="in_proj_gla")
        z_s = matmul(h, w_s, BF16, name="in_proj_swa")
        q, k, v = mla_prep(z_a, pos, mla_q_norm[l], mla_kv_norm[l], freq, w_q1, w_q2,
                           mla_w_ukv[l].astype(BF16), heads_mla)
        y_mla = mla_attention(q, k, v, batch, seq, heads_mla)
        w_a2p = jnp.concatenate([gla_w_a2[l], jnp.zeros((LANES - GLA_GATE_RANK, gla_w_a2.shape[2]), F32)], axis=0)
        y_gla = gla_attention(z_g, z_a, a_blk, w_a2p, gla_b_a[l], gla_out_norm[l], batch, seq, GLA_HEADS)
        y_swa = swa_attention(z_s, swa_sinks[l], batch, seq, SWA_Q_HEADS, SWA_KV_HEADS)
        w_br3 = jnp.stack([w_br_mla[l], w_br_gla[l], w_br_swa[l]]).astype(BF16)
        merged = merge_branches(h, (y_mla, y_gla, y_swa), w_gate3, w_br3, gate_bias[l])
        xf = matmul(merged, w_out[l].astype(BF16), F32, residual=xf, name="out_proj")
        last = l == depth - 1
        res = _moe_layer(xf, ffn_norm[l], w_router_group[l], b_router_group[l],
                         w_router_expert[l], b_router_expert[l], w_gate[l], w_up[l], w_down[l],
                         final_norm if last else attn_norm[l + 1],
                         F32 if last else BF16, emit_x=not last)
        if last:
            out = res[0]
        else:
            xf, h = res
    return out.reshape(batch, seq, d)
```

```python
import functools
import math

import numpy as np
import jax
import jax.numpy as jnp
from jax import lax
from jax.experimental import pallas as pl
from jax.experimental.pallas import tpu as pltpu

F32 = jnp.float32
BF16 = jnp.bfloat16
HIGHEST = lax.Precision.HIGHEST

RMS_EPS = 1e-6
NEG_INF = -1e30
MLA_HEADS = 16
MLA_Q_LORA = 1024
MLA_KV_LORA = 512
MLA_NOPE_DIM = 128
MLA_ROPE_DIM = 64
MLA_V_DIM = 128
ROPE_THETA = 10000.0
GLA_HEADS = 8
GLA_DK = 128
GLA_DV = 256
GLA_GATE_RANK = 16
GLA_GATE_TAU = 16.0
SWA_Q_HEADS = 32
SWA_KV_HEADS = 4
SWA_HEAD_DIM = 64
SWA_WINDOW = 128
N_BRANCHES = 3
N_GROUPS = 4
EXPERTS_PER_GROUP = 8
TOP_K_IN_GROUP = 2

LANES = 128
MLA_HEAD_PAD = 256
GLA_SUB = 16
VMEM_LIMIT = 60 * 1024 * 1024


def _cparams(sem, vmem=VMEM_LIMIT):
    return pltpu.CompilerParams(dimension_semantics=sem, vmem_limit_bytes=vmem)


def _pick(n, cap, mult=LANES):
    best = None
    for t in range(mult, min(n, cap) + 1, mult):
        if n % t == 0:
            best = t
    assert best is not None, (n, cap, mult)
    return best


def _rmsnorm_body(x_ref, g_ref, o_ref):
    x = x_ref[...]
    y = x * lax.rsqrt(jnp.mean(x * x, axis=-1, keepdims=True) + RMS_EPS)
    o_ref[...] = (y * g_ref[...]).astype(o_ref.dtype)


def rmsnorm(x, gain, out_dtype):
    m, d = x.shape
    tm = _pick(m, 512, 8)
    return pl.pallas_call(
        _rmsnorm_body,
        out_shape=jax.ShapeDtypeStruct((m, d), out_dtype),
        grid=(m // tm,),
        in_specs=[pl.BlockSpec((tm, d), lambda i: (i, 0)),
                  pl.BlockSpec((1, d), lambda i: (0, 0))],
        out_specs=pl.BlockSpec((tm, d), lambda i: (i, 0)),
        compiler_params=_cparams(("parallel",)),
        name="rmsnorm",
    )(x, gain.reshape(1, d))


def _matmul_body(a_ref, w_ref, o_ref):
    o_ref[...] = jnp.dot(a_ref[...], w_ref[...], preferred_element_type=F32).astype(o_ref.dtype)


def _matmul_res_body(a_ref, w_ref, r_ref, o_ref):
    acc = jnp.dot(a_ref[...], w_ref[...], preferred_element_type=F32)
    o_ref[...] = (r_ref[...] + acc).astype(o_ref.dtype)


def matmul(a, w, out_dtype, residual=None, tm_cap=512, tn_cap=1024, name="matmul"):
    m, k = a.shape
    n = w.shape[1]
    tm = _pick(m, tm_cap, 16)
    tn = _pick(n, tn_cap)
    in_specs = [pl.BlockSpec((tm, k), lambda j, i: (i, 0)),
                pl.BlockSpec((k, tn), lambda j, i: (0, j))]
    args = [a, w]
    body = _matmul_body
    if residual is not None:
        in_specs.append(pl.BlockSpec((tm, tn), lambda j, i: (i, j)))
        args.append(residual)
        body = _matmul_res_body
    return pl.pallas_call(
        body,
        out_shape=jax.ShapeDtypeStruct((m, n), out_dtype),
        grid=(n // tn, m // tm),
        in_specs=in_specs,
        out_specs=pl.BlockSpec((tm, tn), lambda j, i: (i, j)),
        compiler_params=_cparams(("parallel", "parallel")),
        name=name,
    )(*args)


def _rope_tables(pos_ref, freq_ref):
    ang = pos_ref[...].astype(F32) * freq_ref[...]
    return jnp.cos(ang), jnp.sin(ang)


def _mla_q_body(c_ref, pos_ref, g_ref, freq_ref, w1_ref, w2_ref, q_ref, *, heads, scale):
    c = c_ref[...]
    cn = (c * lax.rsqrt(jnp.mean(c * c, axis=-1, keepdims=True) + RMS_EPS) * g_ref[...]).astype(BF16)
    q = jnp.dot(cn, w1_ref[...], preferred_element_type=F32)
    qp = jnp.dot(cn, w2_ref[...], preferred_element_type=F32)
    cs, sn = _rope_tables(pos_ref, freq_ref)
    for h in range(heads):
        a = h * MLA_HEAD_PAD
        q_ref[:, a:a + LANES] = (q[:, a:a + LANES] * scale).astype(BF16)
        rot = q[:, a + LANES:a + 2 * LANES] * cs + qp[:, h * LANES:(h + 1) * LANES] * sn
        q_ref[:, a + LANES:a + 2 * LANES] = (rot * scale).astype(BF16)


def _mla_kv_body(c_ref, kr_ref, krp_ref, pos_ref, g_ref, freq_ref, w_ref, k_ref, v_ref, *, heads):
    c = c_ref[...]
    cn = (c * lax.rsqrt(jnp.mean(c * c, axis=-1, keepdims=True) + RMS_EPS) * g_ref[...]).astype(BF16)
    kv = jnp.dot(cn, w_ref[...], preferred_element_type=F32)
    cs, sn = _rope_tables(pos_ref, freq_ref)
    krot = (kr_ref[...] * cs + krp_ref[...] * sn).astype(BF16)
    for h in range(heads):
        a = h * MLA_HEAD_PAD
        k_ref[:, a:a + LANES] = kv[:, a:a + LANES].astype(BF16)
        k_ref[:, a + LANES:a + 2 * LANES] = krot
        v_ref[:, h * LANES:(h + 1) * LANES] = kv[:, a + LANES:a + 2 * LANES].astype(BF16)


def mla_prep(z_a, pos, q_norm, kv_norm, freq, w_q1, w_q2, w_kv, heads):
    m = z_a.shape[0]
    ql, kvl = w_q1.shape[0], w_kv.shape[0]
    tm = _pick(m, 256, 8)
    scale = (MLA_NOPE_DIM + MLA_ROPE_DIM) ** -0.5
    row = lambda i: (i, 0)
    const = lambda i: (0, 0)
    q = pl.pallas_call(
        functools.partial(_mla_q_body, heads=heads, scale=scale),
        out_shape=jax.ShapeDtypeStruct((m, heads * MLA_HEAD_PAD), BF16),
        grid=(m // tm,),
        in_specs=[pl.BlockSpec((tm, ql), row),
                  pl.BlockSpec((tm, 1), row),
                  pl.BlockSpec((1, ql), const),
                  pl.BlockSpec((1, LANES), const),
                  pl.BlockSpec(w_q1.shape, const),
                  pl.BlockSpec(w_q2.shape, const)],
        out_specs=pl.BlockSpec((tm, heads * MLA_HEAD_PAD), row),
        compiler_params=_cparams(("parallel",)),
        name="mla_q_prep",
    )(z_a, pos, q_norm.reshape(1, ql), freq, w_q1, w_q2)
    kr_blk = (ql + kvl) // LANES
    k, v = pl.pallas_call(
        functools.partial(_mla_kv_body, heads=heads),
        out_shape=(jax.ShapeDtypeStruct((m, heads * MLA_HEAD_PAD), BF16),
                   jax.ShapeDtypeStruct((m, heads * LANES), BF16)),
        grid=(m // tm,),
        in_specs=[pl.BlockSpec((tm, kvl), lambda i: (i, ql // kvl)),
                  pl.BlockSpec((tm, LANES), lambda i: (i, kr_blk)),
                  pl.BlockSpec((tm, LANES), lambda i: (i, kr_blk + 1)),
                  pl.BlockSpec((tm, 1), row),
                  pl.BlockSpec((1, kvl), const),
                  pl.BlockSpec((1, LANES), const),
                  pl.BlockSpec(w_kv.shape, const)],
        out_specs=(pl.BlockSpec((tm, heads * MLA_HEAD_PAD), row),
                   pl.BlockSpec((tm, heads * LANES), row)),
        compiler_params=_cparams(("parallel",)),
        name="mla_kv_prep",
    )(z_a, z_a, z_a, pos, kv_norm.reshape(1, kvl), freq, w_kv)
    return q, k, v


def _mla_attn_body(q_ref, k_ref, v_ref, o_ref, m_ref, l_ref, acc_ref, *, tq, sub):
    qi = pl.program_id(2)
    n_sub = tq // sub
    m_ref[...] = jnp.full_like(m_ref, NEG_INF)
    l_ref[...] = jnp.zeros_like(l_ref)
    acc_ref[...] = jnp.zeros_like(acc_ref)

    def update(h, kc, vc, mask):
        rows = pl.ds(h * sub, sub)
        s = lax.dot_general(q_ref[rows, :], kc, (((1,), (1,)), ((), ())), preferred_element_type=F32)
        if mask is not None:
            s = jnp.where(mask, s, NEG_INF)
        m_old = m_ref[rows, :]
        m_new = jnp.maximum(m_old, jnp.max(s, axis=-1, keepdims=True))
        alpha = jnp.exp(m_old - m_new)
        p = jnp.exp(s - m_new)
        l_ref[rows, :] = alpha * l_ref[rows, :] + jnp.sum(p, axis=-1, keepdims=True)
        acc_ref[rows, :] = alpha * acc_ref[rows, :] + jnp.dot(p.astype(BF16), vc,
                                                              preferred_element_type=F32)
        m_ref[rows, :] = m_new

    def full_tile(j, carry):
        k0 = pl.multiple_of(j * tq, tq)
        kc = k_ref[pl.ds(k0, tq), :]
        vc = v_ref[pl.ds(k0, tq), :]
        for h in range(n_sub):
            update(h, kc, vc, None)
        return carry

    lax.fori_loop(0, qi, full_tile, 0)
    base = pl.multiple_of(qi * tq, tq)
    tri = (lax.broadcasted_iota(jnp.int32, (sub, sub), 1)
           <= lax.broadcasted_iota(jnp.int32, (sub, sub), 0))
    for h in range(n_sub):
        for g in range(h + 1):
            kc = k_ref[pl.ds(base + g * sub, sub), :]
            vc = v_ref[pl.ds(base + g * sub, sub), :]
            update(h, kc, vc, tri if g == h else None)
    o_ref[...] = (acc_ref[...] / l_ref[...]).astype(o_ref.dtype)


def mla_attention(q, k, v, batch, seq, heads):
    tq = _pick(seq, 512, 16)
    sub = tq // 2
    nq = seq // tq
    return pl.pallas_call(
        functools.partial(_mla_attn_body, tq=tq, sub=sub),
        out_shape=jax.ShapeDtypeStruct((batch * seq, heads * MLA_V_DIM), BF16),
        grid=(batch, heads, nq),
        in_specs=[pl.BlockSpec((tq, MLA_HEAD_PAD), lambda b, h, qi: (b * nq + qi, h)),
                  pl.BlockSpec((seq, MLA_HEAD_PAD), lambda b, h, qi: (b, h)),
                  pl.BlockSpec((seq, MLA_V_DIM), lambda b, h, qi: (b, h))],
        out_specs=pl.BlockSpec((tq, MLA_V_DIM), lambda b, h, qi: (b * nq + qi, h)),
        scratch_shapes=[pltpu.VMEM((tq, 1), F32), pltpu.VMEM((tq, 1), F32),
                        pltpu.VMEM((tq, MLA_V_DIM), F32)],
        compiler_params=_cparams(("parallel", "parallel", "parallel")),
        name="mla_attention",
    )(q, k, v)


def _swa_body(sink_ref, q_ref, kp_ref, kc_ref, vp_ref, vc_ref, o_ref, *, group):
    w = SWA_WINDOW
    n = pl.program_id(1)
    scale = SWA_HEAD_DIM ** -0.5
    lane = lax.broadcasted_iota(jnp.int32, (2 * w, LANES), 1)
    lo = lane < SWA_HEAD_DIM
    qlane_lo = lax.broadcasted_iota(jnp.int32, (w, LANES), 1) < SWA_HEAD_DIM
    ri = lax.broadcasted_iota(jnp.int32, (2 * w, 2 * w), 0) % w
    cj = lax.broadcasted_iota(jnp.int32, (2 * w, 2 * w), 1)
    valid = (cj > ri) & (cj <= ri + w) & ((cj >= w) | (n > 0))
    row_first = lax.broadcasted_iota(jnp.int32, (2 * w, 1), 0) < w
    kfull = jnp.concatenate([kp_ref[...], kc_ref[...]], axis=0).astype(F32)
    vfull = jnp.concatenate([vp_ref[...], vc_ref[...]], axis=0).astype(F32)
    n_chunks = kfull.shape[1] // LANES
    for c in range(n_chunks):
        kc = kfull[:, c * LANES:(c + 1) * LANES]
        vc = vfull[:, c * LANES:(c + 1) * LANES]
        kr = pltpu.roll(kc, SWA_HEAD_DIM, axis=1)
        vr = pltpu.roll(vc, SWA_HEAD_DIM, axis=1)
        for sub in range(2):
            hk = 2 * c + sub
            k2 = (jnp.where(lo, kc, kr) if sub == 0 else jnp.where(lo, kr, kc)).astype(BF16)
            v2 = (jnp.where(lo, vc, vr) if sub == 0 else jnp.where(lo, vr, vc)).astype(BF16)
            for j in range(group // 2):
                chunk = hk * (group // 2) + j
                hq = hk * group + 2 * j
                qp = q_ref[:, chunk * LANES:(chunk + 1) * LANES].astype(F32)
                qa = jnp.where(qlane_lo, qp, 0.0)
                qb = jnp.where(qlane_lo, 0.0, qp)
                qs = jnp.concatenate([qa, qb], axis=0).astype(BF16)
                s = lax.dot_general(qs, k2, (((1,), (1,)), ((), ())),
                                    preferred_element_type=F32) * scale
                s = jnp.where(valid, s, NEG_INF)
                sink = jnp.where(row_first, sink_ref[hq], sink_ref[hq + 1])
                mx = jnp.maximum(jnp.max(s, axis=-1, keepdims=True), sink)
                p = jnp.exp(s - mx)
                denom = jnp.sum(p, axis=-1, keepdims=True) + jnp.exp(sink - mx)
                o2 = jnp.dot((p / denom).astype(BF16), v2, preferred_element_type=F32)
                o_ref[:, chunk * LANES:(chunk + 1) * LANES] = jnp.where(
                    qlane_lo, o2[:w], o2[w:]).astype(o_ref.dtype)


def swa_attention(z_s, sinks, batch, seq, q_heads, kv_heads):
    w = SWA_WINDOW
    nb = seq // w
    qw = q_heads * SWA_HEAD_DIM
    kw = kv_heads * SWA_HEAD_DIM
    assert qw % kw == 0 and kw % LANES == 0 and (q_heads // kv_heads) % 2 == 0
    kb = qw // kw
    cur = lambda blk: (lambda b, n, s: (b * nb + n, blk))
    prev = lambda blk: (lambda b, n, s: (b * nb + jnp.maximum(n - 1, 0), blk))
    return pl.pallas_call(
        functools.partial(_swa_body, group=q_heads // kv_heads),
        out_shape=jax.ShapeDtypeStruct((batch * seq, qw), BF16),
        grid_spec=pltpu.PrefetchScalarGridSpec(
            num_scalar_prefetch=1,
            grid=(batch, nb),
            in_specs=[pl.BlockSpec((w, qw), cur(0)),
                      pl.BlockSpec((w, kw), prev(kb)),
                      pl.BlockSpec((w, kw), cur(kb)),
                      pl.BlockSpec((w, kw), prev(kb + 1)),
                      pl.BlockSpec((w, kw), cur(kb + 1))],
            out_specs=pl.BlockSpec((w, qw), cur(0))),
        compiler_params=_cparams(("parallel", "arbitrary")),
        name="swa_attention",
    )(sinks, z_s, z_s, z_s, z_s, z_s)


def _gla_body(q_ref, k_ref, v_ref, g_ref, a_ref, wa_ref, ba_ref, gn_ref, o_ref,
              state_ref, b_ref, *, chunk, tokens):
    c_sz = chunk
    t = pl.program_id(2)

    @pl.when(t == 0)
    def _():
        state_ref[...] = jnp.zeros_like(state_ref)

    z = jnp.dot(a_ref[...], wa_ref[...], precision=HIGHEST, preferred_element_type=F32) + ba_ref[...]
    log_a = (jnp.minimum(z, 0.0) - jnp.log(1.0 + jnp.exp(-jnp.abs(z)))) * (1.0 / GLA_GATE_TAU)
    ri = lax.broadcasted_iota(jnp.int32, (tokens, tokens), 0)
    ci = lax.broadcasted_iota(jnp.int32, (tokens, tokens), 1)
    tri = ((ci <= ri) & (ci // c_sz == ri // c_sz)).astype(F32)
    b_ref[...] = jnp.dot(tri, log_a, precision=HIGHEST, preferred_element_type=F32)

    qscale = GLA_DK ** -0.5
    sub = GLA_SUB
    n_sub = c_sz // sub
    rowid = lax.broadcasted_iota(jnp.int32, (sub, 1), 0)

    def chunk_step(c, carry):
        c0 = pl.multiple_of(c * c_sz, c_sz)
        b = b_ref[pl.ds(c0, c_sz), :]
        q = q_ref[pl.ds(c0, c_sz), :].astype(F32) * qscale
        k = k_ref[pl.ds(c0, c_sz), :].astype(F32)
        v = v_ref[pl.ds(c0, c_sz), :]
        vf = v.astype(F32)
        state_t = state_ref[...]
        o_inter = lax.dot_general((q * jnp.exp(b)).astype(BF16), state_t.astype(BF16),
                                  (((1,), (1,)), ((), ())), preferred_element_type=F32)
        pieces = []
        for i in range(n_sub):
            r0 = i * sub
            b_i, q_i, k_i, v_i = b[r0:r0 + sub], q[r0:r0 + sub], k[r0:r0 + sub], vf[r0:r0 + sub]
            acc = jnp.zeros((sub, vf.shape[1]), F32)
            for j in range(sub):
                wgt = jnp.exp(jnp.minimum(b_i - b_i[j:j + 1, :], 0.0))
                col = jnp.sum(q_i * k_i[j:j + 1, :] * wgt, axis=-1, keepdims=True)
                col = jnp.where(rowid >= j, col, 0.0)
                acc = acc + col * v_i[j:j + 1, :]
            if i > 0:
                ref_b = b[r0 - 1:r0, :]
                qh = (q_i * jnp.exp(b_i - ref_b)).astype(BF16)
                kh = (k[:r0] * jnp.exp(ref_b - b[:r0])).astype(BF16)
                att = lax.dot_general(qh, kh, (((1,), (1,)), ((), ())), preferred_element_type=F32)
                acc = acc + jnp.dot(att.astype(BF16), v[:r0], preferred_element_type=F32)
            pieces.append(acc)
        o = o_inter + jnp.concatenate(pieces, axis=0)
        b_last = b[c_sz - 1:c_sz, :]
        k_dec = (k * jnp.exp(b_last - b)).astype(BF16)
        upd = lax.dot_general(v, k_dec, (((0,), (0,)), ((), ())), preferred_element_type=F32)
        state_ref[...] = state_t * jnp.exp(b_last) + upd
        y = o * lax.rsqrt(jnp.mean(o * o, axis=-1, keepdims=True) + RMS_EPS) * gn_ref[...]
        g = g_ref[pl.ds(c0, c_sz), :].astype(F32)
        y = y * (g / (1.0 + jnp.exp(-g)))
        o_ref[pl.ds(c0, c_sz), :] = y.astype(o_ref.dtype)
        return carry

    lax.fori_loop(0, tokens // c_sz, chunk_step, 0)


def gla_attention(z_g, z_a, a_blk, w_a2p, b_a, out_norm, batch, seq, heads, chunk=64):
    dk, dv = GLA_DK, GLA_DV
    tokens = _pick(seq, 256, chunk)
    nt = seq // tokens
    kq = heads * dk // dk
    kv_off = 2 * heads * dk // dv
    g_off = kv_off + heads
    row = lambda blk_fn: (lambda b, h, t: (b * nt + t, blk_fn(h)))
    return pl.pallas_call(
        functools.partial(_gla_body, chunk=chunk, tokens=tokens),
        out_shape=jax.ShapeDtypeStruct((batch * seq, heads * dv), BF16),
        grid=(batch, heads, nt),
        in_specs=[pl.BlockSpec((tokens, dk), row(lambda h: h)),
                  pl.BlockSpec((tokens, dk), row(lambda h: kq + h)),
                  pl.BlockSpec((tokens, dv), row(lambda h: kv_off + h)),
                  pl.BlockSpec((tokens, dv), row(lambda h: g_off + h)),
                  pl.BlockSpec((tokens, LANES), row(lambda h: a_blk)),
                  pl.BlockSpec((LANES, dk), lambda b, h, t: (0, h)),
                  pl.BlockSpec((1, dk), lambda b, h, t: (0, h)),
                  pl.BlockSpec((1, dv), lambda b, h, t: (0, 0))],
        out_specs=pl.BlockSpec((tokens, dv), row(lambda h: h)),
        scratch_shapes=[pltpu.VMEM((dv, dk), F32), pltpu.VMEM((tokens, dk), F32)],
        compiler_params=_cparams(("parallel", "parallel", "arbitrary")),
        name="gla_attention",
    )(z_g, z_g, z_g, z_g, z_a, w_a2p, b_a.reshape(1, -1), out_norm.reshape(1, dv))


def _merge_body(h_ref, y0_ref, y1_ref, y2_ref, wg0_ref, wg1_ref, wg2_ref,
                wb0_ref, wb1_ref, wb2_ref, bias_ref, o_ref):
    h = h_ref[...]
    acc = None
    branches = ((y0_ref, wg0_ref, wb0_ref), (y1_ref, wg1_ref, wb1_ref), (y2_ref, wg2_ref, wb2_ref))
    for b, (y_ref, wg_ref, wb_ref) in enumerate(branches):
        logit = jnp.dot(h, wg_ref[...], preferred_element_type=F32) + bias_ref[b:b + 1, :]
        gate = 1.0 / (1.0 + jnp.exp(-logit))
        term = gate * jnp.dot(y_ref[...], wb_ref[...], preferred_element_type=F32)
        acc = term if acc is None else acc + term
    o_ref[...] = acc.astype(o_ref.dtype)


def merge_branches(h, ys, w_gates, w_brs, bias3):
    m, d = h.shape
    bw = ys[0].shape[1]
    tm = _pick(m, 512, 16)
    tn = _pick(d, 256)
    nj = d // tn
    gate_spec = lambda b: pl.BlockSpec((d, tn), lambda j, i: (0, b * nj + j))
    return pl.pallas_call(
        _merge_body,
        out_shape=jax.ShapeDtypeStruct((m, d), BF16),
        grid=(nj, m // tm),
        in_specs=[pl.BlockSpec((tm, d), lambda j, i: (i, 0))]
        + [pl.BlockSpec((tm, bw), lambda j, i: (i, 0))] * N_BRANCHES
        + [gate_spec(b) for b in range(N_BRANCHES)]
        + [pl.BlockSpec((bw, tn), lambda j, i: (0, j))] * N_BRANCHES
        + [pl.BlockSpec((N_BRANCHES, tn), lambda j, i: (0, j))],
        out_specs=pl.BlockSpec((tm, tn), lambda j, i: (i, j)),
        compiler_params=_cparams(("parallel", "parallel")),
        name="merge_branches",
    )(h, *ys, w_gates, w_gates, w_gates, *w_brs, bias3)


def _router_body(x_ref, g_ref, w_ref, b_ref, h_ref, e_ref, wt_ref, rk_ref, cnt_ref, carry_ref,
                 *, n_exp, n_grp, per_grp):
    i = pl.program_id(0)

    @pl.when(i == 0)
    def _():
        carry_ref[...] = jnp.zeros_like(carry_ref)

    x = x_ref[...]
    tm = x.shape[0]
    hn = x * lax.rsqrt(jnp.mean(x * x, axis=-1, keepdims=True) + RMS_EPS) * g_ref[...]
    h_ref[...] = hn
    logits = jnp.dot(hn, w_ref[...], precision=HIGHEST, preferred_element_type=F32) + b_ref[...]
    lane = lax.broadcasted_iota(jnp.int32, (tm, LANES), 1)
    big = jnp.int32(LANES)

    def top1(vals):
        mx = jnp.max(vals, axis=-1, keepdims=True)
        idx = jnp.min(jnp.where(vals == mx, lane, big), axis=-1, keepdims=True)
        return mx, idx

    gl = jnp.where((lane >= n_exp) & (lane < n_exp + n_grp), logits, NEG_INF)
    gmax, gidx = top1(gl)
    g_top_p = 1.0 / jnp.sum(jnp.exp(gl - gmax), axis=-1, keepdims=True)
    grp = gidx - n_exp
    el = jnp.where((lane >= grp * per_grp) & (lane < (grp + 1) * per_grp), logits, NEG_INF)
    m1, i1 = top1(el)
    m2, i2 = top1(jnp.where(lane == i1, NEG_INF, el))
    e21 = jnp.exp(m2 - m1)
    w1 = g_top_p / (1.0 + e21)
    w2 = w1 * e21
    e_ref[...] = jnp.where(lane == 0, i1, jnp.where(lane == 1, i2, 0))[:, :8]
    wt_ref[...] = jnp.where(lane == 0, w1, jnp.where(lane == 1, w2, 0.0))[:, :8]

    oh1 = (lane == i1).astype(F32)
    oh2 = (lane == i2).astype(F32)
    ri = lax.broadcasted_iota(jnp.int32, (tm, tm), 0)
    ci = lax.broadcasted_iota(jnp.int32, (tm, tm), 1)
    strict = (ci < ri).astype(BF16)
    cum1 = jnp.dot(strict, oh1.astype(BF16), preferred_element_type=F32)
    cum2 = jnp.dot(strict, oh2.astype(BF16), preferred_element_type=F32)
    tot1 = jnp.sum(oh1, axis=0, keepdims=True)
    tot2 = jnp.sum(oh2, axis=0, keepdims=True)
    carry = carry_ref[...]
    r1 = jnp.sum(oh1 * (cum1 + carry), axis=-1, keepdims=True)
    r2 = jnp.sum(oh2 * (cum2 + carry + tot1), axis=-1, keepdims=True)
    rk_ref[...] = jnp.where(lane == 0, r1, jnp.where(lane == 1, r2, 0.0))[:, :8].astype(jnp.int32)
    carry = carry + tot1 + tot2
    carry_ref[...] = carry
    cnt_ref[...] = jnp.broadcast_to(carry, cnt_ref.shape).astype(jnp.int32)


def moe_router(x, gain, w_r, b_r, n_exp, n_grp):
    m, d = x.shape
    tm = _pick(m, 256, 8)
    row = lambda i: (i, 0)
    const = lambda i: (0, 0)
    return pl.pallas_call(
        functools.partial(_router_body, n_exp=n_exp, n_grp=n_grp, per_grp=n_exp // n_grp),
        out_shape=(jax.ShapeDtypeStruct((m, d), F32),
                   jax.ShapeDtypeStruct((m, 8), jnp.int32),
                   jax.ShapeDtypeStruct((m, 8), F32),
                   jax.ShapeDtypeStruct((m, 8), jnp.int32),
                   jax.ShapeDtypeStruct((8, LANES), jnp.int32)),
        grid=(m // tm,),
        in_specs=[pl.BlockSpec((tm, d), row), pl.BlockSpec((1, d), const),
                  pl.BlockSpec((d, LANES), const), pl.BlockSpec((1, LANES), const)],
        out_specs=(pl.BlockSpec((tm, d), row), pl.BlockSpec((tm, 8), row),
                   pl.BlockSpec((tm, 8), row), pl.BlockSpec((tm, 8), row),
                   pl.BlockSpec((8, LANES), const)),
        scratch_shapes=[pltpu.VMEM((1, LANES), F32)],
        compiler_params=_cparams(("arbitrary",)),
        name="moe_router",
    )(x, gain.reshape(1, d), w_r, b_r)


def _dispatch_body(p1_ref, p2_ref, h_ref, xs_in_ref, xs_ref, sem, *, tokens):
    del xs_in_ref
    base = pl.program_id(0) * tokens

    def copies(t):
        src = h_ref.at[pl.ds(t, 1), :]
        return (pltpu.make_async_copy(src, xs_ref.at[pl.ds(p1_ref[base + t], 1), :], sem),
                pltpu.make_async_copy(src, xs_ref.at[pl.ds(p2_ref[base + t], 1), :], sem))

    def start(t, c):
        a, b = copies(t)
        a.start()
        b.start()
        return c

    def wait(t, c):
        a, b = copies(t)
        a.wait()
        b.wait()
        return c

    lax.fori_loop(0, tokens, start, 0, unroll=8)
    lax.fori_loop(0, tokens, wait, 0, unroll=8)


def moe_dispatch(h, pos1, pos2, rows):
    m, d = h.shape
    tokens = _pick(m, 256, 8)
    xs0 = jnp.zeros((rows, d), F32)
    return pl.pallas_call(
        functools.partial(_dispatch_body, tokens=tokens),
        out_shape=jax.ShapeDtypeStruct((rows, d), F32),
        grid_spec=pltpu.PrefetchScalarGridSpec(
            num_scalar_prefetch=2,
            grid=(m // tokens,),
            in_specs=[pl.BlockSpec((tokens, d), lambda i, p1, p2: (i, 0)),
                      pl.BlockSpec(memory_space=pl.ANY)],
            out_specs=pl.BlockSpec(memory_space=pl.ANY),
            scratch_shapes=[pltpu.SemaphoreType.DMA(())]),
        input_output_aliases={3: 0},
        compiler_params=_cparams(("arbitrary",)),
        name="moe_dispatch",
    )(pos1, pos2, h, xs0)


def _experts_up_body(te_ref, ts_ref, tv_ref, x_ref, wg_ref, wu_ref, h_ref):
    del te_ref, ts_ref
    i = pl.program_id(0)

    @pl.when(tv_ref[i] > 0)
    def _():
        x = x_ref[...].astype(BF16)
        a = jnp.dot(x, wg_ref[...].astype(BF16), preferred_element_type=F32)
        u = jnp.dot(x, wu_ref[...].astype(BF16), preferred_element_type=F32)
        h_ref[...] = ((a / (1.0 + jnp.exp(-a))) * u).astype(h_ref.dtype)

    @pl.when(tv_ref[i] == 0)
    def _():
        h_ref[...] = jnp.zeros_like(h_ref)


def _experts_down_body(te_ref, ts_ref, tv_ref, h_ref, wd_ref, o_ref):
    del te_ref, ts_ref
    i = pl.program_id(0)

    @pl.when(tv_ref[i] > 0)
    def _():
        o_ref[...] = jnp.dot(h_ref[...], wd_ref[...].astype(BF16), preferred_element_type=F32)

    @pl.when(tv_ref[i] == 0)
    def _():
        o_ref[...] = jnp.zeros_like(o_ref)


def moe_experts(xs, tile_expert, tile_src, tile_valid, w_gate, w_up, w_down, layer, tm):
    rows, d = xs.shape
    ff = w_gate.shape[3]
    n_tiles = rows // tm
    tf = _pick(ff, 256)
    td = _pick(d, 1024)
    nf, nd = ff // tf, d // td
    prefetch = (tile_expert, tile_src, tile_valid)

    def up_w(i, j, te, ts, tv):
        return (layer, te[i], 0, jnp.where(tv[i] > 0, j, nf - 1))

    hid = pl.pallas_call(
        _experts_up_body,
        out_shape=jax.ShapeDtypeStruct((rows, ff), BF16),
        grid_spec=pltpu.PrefetchScalarGridSpec(
            num_scalar_prefetch=3,
            grid=(n_tiles, nf),
            in_specs=[pl.BlockSpec((tm, d), lambda i, j, te, ts, tv: (ts[i], 0)),
                      pl.BlockSpec((None, None, d, tf), up_w),
                      pl.BlockSpec((None, None, d, tf), up_w)],
            out_specs=pl.BlockSpec((tm, tf), lambda i, j, te, ts, tv: (i, j))),
        compiler_params=_cparams(("arbitrary", "arbitrary")),
        name="moe_experts_up",
    )(*prefetch, xs, w_gate, w_up)
    return pl.pallas_call(
        _experts_down_body,
        out_shape=jax.ShapeDtypeStruct((rows, d), F32),
        grid_spec=pltpu.PrefetchScalarGridSpec(
            num_scalar_prefetch=3,
            grid=(n_tiles, nd),
            in_specs=[pl.BlockSpec((tm, ff), lambda i, j, te, ts, tv: (i, 0)),
                      pl.BlockSpec((None, None, ff, td),
                                   lambda i, j, te, ts, tv: (layer, te[i], 0, jnp.where(tv[i] > 0, j, nd - 1)))],
            out_specs=pl.BlockSpec((tm, td), lambda i, j, te, ts, tv: (i, j))),
        compiler_params=_cparams(("arbitrary", "arbitrary")),
        name="moe_experts_down",
    )(*prefetch, hid, w_down)


def _combine_body(p1_ref, p2_ref, x_ref, wt_ref, g_ref, ys_ref, *rest, tokens, emit_x):
    if emit_x:
        xo_ref, no_ref, buf_ref, sem = rest
    else:
        no_ref, buf_ref, sem = rest
    base = pl.program_id(0) * tokens

    def copies(t):
        return (pltpu.make_async_copy(ys_ref.at[pl.ds(p1_ref[base + t], 1), :],
                                      buf_ref.at[0, pl.ds(t, 1), :], sem),
                pltpu.make_async_copy(ys_ref.at[pl.ds(p2_ref[base + t], 1), :],
                                      buf_ref.at[1, pl.ds(t, 1), :], sem))

    def start(t, c):
        a, b = copies(t)
        a.start()
        b.start()
        return c

    def wait(t, c):
        a, b = copies(t)
        a.wait()
        b.wait()
        return c

    lax.fori_loop(0, tokens, start, 0, unroll=8)
    lax.fori_loop(0, tokens, wait, 0, unroll=8)
    wt = wt_ref[...]
    x = x_ref[...] + wt[:, 0:1] * buf_ref[0] + wt[:, 1:2] * buf_ref[1]
    if emit_x:
        xo_ref[...] = x
    y = x * lax.rsqrt(jnp.mean(x * x, axis=-1, keepdims=True) + RMS_EPS)
    no_ref[...] = (y * g_ref[...]).astype(no_ref.dtype)


def moe_combine(x, ys, pos1, pos2, wts, gain, norm_dtype, emit_x):
    m, d = x.shape
    tokens = _pick(m, 256, 8)
    row = lambda i, p1, p2: (i, 0)
    out_shape = [jax.ShapeDtypeStruct((m, d), norm_dtype)]
    out_specs = [pl.BlockSpec((tokens, d), row)]
    if emit_x:
        out_shape.insert(0, jax.ShapeDtypeStruct((m, d), F32))
        out_specs.insert(0, pl.BlockSpec((tokens, d), row))
    return pl.pallas_call(
        functools.partial(_combine_body, tokens=tokens, emit_x=emit_x),
        out_shape=tuple(out_shape),
        grid_spec=pltpu.PrefetchScalarGridSpec(
            num_scalar_prefetch=2,
            grid=(m // tokens,),
            in_specs=[pl.BlockSpec((tokens, d), row),
                      pl.BlockSpec((tokens, 8), row),
                      pl.BlockSpec((1, d), lambda i, p1, p2: (0, 0)),
                      pl.BlockSpec(memory_space=pl.ANY)],
            out_specs=tuple(out_specs),
            scratch_shapes=[pltpu.VMEM((2, tokens, d), F32), pltpu.SemaphoreType.DMA(())]),
        compiler_params=_cparams(("arbitrary",)),
        name="moe_combine",
    )(pos1, pos2, x, wts, gain.reshape(1, d), ys)


def _rope_partner(w_rot):
    half = MLA_ROPE_DIM // 2
    return jnp.concatenate([-w_rot[..., half:], w_rot[..., :half]], axis=-1)


def _layer_weights(w_in, w_uq, heads_mla):
    d = w_in.shape[0]
    gla_qk = GLA_HEADS * GLA_DK
    gla_w = GLA_HEADS * GLA_DV
    swa_q = SWA_Q_HEADS * SWA_HEAD_DIM
    swa_kv = SWA_KV_HEADS * SWA_HEAD_DIM
    sizes = (MLA_Q_LORA, MLA_KV_LORA, MLA_ROPE_DIM, gla_qk, gla_qk, gla_w, GLA_GATE_RANK, gla_w,
             swa_q, swa_kv, swa_kv, N_BRANCHES * d)
    offs = np.concatenate([[0], np.cumsum(sizes)])
    seg = lambda i: w_in[:, offs[i]:offs[i + 1]]
    zpad = lambda n: jnp.zeros((d, n), w_in.dtype)
    kr = seg(2)
    w_a = jnp.concatenate([seg(0), seg(1), kr, zpad(LANES - MLA_ROPE_DIM),
                           _rope_partner(kr), zpad(LANES - MLA_ROPE_DIM),
                           seg(6), zpad(LANES - GLA_GATE_RANK)], axis=1).astype(BF16)
    w_g = jnp.concatenate([seg(3), seg(4), seg(5), seg(7)], axis=1).astype(BF16)
    w_s = w_in[:, offs[8]:offs[11]].astype(BF16)
    w_gates = seg(11).astype(BF16)
    qk = MLA_NOPE_DIM + MLA_ROPE_DIM
    wq = w_uq.reshape(-1, heads_mla, qk)
    pad = jnp.zeros((wq.shape[0], heads_mla, MLA_HEAD_PAD - qk), w_uq.dtype)
    w_q1 = jnp.concatenate([wq, pad], axis=-1).reshape(wq.shape[0], -1).astype(BF16)
    rot = wq[..., MLA_NOPE_DIM:]
    pad2 = jnp.zeros((wq.shape[0], heads_mla, LANES - MLA_ROPE_DIM), w_uq.dtype)
    w_q2 = jnp.concatenate([_rope_partner(rot), pad2], axis=-1).reshape(wq.shape[0], -1).astype(BF16)
    return w_a, w_g, w_s, w_gates, w_q1, w_q2


def _moe_layer(x, gain, w_rg, b_rg, w_re, b_re, w_gate, w_up, w_down, layer, next_gain, norm_dtype,
               emit_x):
    m, d = x.shape
    n_exp, n_grp = w_re.shape[1], w_rg.shape[1]
    w_r = jnp.concatenate([w_re, w_rg, jnp.zeros((d, LANES - n_exp - n_grp), F32)], axis=1)
    b_r = jnp.concatenate([b_re, b_rg, jnp.zeros((LANES - n_exp - n_grp,), F32)]).reshape(1, LANES)
    hn, e_idx, wts, rank, counts = moe_router(x, gain, w_r, b_r, n_exp, n_grp)
    tm = 512 if m >= 2048 else 64
    cnt = counts[0, :n_exp]
    padded = ((cnt + tm - 1) // tm) * tm
    ends = jnp.cumsum(padded)
    offs = ends - padded
    n_tiles = (TOP_K_IN_GROUP * m) // tm + n_exp
    tile_start = jnp.arange(n_tiles, dtype=jnp.int32) * tm
    tile_valid = (tile_start < ends[-1]).astype(jnp.int32)
    last_tile = jnp.maximum(ends[-1] // tm - 1, 0).astype(jnp.int32)
    tile_src = jnp.minimum(jnp.arange(n_tiles, dtype=jnp.int32), last_tile)
    tile_expert = jnp.sum((ends[None, :] <= (tile_src * tm)[:, None]).astype(jnp.int32), axis=1)
    tile_expert = jnp.minimum(tile_expert, n_exp - 1)
    pos = offs[e_idx[:, :2]].astype(jnp.int32) + rank[:, :2]
    pos1, pos2 = pos[:, 0], pos[:, 1]
    xs = moe_dispatch(hn, pos1, pos2, n_tiles * tm)
    ys = moe_experts(xs, tile_expert, tile_src, tile_valid, w_gate, w_up, w_down, layer, tm)
    return moe_combine(x, ys, pos1, pos2, wts, next_gain, norm_dtype, emit_x)


def kernel(x, positions, attn_norm, w_in, gate_bias, mla_q_norm, mla_w_uq, mla_kv_norm, mla_w_ukv,
           gla_w_a2, gla_b_a, gla_out_norm, swa_sinks, w_br_mla, w_br_gla, w_br_swa, w_out,
           ffn_norm, w_router_group, b_router_group, w_router_expert, b_router_expert,
           w_gate, w_up, w_down, final_norm):
    batch, seq, d = x.shape
    depth = w_in.shape[0]
    m = batch * seq
    heads_mla = mla_w_uq.shape[2] // (MLA_NOPE_DIM + MLA_ROPE_DIM)
    xf = x.reshape(m, d)
    pos = positions.reshape(m, 1).astype(jnp.int32)
    half = MLA_ROPE_DIM // 2
    inv_freq = ROPE_THETA ** (-jnp.arange(0, MLA_ROPE_DIM, 2, dtype=F32) / MLA_ROPE_DIM)
    freq = jnp.concatenate([inv_freq, inv_freq, jnp.zeros((LANES - 2 * half,), F32)]).reshape(1, LANES)
    a_blk = (MLA_Q_LORA + MLA_KV_LORA) // LANES + 2

    h = rmsnorm(xf, attn_norm[0], BF16)
    out = None
    for l in range(depth):
        w_a, w_g, w_s, w_gates, w_q1, w_q2 = _layer_weights(w_in[l], mla_w_uq[l], heads_mla)
        z_a = matmul(h, w_a, F32, name="in_proj_mla")
        z_g = matmul(h, w_g, BF16, name="in_proj_gla")
        z_s = matmul(h, w_s, BF16, name="in_proj_swa")
        q, k, v = mla_prep(z_a, pos, mla_q_norm[l], mla_kv_norm[l], freq, w_q1, w_q2,
                           mla_w_ukv[l].astype(BF16), heads_mla)
        y_mla = mla_attention(q, k, v, batch, seq, heads_mla)
        w_a2p = jnp.concatenate([gla_w_a2[l], jnp.zeros((LANES - GLA_GATE_RANK, gla_w_a2.shape[2]), F32)], axis=0)
        y_gla = gla_attention(z_g, z_a, a_blk, w_a2p, gla_b_a[l], gla_out_norm[l], batch, seq, GLA_HEADS)
        y_swa = swa_attention(z_s, swa_sinks[l], batch, seq, SWA_Q_HEADS, SWA_KV_HEADS)
        w_brs = (w_br_mla[l].astype(BF16), w_br_gla[l].astype(BF16), w_br_swa[l].astype(BF16))
        merged = merge_branches(h, (y_mla, y_gla, y_swa), w_gates, w_brs, gate_bias[l])
        xf = matmul(merged, w_out[l].astype(BF16), F32, residual=xf, name="out_proj")
        last = l == depth - 1
        res = _moe_layer(xf, ffn_norm[l], w_router_group[l], b_router_group[l],
                         w_router_expert[l], b_router_expert[l], w_gate, w_up, w_down, l,
                         final_norm if last else attn_norm[l + 1],
                         F32 if last else BF16, emit_x=not last)
        if last:
            out = res[0]
        else:
            xf, h = res
    return out.reshape(batch, seq, d)
```

```python
import functools
import math

import numpy as np
import jax
import jax.numpy as jnp
from jax import lax
from jax.experimental import pallas as pl
from jax.experimental.pallas import tpu as pltpu

F32 = jnp.float32
BF16 = jnp.bfloat16
HIGHEST = lax.Precision.HIGHEST

RMS_EPS = 1e-6
NEG_INF = -1e30
MLA_HEADS = 16
MLA_Q_LORA = 1024
MLA_KV_LORA = 512
MLA_NOPE_DIM = 128
MLA_ROPE_DIM = 64
MLA_V_DIM = 128
ROPE_THETA = 10000.0
GLA_HEADS = 8
GLA_DK = 128
GLA_DV = 256
GLA_GATE_RANK = 16
GLA_GATE_TAU = 16.0
SWA_Q_HEADS = 32
SWA_KV_HEADS = 4
SWA_HEAD_DIM = 64
SWA_WINDOW = 128
N_BRANCHES = 3
N_GROUPS = 4
EXPERTS_PER_GROUP = 8
TOP_K_IN_GROUP = 2

LANES = 128
MLA_HEAD_PAD = 256
GLA_SUB = 16
VMEM_LIMIT = 60 * 1024 * 1024


def _cparams(sem, vmem=VMEM_LIMIT):
    return pltpu.CompilerParams(dimension_semantics=sem, vmem_limit_bytes=vmem)


def _pick(n, cap, mult=LANES):
    best = None
    for t in range(mult, min(n, cap) + 1, mult):
        if n % t == 0:
            best = t
    assert best is not None, (n, cap, mult)
    return best


def _rmsnorm_body(x_ref, g_ref, o_ref):
    x = x_ref[...]
    y = x * lax.rsqrt(jnp.mean(x * x, axis=-1, keepdims=True) + RMS_EPS)
    o_ref[...] = (y * g_ref[...]).astype(o_ref.dtype)


def rmsnorm(x, gain, out_dtype):
    m, d = x.shape
    tm = _pick(m, 512, 8)
    return pl.pallas_call(
        _rmsnorm_body,
        out_shape=jax.ShapeDtypeStruct((m, d), out_dtype),
        grid=(m // tm,),
        in_specs=[pl.BlockSpec((tm, d), lambda i: (i, 0)),
                  pl.BlockSpec((1, d), lambda i: (0, 0))],
        out_specs=pl.BlockSpec((tm, d), lambda i: (i, 0)),
        compiler_params=_cparams(("parallel",)),
        name="rmsnorm",
    )(x, gain.reshape(1, d))


def _matmul_body(a_ref, w_ref, o_ref):
    o_ref[...] = jnp.dot(a_ref[...], w_ref[...], preferred_element_type=F32).astype(o_ref.dtype)


def _matmul_res_body(a_ref, w_ref, r_ref, o_ref):
    acc = jnp.dot(a_ref[...], w_ref[...], preferred_element_type=F32)
    o_ref[...] = (r_ref[...] + acc).astype(o_ref.dtype)


def matmul(a, w, out_dtype, residual=None, tm_cap=512, tn_cap=1024, name="matmul"):
    m, k = a.shape
    n = w.shape[1]
    tm = _pick(m, tm_cap, 16)
    tn = _pick(n, tn_cap)
    in_specs = [pl.BlockSpec((tm, k), lambda j, i: (i, 0)),
                pl.BlockSpec((k, tn), lambda j, i: (0, j))]
    args = [a, w]
    body = _matmul_body
    if residual is not None:
        in_specs.append(pl.BlockSpec((tm, tn), lambda j, i: (i, j)))
        args.append(residual)
        body = _matmul_res_body
    return pl.pallas_call(
        body,
        out_shape=jax.ShapeDtypeStruct((m, n), out_dtype),
        grid=(n // tn, m // tm),
        in_specs=in_specs,
        out_specs=pl.BlockSpec((tm, tn), lambda j, i: (i, j)),
        compiler_params=_cparams(("parallel", "parallel")),
        name=name,
    )(*args)


def _rope_tables(pos_ref, freq_ref):
    ang = pos_ref[...].astype(F32) * freq_ref[...]
    return jnp.cos(ang), jnp.sin(ang)


def _mla_q_body(c_ref, pos_ref, g_ref, freq_ref, w1t_ref, w2t_ref, qt_ref, *, heads, scale):
    c = c_ref[...]
    cn = (c * lax.rsqrt(jnp.mean(c * c, axis=-1, keepdims=True) + RMS_EPS) * g_ref[...]).astype(BF16)
    nt = (((1,), (1,)), ((), ()))
    q = lax.dot_general(w1t_ref[...], cn, nt, preferred_element_type=F32)
    qp = lax.dot_general(w2t_ref[...], cn, nt, preferred_element_type=F32)
    ang = freq_ref[...] * pos_ref[...].astype(F32)
    cs, sn = jnp.cos(ang), jnp.sin(ang)
    for h in range(heads):
        a = h * MLA_HEAD_PAD
        qt_ref[a:a + LANES, :] = (q[a:a + LANES] * scale).astype(BF16)
        rot = q[a + LANES:a + 2 * LANES] * cs + qp[h * LANES:(h + 1) * LANES] * sn
        qt_ref[a + LANES:a + 2 * LANES, :] = (rot * scale).astype(BF16)


def _mla_kv_body(c_ref, kr_ref, krp_ref, pos_ref, g_ref, freq_ref, wk_ref, wvt_ref, k_ref, vt_ref,
                 *, heads):
    c = c_ref[...]
    cn = (c * lax.rsqrt(jnp.mean(c * c, axis=-1, keepdims=True) + RMS_EPS) * g_ref[...]).astype(BF16)
    kn = jnp.dot(cn, wk_ref[...], preferred_element_type=F32)
    vt_ref[...] = lax.dot_general(wvt_ref[...], cn, (((1,), (1,)), ((), ())),
                                  preferred_element_type=F32).astype(BF16)
    cs, sn = _rope_tables(pos_ref, freq_ref)
    krot = (kr_ref[...] * cs + krp_ref[...] * sn).astype(BF16)
    for h in range(heads):
        a = h * MLA_HEAD_PAD
        k_ref[:, a:a + LANES] = kn[:, h * LANES:(h + 1) * LANES].astype(BF16)
        k_ref[:, a + LANES:a + 2 * LANES] = krot


def mla_prep(z_a, pos, q_norm, kv_norm, freq, w_q1t, w_q2t, w_kn, w_vt, heads):
    m = z_a.shape[0]
    ql, kvl = w_q1t.shape[1], w_kn.shape[0]
    tm = _pick(m, 256, LANES)
    scale = (MLA_NOPE_DIM + MLA_ROPE_DIM) ** -0.5
    row = lambda i: (i, 0)
    const = lambda i: (0, 0)
    qt = pl.pallas_call(
        functools.partial(_mla_q_body, heads=heads, scale=scale),
        out_shape=jax.ShapeDtypeStruct((heads * MLA_HEAD_PAD, m), BF16),
        grid=(m // tm,),
        in_specs=[pl.BlockSpec((tm, ql), row),
                  pl.BlockSpec((1, tm), lambda i: (0, i)),
                  pl.BlockSpec((1, ql), const),
                  pl.BlockSpec((LANES, 1), const),
                  pl.BlockSpec(w_q1t.shape, const),
                  pl.BlockSpec(w_q2t.shape, const)],
        out_specs=pl.BlockSpec((heads * MLA_HEAD_PAD, tm), lambda i: (0, i)),
        compiler_params=_cparams(("parallel",)),
        name="mla_q_prep",
    )(z_a, pos.reshape(1, m), q_norm.reshape(1, ql), freq.reshape(LANES, 1), w_q1t, w_q2t)
    kr_blk = (ql + kvl) // LANES
    k, vt = pl.pallas_call(
        functools.partial(_mla_kv_body, heads=heads),
        out_shape=(jax.ShapeDtypeStruct((m, heads * MLA_HEAD_PAD), BF16),
                   jax.ShapeDtypeStruct((heads * MLA_V_DIM, m), BF16)),
        grid=(m // tm,),
        in_specs=[pl.BlockSpec((tm, kvl), lambda i: (i, ql // kvl)),
                  pl.BlockSpec((tm, LANES), lambda i: (i, kr_blk)),
                  pl.BlockSpec((tm, LANES), lambda i: (i, kr_blk + 1)),
                  pl.BlockSpec((tm, 1), row),
                  pl.BlockSpec((1, kvl), const),
                  pl.BlockSpec((1, LANES), const),
                  pl.BlockSpec(w_kn.shape, const),
                  pl.BlockSpec(w_vt.shape, const)],
        out_specs=(pl.BlockSpec((tm, heads * MLA_HEAD_PAD), row),
                   pl.BlockSpec((heads * MLA_V_DIM, tm), lambda i: (0, i))),
        compiler_params=_cparams(("parallel",)),
        name="mla_kv_prep",
    )(z_a, z_a, z_a, pos, kv_norm.reshape(1, kvl), freq, w_kn, w_vt)
    return qt, k, vt


def _mla_attn_body(qt_ref, k_ref, vt_ref, o_ref, m_ref, l_ref, acc_ref, *, tq, sub):
    del sub
    qi = pl.program_id(2)
    m_ref[...] = jnp.full_like(m_ref, NEG_INF)
    l_ref[...] = jnp.zeros_like(l_ref)
    acc_ref[...] = jnp.zeros_like(acc_ref)

    def scores(j):
        k0 = pl.multiple_of(j * tq, tq)
        return jnp.dot(k_ref[pl.ds(k0, tq), :], qt_ref[...], preferred_element_type=F32)

    def accumulate(j, st):
        k0 = pl.multiple_of(j * tq, tq)
        m_old = m_ref[...]
        m_new = jnp.maximum(m_old, jnp.max(st, axis=0, keepdims=True))
        alpha = jnp.exp(m_old - m_new)
        p = jnp.exp(st - m_new)
        l_ref[...] = alpha * l_ref[...] + jnp.sum(p, axis=0, keepdims=True)
        acc_ref[...] = alpha * acc_ref[...] + jnp.dot(vt_ref[:, pl.ds(k0, tq)], p.astype(BF16),
                                                      preferred_element_type=F32)
        m_ref[...] = m_new

    def full_tile(j, st):
        st_next = scores(j + 1)
        accumulate(j, st)
        return st_next

    st = lax.fori_loop(0, qi, full_tile, scores(0))
    tri = (lax.broadcasted_iota(jnp.int32, (tq, tq), 0)
           <= lax.broadcasted_iota(jnp.int32, (tq, tq), 1))
    accumulate(qi, jnp.where(tri, st, NEG_INF))
    o_ref[...] = (acc_ref[...] / l_ref[...]).T.astype(o_ref.dtype)


def mla_attention(qt, k, vt, batch, seq, heads):
    tq = _pick(seq, 512, LANES)
    sub = tq // 2
    nq = seq // tq
    return pl.pallas_call(
        functools.partial(_mla_attn_body, tq=tq, sub=sub),
        out_shape=jax.ShapeDtypeStruct((batch * seq, heads * MLA_V_DIM), BF16),
        grid=(batch, heads, nq),
        in_specs=[pl.BlockSpec((MLA_HEAD_PAD, tq), lambda b, h, qi: (h, b * nq + qi)),
                  pl.BlockSpec((seq, MLA_HEAD_PAD), lambda b, h, qi: (b, h)),
                  pl.BlockSpec((MLA_V_DIM, seq), lambda b, h, qi: (h, b))],
        out_specs=pl.BlockSpec((tq, MLA_V_DIM), lambda b, h, qi: (b * nq + qi, h)),
        scratch_shapes=[pltpu.VMEM((1, tq), F32), pltpu.VMEM((1, tq), F32),
                        pltpu.VMEM((MLA_V_DIM, tq), F32)],
        compiler_params=_cparams(("parallel", "parallel", "parallel")),
        name="mla_attention",
    )(qt, k, vt)


def _swa_body(sink_ref, q_ref, kp_ref, kc_ref, vp_ref, vc_ref, o_ref, *, group):
    w = SWA_WINDOW
    n = pl.program_id(1)
    scale = SWA_HEAD_DIM ** -0.5
    lane = lax.broadcasted_iota(jnp.int32, (2 * w, LANES), 1)
    lo = lane < SWA_HEAD_DIM
    qlane_lo = lax.broadcasted_iota(jnp.int32, (w, LANES), 1) < SWA_HEAD_DIM
    ri = lax.broadcasted_iota(jnp.int32, (2 * w, 2 * w), 0) % w
    cj = lax.broadcasted_iota(jnp.int32, (2 * w, 2 * w), 1)
    valid = (cj > ri) & (cj <= ri + w) & ((cj >= w) | (n > 0))
    row_first = lax.broadcasted_iota(jnp.int32, (2 * w, 1), 0) < w
    kfull = jnp.concatenate([kp_ref[...], kc_ref[...]], axis=0).astype(F32)
    vfull = jnp.concatenate([vp_ref[...], vc_ref[...]], axis=0).astype(F32)
    n_chunks = kfull.shape[1] // LANES
    for c in range(n_chunks):
        kc = kfull[:, c * LANES:(c + 1) * LANES]
        vc = vfull[:, c * LANES:(c + 1) * LANES]
        kr = pltpu.roll(kc, SWA_HEAD_DIM, axis=1)
        vr = pltpu.roll(vc, SWA_HEAD_DIM, axis=1)
        for sub in range(2):
            hk = 2 * c + sub
            k2 = (jnp.where(lo, kc, kr) if sub == 0 else jnp.where(lo, kr, kc)).astype(BF16)
            v2 = (jnp.where(lo, vc, vr) if sub == 0 else jnp.where(lo, vr, vc)).astype(BF16)
            for j in range(group // 2):
                chunk = hk * (group // 2) + j
                hq = hk * group + 2 * j
                qp = q_ref[:, chunk * LANES:(chunk + 1) * LANES].astype(F32)
                qa = jnp.where(qlane_lo, qp, 0.0)
                qb = jnp.where(qlane_lo, 0.0, qp)
                qs = jnp.concatenate([qa, qb], axis=0).astype(BF16)
                s = lax.dot_general(qs, k2, (((1,), (1,)), ((), ())),
                                    preferred_element_type=F32) * scale
                s = jnp.where(valid, s, NEG_INF)
                sink = jnp.where(row_first, sink_ref[hq], sink_ref[hq + 1])
                mx = jnp.maximum(jnp.max(s, axis=-1, keepdims=True), sink)
                p = jnp.exp(s - mx)
                denom = jnp.sum(p, axis=-1, keepdims=True) + jnp.exp(sink - mx)
                o2 = jnp.dot((p / denom).astype(BF16), v2, preferred_element_type=F32)
                o_ref[:, chunk * LANES:(chunk + 1) * LANES] = jnp.where(
                    qlane_lo, o2[:w], o2[w:]).astype(o_ref.dtype)


def swa_attention(z_s, sinks, batch, seq, q_heads, kv_heads):
    w = SWA_WINDOW
    nb = seq // w
    qw = q_heads * SWA_HEAD_DIM
    kw = kv_heads * SWA_HEAD_DIM
    assert qw % kw == 0 and kw % LANES == 0 and (q_heads // kv_heads) % 2 == 0
    kb = qw // kw
    cur = lambda blk: (lambda b, n, s: (b * nb + n, blk))
    prev = lambda blk: (lambda b, n, s: (b * nb + jnp.maximum(n - 1, 0), blk))
    return pl.pallas_call(
        functools.partial(_swa_body, group=q_heads // kv_heads),
        out_shape=jax.ShapeDtypeStruct((batch * seq, qw), BF16),
        grid_spec=pltpu.PrefetchScalarGridSpec(
            num_scalar_prefetch=1,
            grid=(batch, nb),
            in_specs=[pl.BlockSpec((w, qw), cur(0)),
                      pl.BlockSpec((w, kw), prev(kb)),
                      pl.BlockSpec((w, kw), cur(kb)),
                      pl.BlockSpec((w, kw), prev(kb + 1)),
                      pl.BlockSpec((w, kw), cur(kb + 1))],
            out_specs=pl.BlockSpec((w, qw), cur(0))),
        compiler_params=_cparams(("parallel", "arbitrary")),
        name="swa_attention",
    )(sinks, z_s, z_s, z_s, z_s, z_s)


def _gla_body(q_ref, k_ref, v_ref, g_ref, a_ref, wa_ref, ba_ref, gn_ref, o_ref,
              state_ref, b_ref, *, chunk, tokens, n_heads):
    c_sz = chunk
    dk, dv = GLA_DK, GLA_DV
    t = pl.program_id(2)

    @pl.when(t == 0)
    def _():
        state_ref[...] = jnp.zeros_like(state_ref)

    z = jnp.dot(a_ref[...], wa_ref[...], precision=HIGHEST, preferred_element_type=F32) + ba_ref[...]
    log_a = (jnp.minimum(z, 0.0) - jnp.log(1.0 + jnp.exp(-jnp.abs(z)))) * (1.0 / GLA_GATE_TAU)
    ri = lax.broadcasted_iota(jnp.int32, (tokens, tokens), 0)
    ci = lax.broadcasted_iota(jnp.int32, (tokens, tokens), 1)
    tri = ((ci <= ri) & (ci // c_sz == ri // c_sz)).astype(BF16)
    hi = log_a.astype(BF16)
    r1 = log_a - hi.astype(F32)
    mid = r1.astype(BF16)
    lo = (r1 - mid.astype(F32)).astype(BF16)
    b_ref[...] = (jnp.dot(tri, hi, preferred_element_type=F32)
                  + jnp.dot(tri, mid, preferred_element_type=F32)
                  + jnp.dot(tri, lo, preferred_element_type=F32))

    qscale = GLA_DK ** -0.5
    sub = GLA_SUB
    n_sub = c_sz // sub
    rowid = lax.broadcasted_iota(jnp.int32, (sub, 1), 0)

    def head_chunk(c0, hh):
        rows = pl.ds(c0, c_sz)
        kcols = slice(hh * dk, (hh + 1) * dk)
        vcols = slice(hh * dv, (hh + 1) * dv)
        b = b_ref[rows, kcols]
        q = q_ref[rows, kcols].astype(F32) * qscale
        k = k_ref[rows, kcols].astype(F32)
        v = v_ref[rows, vcols]
        vf = v.astype(F32)
        state_t = state_ref[hh]
        o_inter = lax.dot_general((q * jnp.exp(b)).astype(BF16), state_t.astype(BF16),
                                  (((1,), (1,)), ((), ())), preferred_element_type=F32)
        pieces = []
        for i in range(n_sub):
            r0 = i * sub
            b_i, q_i, k_i, v_i = b[r0:r0 + sub], q[r0:r0 + sub], k[r0:r0 + sub], vf[r0:r0 + sub]
            acc = jnp.zeros((sub, dv), F32)
            for j in range(sub):
                wgt = jnp.exp(b_i - b_i[j:j + 1, :])
                col = jnp.sum(q_i * k_i[j:j + 1, :] * wgt, axis=-1, keepdims=True)
                col = jnp.where(rowid >= j, col, 0.0)
                acc = acc + col * v_i[j:j + 1, :]
            if i > 0:
                ref_b = b[r0 - 1:r0, :]
                qh = (q_i * jnp.exp(b_i - ref_b)).astype(BF16)
                kh = (k[:r0] * jnp.exp(ref_b - b[:r0])).astype(BF16)
                att = lax.dot_general(qh, kh, (((1,), (1,)), ((), ())), preferred_element_type=F32)
                acc = acc + jnp.dot(att.astype(BF16), v[:r0], preferred_element_type=F32)
            pieces.append(acc)
        o = o_inter + jnp.concatenate(pieces, axis=0)
        b_last = b[c_sz - 1:c_sz, :]
        k_dec = (k * jnp.exp(b_last - b)).astype(BF16)
        upd = lax.dot_general(v, k_dec, (((0,), (0,)), ((), ())), preferred_element_type=F32)
        state_ref[hh] = state_t * jnp.exp(b_last) + upd
        y = o * lax.rsqrt(jnp.mean(o * o, axis=-1, keepdims=True) + RMS_EPS) * gn_ref[...]
        g = g_ref[rows, vcols].astype(F32)
        y = y * (g / (1.0 + jnp.exp(-g)))
        o_ref[rows, vcols] = y.astype(o_ref.dtype)

    def chunk_step(c, carry):
        c0 = pl.multiple_of(c * c_sz, c_sz)
        for hh in range(n_heads):
            head_chunk(c0, hh)
        return carry

    lax.fori_loop(0, tokens // c_sz, chunk_step, 0)


def gla_attention(z_g, z_a, a_blk, w_a2p, b_a, out_norm, batch, seq, heads, chunk=64):
    dk, dv = GLA_DK, GLA_DV
    tokens = _pick(seq, 256, chunk)
    nt = seq // tokens
    nh = 2
    assert heads % nh == 0
    wk, wv = nh * dk, nh * dv
    k_off = heads * dk // wk
    v_off = 2 * heads * dk // wv
    g_off = v_off + heads * dv // wv
    row = lambda blk_fn: (lambda b, h, t: (b * nt + t, blk_fn(h)))
    return pl.pallas_call(
        functools.partial(_gla_body, chunk=chunk, tokens=tokens, n_heads=nh),
        out_shape=jax.ShapeDtypeStruct((batch * seq, heads * dv), BF16),
        grid=(batch, heads // nh, nt),
        in_specs=[pl.BlockSpec((tokens, wk), row(lambda h: h)),
                  pl.BlockSpec((tokens, wk), row(lambda h: k_off + h)),
                  pl.BlockSpec((tokens, wv), row(lambda h: v_off + h)),
                  pl.BlockSpec((tokens, wv), row(lambda h: g_off + h)),
                  pl.BlockSpec((tokens, LANES), row(lambda h: a_blk)),
                  pl.BlockSpec((LANES, wk), lambda b, h, t: (0, h)),
                  pl.BlockSpec((1, wk), lambda b, h, t: (0, h)),
                  pl.BlockSpec((1, dv), lambda b, h, t: (0, 0))],
        out_specs=pl.BlockSpec((tokens, wv), row(lambda h: h)),
        scratch_shapes=[pltpu.VMEM((nh, dv, dk), F32), pltpu.VMEM((tokens, wk), F32)],
        compiler_params=_cparams(("parallel", "parallel", "arbitrary")),
        name="gla_attention",
    )(z_g, z_g, z_g, z_g, z_a, w_a2p, b_a.reshape(1, -1), out_norm.reshape(1, dv))


def _merge_body(h_ref, y0_ref, y1_ref, y2_ref, wg0_ref, wg1_ref, wg2_ref,
                wb0_ref, wb1_ref, wb2_ref, bias_ref, o_ref):
    h = h_ref[...]
    acc = None
    branches = ((y0_ref, wg0_ref, wb0_ref), (y1_ref, wg1_ref, wb1_ref), (y2_ref, wg2_ref, wb2_ref))
    for b, (y_ref, wg_ref, wb_ref) in enumerate(branches):
        logit = jnp.dot(h, wg_ref[...], preferred_element_type=F32) + bias_ref[b:b + 1, :]
        gate = 1.0 / (1.0 + jnp.exp(-logit))
        term = gate * jnp.dot(y_ref[...], wb_ref[...], preferred_element_type=F32)
        acc = term if acc is None else acc + term
    o_ref[...] = acc.astype(o_ref.dtype)


def merge_branches(h, ys, w_gates, w_brs, bias3):
    m, d = h.shape
    bw = ys[0].shape[1]
    tm = _pick(m, 512, 16)
    tn = _pick(d, 256)
    nj = d // tn
    gate_spec = lambda b: pl.BlockSpec((d, tn), lambda j, i: (0, b * nj + j))
    return pl.pallas_call(
        _merge_body,
        out_shape=jax.ShapeDtypeStruct((m, d), BF16),
        grid=(nj, m // tm),
        in_specs=[pl.BlockSpec((tm, d), lambda j, i: (i, 0))]
        + [pl.BlockSpec((tm, bw), lambda j, i: (i, 0))] * N_BRANCHES
        + [gate_spec(b) for b in range(N_BRANCHES)]
        + [pl.BlockSpec((bw, tn), lambda j, i: (0, j))] * N_BRANCHES
        + [pl.BlockSpec((N_BRANCHES, tn), lambda j, i: (0, j))],
        out_specs=pl.BlockSpec((tm, tn), lambda j, i: (i, j)),
        compiler_params=_cparams(("parallel", "parallel")),
        name="merge_branches",
    )(h, *ys, w_gates, w_gates, w_gates, *w_brs, bias3)


def _router_body(x_ref, g_ref, w_ref, b_ref, h_ref, e_ref, wt_ref, rk_ref, cnt_ref, carry_ref,
                 *, n_exp, n_grp, per_grp):
    i = pl.program_id(0)

    @pl.when(i == 0)
    def _():
        carry_ref[...] = jnp.zeros_like(carry_ref)

    x = x_ref[...]
    tm = x.shape[0]
    hn = x * lax.rsqrt(jnp.mean(x * x, axis=-1, keepdims=True) + RMS_EPS) * g_ref[...]
    h_ref[...] = hn
    logits = jnp.dot(hn, w_ref[...], precision=HIGHEST, preferred_element_type=F32) + b_ref[...]
    lane = lax.broadcasted_iota(jnp.int32, (tm, LANES), 1)
    big = jnp.int32(LANES)

    def top1(vals):
        mx = jnp.max(vals, axis=-1, keepdims=True)
        idx = jnp.min(jnp.where(vals == mx, lane, big), axis=-1, keepdims=True)
        return mx, idx

    gl = jnp.where((lane >= n_exp) & (lane < n_exp + n_grp), logits, NEG_INF)
    gmax, gidx = top1(gl)
    g_top_p = 1.0 / jnp.sum(jnp.exp(gl - gmax), axis=-1, keepdims=True)
    grp = gidx - n_exp
    el = jnp.where((lane >= grp * per_grp) & (lane < (grp + 1) * per_grp), logits, NEG_INF)
    m1, i1 = top1(el)
    m2, i2 = top1(jnp.where(lane == i1, NEG_INF, el))
    e21 = jnp.exp(m2 - m1)
    w1 = g_top_p / (1.0 + e21)
    w2 = w1 * e21
    e_ref[...] = jnp.where(lane == 0, i1, jnp.where(lane == 1, i2, 0))[:, :8]
    wt_ref[...] = jnp.where(lane == 0, w1, jnp.where(lane == 1, w2, 0.0))[:, :8]

    oh1 = (lane == i1).astype(F32)
    oh2 = (lane == i2).astype(F32)
    ri = lax.broadcasted_iota(jnp.int32, (tm, tm), 0)
    ci = lax.broadcasted_iota(jnp.int32, (tm, tm), 1)
    strict = (ci < ri).astype(BF16)
    cum1 = jnp.dot(strict, oh1.astype(BF16), preferred_element_type=F32)
    cum2 = jnp.dot(strict, oh2.astype(BF16), preferred_element_type=F32)
    tot1 = jnp.sum(oh1, axis=0, keepdims=True)
    tot2 = jnp.sum(oh2, axis=0, keepdims=True)
    carry = carry_ref[...]
    r1 = jnp.sum(oh1 * (cum1 + carry), axis=-1, keepdims=True)
    r2 = jnp.sum(oh2 * (cum2 + carry + tot1), axis=-1, keepdims=True)
    rk_ref[...] = jnp.where(lane == 0, r1, jnp.where(lane == 1, r2, 0.0))[:, :8].astype(jnp.int32)
    carry = carry + tot1 + tot2
    carry_ref[...] = carry
    cnt_ref[...] = jnp.broadcast_to(carry, cnt_ref.shape).astype(jnp.int32)


def moe_router(x, gain, w_r, b_r, n_exp, n_grp):
    m, d = x.shape
    tm = _pick(m, 256, 8)
    row = lambda i: (i, 0)
    const = lambda i: (0, 0)
    return pl.pallas_call(
        functools.partial(_router_body, n_exp=n_exp, n_grp=n_grp, per_grp=n_exp // n_grp),
        out_shape=(jax.ShapeDtypeStruct((m, d), F32),
                   jax.ShapeDtypeStruct((m, 8), jnp.int32),
                   jax.ShapeDtypeStruct((m, 8), F32),
                   jax.ShapeDtypeStruct((m, 8), jnp.int32),
                   jax.ShapeDtypeStruct((8, LANES), jnp.int32)),
        grid=(m // tm,),
        in_specs=[pl.BlockSpec((tm, d), row), pl.BlockSpec((1, d), const),
                  pl.BlockSpec((d, LANES), const), pl.BlockSpec((1, LANES), const)],
        out_specs=(pl.BlockSpec((tm, d), row), pl.BlockSpec((tm, 8), row),
                   pl.BlockSpec((tm, 8), row), pl.BlockSpec((tm, 8), row),
                   pl.BlockSpec((8, LANES), const)),
        scratch_shapes=[pltpu.VMEM((1, LANES), F32)],
        compiler_params=_cparams(("arbitrary",)),
        name="moe_router",
    )(x, gain.reshape(1, d), w_r, b_r)


def _dispatch_body(p1_ref, p2_ref, h_ref, xs_in_ref, xs_ref, sem, *, tokens):
    del xs_in_ref
    base = pl.program_id(0) * tokens

    def copies(t):
        src = h_ref.at[pl.ds(t, 1), :]
        return (pltpu.make_async_copy(src, xs_ref.at[pl.ds(p1_ref[base + t], 1), :], sem),
                pltpu.make_async_copy(src, xs_ref.at[pl.ds(p2_ref[base + t], 1), :], sem))

    def start(t, c):
        a, b = copies(t)
        a.start()
        b.start()
        return c

    def wait(t, c):
        a, b = copies(t)
        a.wait()
        b.wait()
        return c

    lax.fori_loop(0, tokens, start, 0, unroll=8)
    lax.fori_loop(0, tokens, wait, 0, unroll=8)


def moe_dispatch(h, pos1, pos2, rows):
    m, d = h.shape
    tokens = _pick(m, 256, 8)
    xs0 = jnp.zeros((rows, d), F32)
    return pl.pallas_call(
        functools.partial(_dispatch_body, tokens=tokens),
        out_shape=jax.ShapeDtypeStruct((rows, d), F32),
        grid_spec=pltpu.PrefetchScalarGridSpec(
            num_scalar_prefetch=2,
            grid=(m // tokens,),
            in_specs=[pl.BlockSpec((tokens, d), lambda i, p1, p2: (i, 0)),
                      pl.BlockSpec(memory_space=pl.ANY)],
            out_specs=pl.BlockSpec(memory_space=pl.ANY),
            scratch_shapes=[pltpu.SemaphoreType.DMA(())]),
        input_output_aliases={3: 0},
        compiler_params=_cparams(("arbitrary",)),
        name="moe_dispatch",
    )(pos1, pos2, h, xs0)


def _experts_up_body(te_ref, ts_ref, tv_ref, x_ref, wg_ref, wu_ref, h_ref):
    del te_ref, ts_ref
    i = pl.program_id(0)

    @pl.when(tv_ref[i] > 0)
    def _():
        x = x_ref[...].astype(BF16)
        a = jnp.dot(x, wg_ref[...].astype(BF16), preferred_element_type=F32)
        u = jnp.dot(x, wu_ref[...].astype(BF16), preferred_element_type=F32)
        h_ref[...] = ((a / (1.0 + jnp.exp(-a))) * u).astype(h_ref.dtype)

    @pl.when(tv_ref[i] == 0)
    def _():
        h_ref[...] = jnp.zeros_like(h_ref)


def _experts_down_body(te_ref, ts_ref, tv_ref, h_ref, wd_ref, o_ref):
    del te_ref, ts_ref
    i = pl.program_id(0)

    @pl.when(tv_ref[i] > 0)
    def _():
        o_ref[...] = jnp.dot(h_ref[...], wd_ref[...].astype(BF16), preferred_element_type=F32)

    @pl.when(tv_ref[i] == 0)
    def _():
        o_ref[...] = jnp.zeros_like(o_ref)


def moe_experts(xs, tile_expert, tile_src, tile_valid, w_gate, w_up, w_down, layer, tm):
    rows, d = xs.shape
    ff = w_gate.shape[3]
    n_tiles = rows // tm
    tf = _pick(ff, 256)
    td = _pick(d, 1024)
    nf, nd = ff // tf, d // td
    prefetch = (tile_expert, tile_src, tile_valid)

    def up_w(i, j, te, ts, tv):
        return (layer, te[i], 0, jnp.where(tv[i] > 0, j, nf - 1))

    hid = pl.pallas_call(
        _experts_up_body,
        out_shape=jax.ShapeDtypeStruct((rows, ff), BF16),
        grid_spec=pltpu.PrefetchScalarGridSpec(
            num_scalar_prefetch=3,
            grid=(n_tiles, nf),
            in_specs=[pl.BlockSpec((tm, d), lambda i, j, te, ts, tv: (ts[i], 0)),
                      pl.BlockSpec((None, None, d, tf), up_w),
                      pl.BlockSpec((None, None, d, tf), up_w)],
            out_specs=pl.BlockSpec((tm, tf), lambda i, j, te, ts, tv: (i, j))),
        compiler_params=_cparams(("arbitrary", "arbitrary")),
        name="moe_experts_up",
    )(*prefetch, xs, w_gate, w_up)
    return pl.pallas_call(
        _experts_down_body,
        out_shape=jax.ShapeDtypeStruct((rows, d), F32),
        grid_spec=pltpu.PrefetchScalarGridSpec(
            num_scalar_prefetch=3,
            grid=(n_tiles, nd),
            in_specs=[pl.BlockSpec((tm, ff), lambda i, j, te, ts, tv: (i, 0)),
                      pl.BlockSpec((None, None, ff, td),
                                   lambda i, j, te, ts, tv: (layer, te[i], 0, jnp.where(tv[i] > 0, j, nd - 1)))],
            out_specs=pl.BlockSpec((tm, td), lambda i, j, te, ts, tv: (i, j))),
        compiler_params=_cparams(("arbitrary", "arbitrary")),
        name="moe_experts_down",
    )(*prefetch, hid, w_down)


def _combine_body(p1_ref, p2_ref, x_ref, wt_ref, g_ref, ys_ref, *rest, tokens, emit_x):
    if emit_x:
        xo_ref, no_ref, buf_ref, sem = rest
    else:
        no_ref, buf_ref, sem = rest
    base = pl.program_id(0) * tokens

    def copies(t):
        return (pltpu.make_async_copy(ys_ref.at[pl.ds(p1_ref[base + t], 1), :],
                                      buf_ref.at[0, pl.ds(t, 1), :], sem),
                pltpu.make_async_copy(ys_ref.at[pl.ds(p2_ref[base + t], 1), :],
                                      buf_ref.at[1, pl.ds(t, 1), :], sem))

    def start(t, c):
        a, b = copies(t)
        a.start()
        b.start()
        return c

    def wait(t, c):
        a, b = copies(t)
        a.wait()
        b.wait()
        return c

    lax.fori_loop(0, tokens, start, 0, unroll=8)
    lax.fori_loop(0, tokens, wait, 0, unroll=8)
    wt = wt_ref[...]
    x = x_ref[...] + wt[:, 0:1] * buf_ref[0] + wt[:, 1:2] * buf_ref[1]
    if emit_x:
        xo_ref[...] = x
    y = x * lax.rsqrt(jnp.mean(x * x, axis=-1, keepdims=True) + RMS_EPS)
    no_ref[...] = (y * g_ref[...]).astype(no_ref.dtype)


def moe_combine(x, ys, pos1, pos2, wts, gain, norm_dtype, emit_x):
    m, d = x.shape
    tokens = _pick(m, 256, 8)
    row = lambda i, p1, p2: (i, 0)
    out_shape = [jax.ShapeDtypeStruct((m, d), norm_dtype)]
    out_specs = [pl.BlockSpec((tokens, d), row)]
    if emit_x:
        out_shape.insert(0, jax.ShapeDtypeStruct((m, d), F32))
        out_specs.insert(0, pl.BlockSpec((tokens, d), row))
    return pl.pallas_call(
        functools.partial(_combine_body, tokens=tokens, emit_x=emit_x),
        out_shape=tuple(out_shape),
        grid_spec=pltpu.PrefetchScalarGridSpec(
            num_scalar_prefetch=2,
            grid=(m // tokens,),
            in_specs=[pl.BlockSpec((tokens, d), row),
                      pl.BlockSpec((tokens, 8), row),
                      pl.BlockSpec((1, d), lambda i, p1, p2: (0, 0)),
                      pl.BlockSpec(memory_space=pl.ANY)],
            out_specs=tuple(out_specs),
            scratch_shapes=[pltpu.VMEM((2, tokens, d), F32), pltpu.SemaphoreType.DMA(())]),
        compiler_params=_cparams(("arbitrary",)),
        name="moe_combine",
    )(pos1, pos2, x, wts, gain.reshape(1, d), ys)


def _rope_partner(w_rot):
    half = MLA_ROPE_DIM // 2
    return jnp.concatenate([-w_rot[..., half:], w_rot[..., :half]], axis=-1)


def _layer_weights(w_in, w_uq, heads_mla):
    d = w_in.shape[0]
    gla_qk = GLA_HEADS * GLA_DK
    gla_w = GLA_HEADS * GLA_DV
    swa_q = SWA_Q_HEADS * SWA_HEAD_DIM
    swa_kv = SWA_KV_HEADS * SWA_HEAD_DIM
    sizes = (MLA_Q_LORA, MLA_KV_LORA, MLA_ROPE_DIM, gla_qk, gla_qk, gla_w, GLA_GATE_RANK, gla_w,
             swa_q, swa_kv, swa_kv, N_BRANCHES * d)
    offs = np.concatenate([[0], np.cumsum(sizes)])
    seg = lambda i: w_in[:, offs[i]:offs[i + 1]]
    zpad = lambda n: jnp.zeros((d, n), w_in.dtype)
    kr = seg(2)
    w_a = jnp.concatenate([seg(0), seg(1), kr, zpad(LANES - MLA_ROPE_DIM),
                           _rope_partner(kr), zpad(LANES - MLA_ROPE_DIM),
                           seg(6), zpad(LANES - GLA_GATE_RANK)], axis=1).astype(BF16)
    w_g = jnp.concatenate([seg(3), seg(4), seg(5), seg(7)], axis=1).astype(BF16)
    w_s = w_in[:, offs[8]:offs[11]].astype(BF16)
    w_gates = seg(11).astype(BF16)
    qk = MLA_NOPE_DIM + MLA_ROPE_DIM
    wq = w_uq.reshape(-1, heads_mla, qk)
    pad = jnp.zeros((wq.shape[0], heads_mla, MLA_HEAD_PAD - qk), w_uq.dtype)
    w_q1 = jnp.concatenate([wq, pad], axis=-1).reshape(wq.shape[0], -1).astype(BF16)
    rot = wq[..., MLA_NOPE_DIM:]
    pad2 = jnp.zeros((wq.shape[0], heads_mla, LANES - MLA_ROPE_DIM), w_uq.dtype)
    w_q2 = jnp.concatenate([_rope_partner(rot), pad2], axis=-1).reshape(wq.shape[0], -1).astype(BF16)
    return w_a, w_g, w_s, w_gates, w_q1.T, w_q2.T


def _moe_layer(x, gain, w_rg, b_rg, w_re, b_re, w_gate, w_up, w_down, layer, next_gain, norm_dtype,
               emit_x):
    m, d = x.shape
    n_exp, n_grp = w_re.shape[1], w_rg.shape[1]
    w_r = jnp.concatenate([w_re, w_rg, jnp.zeros((d, LANES - n_exp - n_grp), F32)], axis=1)
    b_r = jnp.concatenate([b_re, b_rg, jnp.zeros((LANES - n_exp - n_grp,), F32)]).reshape(1, LANES)
    hn, e_idx, wts, rank, counts = moe_router(x, gain, w_r, b_r, n_exp, n_grp)
    tm = 512 if m >= 2048 else 64
    cnt = counts[0, :n_exp]
    padded = ((cnt + tm - 1) // tm) * tm
    ends = jnp.cumsum(padded)
    offs = ends - padded
    n_tiles = (TOP_K_IN_GROUP * m) // tm + n_exp
    tile_start = jnp.arange(n_tiles, dtype=jnp.int32) * tm
    tile_valid = (tile_start < ends[-1]).astype(jnp.int32)
    last_tile = jnp.maximum(ends[-1] // tm - 1, 0).astype(jnp.int32)
    tile_src = jnp.minimum(jnp.arange(n_tiles, dtype=jnp.int32), last_tile)
    tile_expert = jnp.sum((ends[None, :] <= (tile_src * tm)[:, None]).astype(jnp.int32), axis=1)
    tile_expert = jnp.minimum(tile_expert, n_exp - 1)
    pos = offs[e_idx[:, :2]].astype(jnp.int32) + rank[:, :2]
    pos1, pos2 = pos[:, 0], pos[:, 1]
    xs = moe_dispatch(hn, pos1, pos2, n_tiles * tm)
    ys = moe_experts(xs, tile_expert, tile_src, tile_valid, w_gate, w_up, w_down, layer, tm)
    return moe_combine(x, ys, pos1, pos2, wts, next_gain, norm_dtype, emit_x)


def kernel(x, positions, attn_norm, w_in, gate_bias, mla_q_norm, mla_w_uq, mla_kv_norm, mla_w_ukv,
           gla_w_a2, gla_b_a, gla_out_norm, swa_sinks, w_br_mla, w_br_gla, w_br_swa, w_out,
           ffn_norm, w_router_group, b_router_group, w_router_expert, b_router_expert,
           w_gate, w_up, w_down, final_norm):
    batch, seq, d = x.shape
    depth = w_in.shape[0]
    m = batch * seq
    heads_mla = mla_w_uq.shape[2] // (MLA_NOPE_DIM + MLA_ROPE_DIM)
    xf = x.reshape(m, d)
    pos = positions.reshape(m, 1).astype(jnp.int32)
    half = MLA_ROPE_DIM // 2
    inv_freq = ROPE_THETA ** (-jnp.arange(0, MLA_ROPE_DIM, 2, dtype=F32) / MLA_ROPE_DIM)
    freq = jnp.concatenate([inv_freq, inv_freq, jnp.zeros((LANES - 2 * half,), F32)]).reshape(1, LANES)
    a_blk = (MLA_Q_LORA + MLA_KV_LORA) // LANES + 2

    h = rmsnorm(xf, attn_norm[0], BF16)
    out = None
    for l in range(depth):
        w_a, w_g, w_s, w_gates, w_q1t, w_q2t = _layer_weights(w_in[l], mla_w_uq[l], heads_mla)
        z_a = matmul(h, w_a, F32, name="in_proj_mla")
        z_g = matmul(h, w_g, BF16, name="in_proj_gla")
        z_s = matmul(h, w_s, BF16, name="in_proj_swa")
        w_kv = mla_w_ukv[l].reshape(-1, heads_mla, MLA_NOPE_DIM + MLA_V_DIM)
        w_kn = w_kv[..., :MLA_NOPE_DIM].reshape(w_kv.shape[0], -1).astype(BF16)
        w_vt = w_kv[..., MLA_NOPE_DIM:].reshape(w_kv.shape[0], -1).T.astype(BF16)
        qt, k, vt = mla_prep(z_a, pos, mla_q_norm[l], mla_kv_norm[l], freq, w_q1t, w_q2t,
                             w_kn, w_vt, heads_mla)
        y_mla = mla_attention(qt, k, vt, batch, seq, heads_mla)
        w_a2p = jnp.concatenate([gla_w_a2[l], jnp.zeros((LANES - GLA_GATE_RANK, gla_w_a2.shape[2]), F32)], axis=0)
        y_gla = gla_attention(z_g, z_a, a_blk, w_a2p, gla_b_a[l], gla_out_norm[l], batch, seq, GLA_HEADS)
        y_swa = swa_attention(z_s, swa_sinks[l], batch, seq, SWA_Q_HEADS, SWA_KV_HEADS)
        w_brs = (w_br_mla[l].astype(BF16), w_br_gla[l].astype(BF16), w_br_swa[l].astype(BF16))
        merged = merge_branches(h, (y_mla, y_gla, y_swa), w_gates, w_brs, gate_bias[l])
        xf = matmul(merged, w_out[l].astype(BF16), F32, residual=xf, name="out_proj")
        last = l == depth - 1
        res = _moe_layer(xf, ffn_norm[l], w_router_group[l], b_router_group[l],
                         w_router_expert[l], b_router_expert[l], w_gate, w_up, w_down, l,
                         final_norm if last else attn_norm[l + 1],
                         F32 if last else BF16, emit_x=not last)
        if last:
            out = res[0]
        else:
            xf, h = res
    return out.reshape(batch, seq, d)
```

```python
import functools
import math

import numpy as np
import jax
import jax.numpy as jnp
from jax import lax
from jax.experimental import pallas as pl
from jax.experimental.pallas import tpu as pltpu

F32 = jnp.float32
BF16 = jnp.bfloat16
HIGHEST = lax.Precision.HIGHEST

RMS_EPS = 1e-6
NEG_INF = -1e30
MLA_HEADS = 16
MLA_Q_LORA = 1024
MLA_KV_LORA = 512
MLA_NOPE_DIM = 128
MLA_ROPE_DIM = 64
MLA_V_DIM = 128
ROPE_THETA = 10000.0
GLA_HEADS = 8
GLA_DK = 128
GLA_DV = 256
GLA_GATE_RANK = 16
GLA_GATE_TAU = 16.0
SWA_Q_HEADS = 32
SWA_KV_HEADS = 4
SWA_HEAD_DIM = 64
SWA_WINDOW = 128
N_BRANCHES = 3
N_GROUPS = 4
EXPERTS_PER_GROUP = 8
TOP_K_IN_GROUP = 2

LANES = 128
MLA_HEAD_PAD = 256
GLA_SUB = 16
VMEM_LIMIT = 60 * 1024 * 1024


def _cparams(sem, vmem=VMEM_LIMIT):
    return pltpu.CompilerParams(dimension_semantics=sem, vmem_limit_bytes=vmem)


def _pick(n, cap, mult=LANES):
    best = None
    for t in range(mult, min(n, cap) + 1, mult):
        if n % t == 0:
            best = t
    assert best is not None, (n, cap, mult)
    return best


def _rmsnorm_body(x_ref, g_ref, o_ref):
    x = x_ref[...]
    y = x * lax.rsqrt(jnp.mean(x * x, axis=-1, keepdims=True) + RMS_EPS)
    o_ref[...] = (y * g_ref[...]).astype(o_ref.dtype)


def rmsnorm(x, gain, out_dtype):
    m, d = x.shape
    tm = _pick(m, 512, 8)
    return pl.pallas_call(
        _rmsnorm_body,
        out_shape=jax.ShapeDtypeStruct((m, d), out_dtype),
        grid=(m // tm,),
        in_specs=[pl.BlockSpec((tm, d), lambda i: (i, 0)),
                  pl.BlockSpec((1, d), lambda i: (0, 0))],
        out_specs=pl.BlockSpec((tm, d), lambda i: (i, 0)),
        compiler_params=_cparams(("parallel",)),
        name="rmsnorm",
    )(x, gain.reshape(1, d))


def _matmul_body(a_ref, w_ref, o_ref):
    o_ref[...] = jnp.dot(a_ref[...], w_ref[...], preferred_element_type=F32).astype(o_ref.dtype)


def _matmul_res_body(a_ref, w_ref, r_ref, o_ref):
    acc = jnp.dot(a_ref[...], w_ref[...], preferred_element_type=F32)
    o_ref[...] = (r_ref[...] + acc).astype(o_ref.dtype)


def matmul(a, w, out_dtype, residual=None, tm_cap=512, tn_cap=1024, name="matmul"):
    m, k = a.shape
    n = w.shape[1]
    tm = _pick(m, tm_cap, 16)
    tn = _pick(n, tn_cap)
    in_specs = [pl.BlockSpec((tm, k), lambda j, i: (i, 0)),
                pl.BlockSpec((k, tn), lambda j, i: (0, j))]
    args = [a, w]
    body = _matmul_body
    if residual is not None:
        in_specs.append(pl.BlockSpec((tm, tn), lambda j, i: (i, j)))
        args.append(residual)
        body = _matmul_res_body
    return pl.pallas_call(
        body,
        out_shape=jax.ShapeDtypeStruct((m, n), out_dtype),
        grid=(n // tn, m // tm),
        in_specs=in_specs,
        out_specs=pl.BlockSpec((tm, tn), lambda j, i: (i, j)),
        compiler_params=_cparams(("parallel", "parallel")),
        name=name,
    )(*args)


def _rope_tables(pos_ref, freq_ref):
    ang = pos_ref[...].astype(F32) * freq_ref[...]
    return jnp.cos(ang), jnp.sin(ang)


def _mla_q_body(c_ref, pos_ref, g_ref, freq_ref, w1t_ref, w2t_ref, qt_ref, *, heads, scale):
    c = c_ref[...]
    cn = (c * lax.rsqrt(jnp.mean(c * c, axis=-1, keepdims=True) + RMS_EPS) * g_ref[...]).astype(BF16)
    nt = (((1,), (1,)), ((), ()))
    q = lax.dot_general(w1t_ref[...], cn, nt, preferred_element_type=F32)
    qp = lax.dot_general(w2t_ref[...], cn, nt, preferred_element_type=F32)
    ang = freq_ref[...] * pos_ref[...].astype(F32)
    cs, sn = jnp.cos(ang), jnp.sin(ang)
    for h in range(heads):
        a = h * MLA_HEAD_PAD
        qt_ref[a:a + LANES, :] = (q[a:a + LANES] * scale).astype(BF16)
        rot = q[a + LANES:a + 2 * LANES] * cs + qp[h * LANES:(h + 1) * LANES] * sn
        qt_ref[a + LANES:a + 2 * LANES, :] = (rot * scale).astype(BF16)


def _mla_kv_body(c_ref, kr_ref, krp_ref, pos_ref, g_ref, freq_ref, wk_ref, wvt_ref, k_ref, vt_ref,
                 *, heads):
    c = c_ref[...]
    cn = (c * lax.rsqrt(jnp.mean(c * c, axis=-1, keepdims=True) + RMS_EPS) * g_ref[...]).astype(BF16)
    kn = jnp.dot(cn, wk_ref[...], preferred_element_type=F32)
    vt_ref[...] = lax.dot_general(wvt_ref[...], cn, (((1,), (1,)), ((), ())),
                                  preferred_element_type=F32).astype(BF16)
    cs, sn = _rope_tables(pos_ref, freq_ref)
    krot = (kr_ref[...] * cs + krp_ref[...] * sn).astype(BF16)
    for h in range(heads):
        a = h * MLA_HEAD_PAD
        k_ref[:, a:a + LANES] = kn[:, h * LANES:(h + 1) * LANES].astype(BF16)
        k_ref[:, a + LANES:a + 2 * LANES] = krot


def mla_prep(z_a, pos, q_norm, kv_norm, freq, w_q1t, w_q2t, w_kn, w_vt, heads):
    m = z_a.shape[0]
    ql, kvl = w_q1t.shape[1], w_kn.shape[0]
    tm = _pick(m, 256, LANES)
    scale = (MLA_NOPE_DIM + MLA_ROPE_DIM) ** -0.5
    row = lambda i: (i, 0)
    const = lambda i: (0, 0)
    qt = pl.pallas_call(
        functools.partial(_mla_q_body, heads=heads, scale=scale),
        out_shape=jax.ShapeDtypeStruct((heads * MLA_HEAD_PAD, m), BF16),
        grid=(m // tm,),
        in_specs=[pl.BlockSpec((tm, ql), row),
                  pl.BlockSpec((1, tm), lambda i: (0, i)),
                  pl.BlockSpec((1, ql), const),
                  pl.BlockSpec((LANES, 1), const),
                  pl.BlockSpec(w_q1t.shape, const),
                  pl.BlockSpec(w_q2t.shape, const)],
        out_specs=pl.BlockSpec((heads * MLA_HEAD_PAD, tm), lambda i: (0, i)),
        compiler_params=_cparams(("parallel",)),
        name="mla_q_prep",
    )(z_a, pos.reshape(1, m), q_norm.reshape(1, ql), freq.reshape(LANES, 1), w_q1t, w_q2t)
    kr_blk = (ql + kvl) // LANES
    k, vt = pl.pallas_call(
        functools.partial(_mla_kv_body, heads=heads),
        out_shape=(jax.ShapeDtypeStruct((m, heads * MLA_HEAD_PAD), BF16),
                   jax.ShapeDtypeStruct((heads * MLA_V_DIM, m), BF16)),
        grid=(m // tm,),
        in_specs=[pl.BlockSpec((tm, kvl), lambda i: (i, ql // kvl)),
                  pl.BlockSpec((tm, LANES), lambda i: (i, kr_blk)),
                  pl.BlockSpec((tm, LANES), lambda i: (i, kr_blk + 1)),
                  pl.BlockSpec((tm, 1), row),
                  pl.BlockSpec((1, kvl), const),
                  pl.BlockSpec((1, LANES), const),
                  pl.BlockSpec(w_kn.shape, const),
                  pl.BlockSpec(w_vt.shape, const)],
        out_specs=(pl.BlockSpec((tm, heads * MLA_HEAD_PAD), row),
                   pl.BlockSpec((heads * MLA_V_DIM, tm), lambda i: (0, i))),
        compiler_params=_cparams(("parallel",)),
        name="mla_kv_prep",
    )(z_a, z_a, z_a, pos, kv_norm.reshape(1, kvl), freq, w_kn, w_vt)
    return qt, k, vt


def _mla_attn_body(qt_ref, k_ref, vt_ref, o_ref, m_ref, l_ref, acc_ref, *, tq, sub):
    del sub
    qi = pl.program_id(2)
    m_ref[...] = jnp.full_like(m_ref, NEG_INF)
    l_ref[...] = jnp.zeros_like(l_ref)
    acc_ref[...] = jnp.zeros_like(acc_ref)

    def scores(j):
        k0 = pl.multiple_of(j * tq, tq)
        return jnp.dot(k_ref[pl.ds(k0, tq), :], qt_ref[...], preferred_element_type=F32)

    def accumulate(j, st):
        k0 = pl.multiple_of(j * tq, tq)
        m_old = m_ref[...]
        m_new = jnp.maximum(m_old, jnp.max(st, axis=0, keepdims=True))
        alpha = jnp.exp(m_old - m_new)
        p = jnp.exp(st - m_new)
        l_ref[...] = alpha * l_ref[...] + jnp.sum(p, axis=0, keepdims=True)
        acc_ref[...] = alpha * acc_ref[...] + jnp.dot(vt_ref[:, pl.ds(k0, tq)], p.astype(BF16),
                                                      preferred_element_type=F32)
        m_ref[...] = m_new

    def full_tile(j, st):
        st_next = scores(j + 1)
        accumulate(j, st)
        return st_next

    st = lax.fori_loop(0, qi, full_tile, scores(0))
    tri = (lax.broadcasted_iota(jnp.int32, (tq, tq), 0)
           <= lax.broadcasted_iota(jnp.int32, (tq, tq), 1))
    accumulate(qi, jnp.where(tri, st, NEG_INF))
    o_ref[...] = (acc_ref[...] / l_ref[...]).T.astype(o_ref.dtype)


def mla_attention(qt, k, vt, batch, seq, heads):
    tq = _pick(seq, 512, LANES)
    sub = tq // 2
    nq = seq // tq
    return pl.pallas_call(
        functools.partial(_mla_attn_body, tq=tq, sub=sub),
        out_shape=jax.ShapeDtypeStruct((batch * seq, heads * MLA_V_DIM), BF16),
        grid=(batch, heads, nq),
        in_specs=[pl.BlockSpec((MLA_HEAD_PAD, tq), lambda b, h, qi: (h, b * nq + qi)),
                  pl.BlockSpec((seq, MLA_HEAD_PAD), lambda b, h, qi: (b, h)),
                  pl.BlockSpec((MLA_V_DIM, seq), lambda b, h, qi: (h, b))],
        out_specs=pl.BlockSpec((tq, MLA_V_DIM), lambda b, h, qi: (b * nq + qi, h)),
        scratch_shapes=[pltpu.VMEM((1, tq), F32), pltpu.VMEM((1, tq), F32),
                        pltpu.VMEM((MLA_V_DIM, tq), F32)],
        compiler_params=_cparams(("parallel", "parallel", "parallel")),
        name="mla_attention",
    )(qt, k, vt)


def _swa_body(sink_ref, q_ref, kp_ref, kc_ref, vp_ref, vc_ref, o_ref, *, group):
    w = SWA_WINDOW
    n = pl.program_id(1)
    scale = SWA_HEAD_DIM ** -0.5
    lane = lax.broadcasted_iota(jnp.int32, (2 * w, LANES), 1)
    lo = lane < SWA_HEAD_DIM
    qlane_lo = lax.broadcasted_iota(jnp.int32, (w, LANES), 1) < SWA_HEAD_DIM
    ri = lax.broadcasted_iota(jnp.int32, (2 * w, 2 * w), 0) % w
    cj = lax.broadcasted_iota(jnp.int32, (2 * w, 2 * w), 1)
    valid = (cj > ri) & (cj <= ri + w) & ((cj >= w) | (n > 0))
    row_first = lax.broadcasted_iota(jnp.int32, (2 * w, 1), 0) < w
    kfull = jnp.concatenate([kp_ref[...], kc_ref[...]], axis=0).astype(F32)
    vfull = jnp.concatenate([vp_ref[...], vc_ref[...]], axis=0).astype(F32)
    n_chunks = kfull.shape[1] // LANES
    for c in range(n_chunks):
        kc = kfull[:, c * LANES:(c + 1) * LANES]
        vc = vfull[:, c * LANES:(c + 1) * LANES]
        kr = pltpu.roll(kc, SWA_HEAD_DIM, axis=1)
        vr = pltpu.roll(vc, SWA_HEAD_DIM, axis=1)
        for sub in range(2):
            hk = 2 * c + sub
            k2 = (jnp.where(lo, kc, kr) if sub == 0 else jnp.where(lo, kr, kc)).astype(BF16)
            v2 = (jnp.where(lo, vc, vr) if sub == 0 else jnp.where(lo, vr, vc)).astype(BF16)
            for j in range(group // 2):
                chunk = hk * (group // 2) + j
                hq = hk * group + 2 * j
                qp = q_ref[:, chunk * LANES:(chunk + 1) * LANES].astype(F32)
                qa = jnp.where(qlane_lo, qp, 0.0)
                qb = jnp.where(qlane_lo, 0.0, qp)
                qs = jnp.concatenate([qa, qb], axis=0).astype(BF16)
                s = lax.dot_general(qs, k2, (((1,), (1,)), ((), ())),
                                    preferred_element_type=F32) * scale
                s = jnp.where(valid, s, NEG_INF)
                sink = jnp.where(row_first, sink_ref[hq], sink_ref[hq + 1])
                mx = jnp.maximum(jnp.max(s, axis=-1, keepdims=True), sink)
                p = jnp.exp(s - mx)
                denom = jnp.sum(p, axis=-1, keepdims=True) + jnp.exp(sink - mx)
                o2 = jnp.dot((p / denom).astype(BF16), v2, preferred_element_type=F32)
                o_ref[:, chunk * LANES:(chunk + 1) * LANES] = jnp.where(
                    qlane_lo, o2[:w], o2[w:]).astype(o_ref.dtype)


def swa_attention(z_s, sinks, batch, seq, q_heads, kv_heads):
    w = SWA_WINDOW
    nb = seq // w
    qw = q_heads * SWA_HEAD_DIM
    kw = kv_heads * SWA_HEAD_DIM
    assert qw % kw == 0 and kw % LANES == 0 and (q_heads // kv_heads) % 2 == 0
    kb = qw // kw
    cur = lambda blk: (lambda b, n, s: (b * nb + n, blk))
    prev = lambda blk: (lambda b, n, s: (b * nb + jnp.maximum(n - 1, 0), blk))
    return pl.pallas_call(
        functools.partial(_swa_body, group=q_heads // kv_heads),
        out_shape=jax.ShapeDtypeStruct((batch * seq, qw), BF16),
        grid_spec=pltpu.PrefetchScalarGridSpec(
            num_scalar_prefetch=1,
            grid=(batch, nb),
            in_specs=[pl.BlockSpec((w, qw), cur(0)),
                      pl.BlockSpec((w, kw), prev(kb)),
                      pl.BlockSpec((w, kw), cur(kb)),
                      pl.BlockSpec((w, kw), prev(kb + 1)),
                      pl.BlockSpec((w, kw), cur(kb + 1))],
            out_specs=pl.BlockSpec((w, qw), cur(0))),
        compiler_params=_cparams(("parallel", "arbitrary")),
        name="swa_attention",
    )(sinks, z_s, z_s, z_s, z_s, z_s)


def _gla_body(q_ref, k_ref, v_ref, g_ref, a_ref, wa_ref, ba_ref, gn_ref, o_ref,
              state_ref, b_ref, *, chunk, tokens, n_heads):
    c_sz = chunk
    dk, dv = GLA_DK, GLA_DV
    t = pl.program_id(2)

    @pl.when(t == 0)
    def _():
        state_ref[...] = jnp.zeros_like(state_ref)

    z = jnp.dot(a_ref[...], wa_ref[...], precision=HIGHEST, preferred_element_type=F32) + ba_ref[...]
    log_a = (jnp.minimum(z, 0.0) - jnp.log(1.0 + jnp.exp(-jnp.abs(z)))) * (1.0 / GLA_GATE_TAU)
    ri = lax.broadcasted_iota(jnp.int32, (tokens, tokens), 0)
    ci = lax.broadcasted_iota(jnp.int32, (tokens, tokens), 1)
    tri = ((ci <= ri) & (ci // c_sz == ri // c_sz)).astype(BF16)
    hi = log_a.astype(BF16)
    r1 = log_a - hi.astype(F32)
    mid = r1.astype(BF16)
    lo = (r1 - mid.astype(F32)).astype(BF16)
    b_ref[...] = (jnp.dot(tri, hi, preferred_element_type=F32)
                  + jnp.dot(tri, mid, preferred_element_type=F32)
                  + jnp.dot(tri, lo, preferred_element_type=F32))

    qscale = GLA_DK ** -0.5
    sub = GLA_SUB
    n_sub = c_sz // sub
    rowid = lax.broadcasted_iota(jnp.int32, (sub, 1), 0)

    def head_chunk(c0, hh):
        rows = pl.ds(c0, c_sz)
        kcols = slice(hh * dk, (hh + 1) * dk)
        vcols = slice(hh * dv, (hh + 1) * dv)
        b = b_ref[rows, kcols]
        q = q_ref[rows, kcols].astype(F32) * qscale
        k = k_ref[rows, kcols].astype(F32)
        v = v_ref[rows, vcols]
        vf = v.astype(F32)
        state_t = state_ref[hh]
        o_inter = lax.dot_general((q * jnp.exp(b)).astype(BF16), state_t.astype(BF16),
                                  (((1,), (1,)), ((), ())), preferred_element_type=F32)
        pieces = []
        for i in range(n_sub):
            r0 = i * sub
            b_i, q_i, k_i, v_i = b[r0:r0 + sub], q[r0:r0 + sub], k[r0:r0 + sub], vf[r0:r0 + sub]
            acc = jnp.zeros((sub, dv), F32)
            for j in range(sub):
                wgt = jnp.exp(b_i - b_i[j:j + 1, :])
                col = jnp.sum(q_i * k_i[j:j + 1, :] * wgt, axis=-1, keepdims=True)
                col = jnp.where(rowid >= j, col, 0.0)
                acc = acc + col * v_i[j:j + 1, :]
            if i > 0:
                ref_b = b[r0 - 1:r0, :]
                qh = (q_i * jnp.exp(b_i - ref_b)).astype(BF16)
                kh = (k[:r0] * jnp.exp(ref_b - b[:r0])).astype(BF16)
                att = lax.dot_general(qh, kh, (((1,), (1,)), ((), ())), preferred_element_type=F32)
                acc = acc + jnp.dot(att.astype(BF16), v[:r0], preferred_element_type=F32)
            pieces.append(acc)
        o = o_inter + jnp.concatenate(pieces, axis=0)
        b_last = b[c_sz - 1:c_sz, :]
        k_dec = (k * jnp.exp(b_last - b)).astype(BF16)
        upd = lax.dot_general(v, k_dec, (((0,), (0,)), ((), ())), preferred_element_type=F32)
        state_ref[hh] = state_t * jnp.exp(b_last) + upd
        y = o * lax.rsqrt(jnp.mean(o * o, axis=-1, keepdims=True) + RMS_EPS) * gn_ref[...]
        g = g_ref[rows, vcols].astype(F32)
        y = y * (g / (1.0 + jnp.exp(-g)))
        o_ref[rows, vcols] = y.astype(o_ref.dtype)

    def chunk_step(c, carry):
        c0 = pl.multiple_of(c * c_sz, c_sz)
        for hh in range(n_heads):
            head_chunk(c0, hh)
        return carry

    lax.fori_loop(0, tokens // c_sz, chunk_step, 0)


def gla_attention(z_qkv, z_gg, z_a, a_blk, w_a2p, b_a, out_norm, batch, seq, heads, chunk=64):
    dk, dv = GLA_DK, GLA_DV
    tokens = _pick(seq, 256, chunk)
    nt = seq // tokens
    nh = 2
    assert heads % nh == 0
    wk, wv = nh * dk, nh * dv
    k_off = heads * dk // wk
    v_off = 2 * heads * dk // wv
    row = lambda blk_fn: (lambda b, h, t: (b * nt + t, blk_fn(h)))
    return pl.pallas_call(
        functools.partial(_gla_body, chunk=chunk, tokens=tokens, n_heads=nh),
        out_shape=jax.ShapeDtypeStruct((batch * seq, heads * dv), BF16),
        grid=(batch, heads // nh, nt),
        in_specs=[pl.BlockSpec((tokens, wk), row(lambda h: h)),
                  pl.BlockSpec((tokens, wk), row(lambda h: k_off + h)),
                  pl.BlockSpec((tokens, wv), row(lambda h: v_off + h)),
                  pl.BlockSpec((tokens, wv), row(lambda h: h)),
                  pl.BlockSpec((tokens, LANES), row(lambda h: a_blk)),
                  pl.BlockSpec((LANES, wk), lambda b, h, t: (0, h)),
                  pl.BlockSpec((1, wk), lambda b, h, t: (0, h)),
                  pl.BlockSpec((1, dv), lambda b, h, t: (0, 0))],
        out_specs=pl.BlockSpec((tokens, wv), row(lambda h: h)),
        scratch_shapes=[pltpu.VMEM((nh, dv, dk), F32), pltpu.VMEM((tokens, wk), F32)],
        compiler_params=_cparams(("parallel", "parallel", "arbitrary")),
        name="gla_attention",
    )(z_qkv, z_qkv, z_qkv, z_gg, z_a, w_a2p, b_a.reshape(1, -1), out_norm.reshape(1, dv))


def _merge_body(h_ref, y0_ref, y1_ref, y2_ref, wg0_ref, wg1_ref, wg2_ref,
                wb0_ref, wb1_ref, wb2_ref, bias_ref, o_ref):
    h = h_ref[...]
    acc = None
    branches = ((y0_ref, wg0_ref, wb0_ref), (y1_ref, wg1_ref, wb1_ref), (y2_ref, wg2_ref, wb2_ref))
    for b, (y_ref, wg_ref, wb_ref) in enumerate(branches):
        logit = jnp.dot(h, wg_ref[...], preferred_element_type=F32) + bias_ref[b:b + 1, :]
        gate = 1.0 / (1.0 + jnp.exp(-logit))
        term = gate * jnp.dot(y_ref[...], wb_ref[...], preferred_element_type=F32)
        acc = term if acc is None else acc + term
    o_ref[...] = acc.astype(o_ref.dtype)


def merge_branches(h, ys, w_gates, w_brs, bias3):
    m, d = h.shape
    bw = ys[0].shape[1]
    tm = _pick(m, 512, 16)
    tn = _pick(d, 256)
    nj = d // tn
    gate_spec = lambda b: pl.BlockSpec((d, tn), lambda j, i: (0, b * nj + j))
    return pl.pallas_call(
        _merge_body,
        out_shape=jax.ShapeDtypeStruct((m, d), BF16),
        grid=(nj, m // tm),
        in_specs=[pl.BlockSpec((tm, d), lambda j, i: (i, 0))]
        + [pl.BlockSpec((tm, bw), lambda j, i: (i, 0))] * N_BRANCHES
        + [gate_spec(b) for b in range(N_BRANCHES)]
        + [pl.BlockSpec((bw, tn), lambda j, i: (0, j))] * N_BRANCHES
        + [pl.BlockSpec((N_BRANCHES, tn), lambda j, i: (0, j))],
        out_specs=pl.BlockSpec((tm, tn), lambda j, i: (i, j)),
        compiler_params=_cparams(("parallel", "parallel")),
        name="merge_branches",
    )(h, *ys, w_gates, w_gates, w_gates, *w_brs, bias3)


def _pack_bf16_pairs(x):
    c = x.shape[1] // 2
    bits = lax.bitcast_convert_type(x.astype(jnp.bfloat16).astype(F32), jnp.uint32)
    return (bits[:, c:] & jnp.uint32(0xFFFF0000)) | (bits[:, :c] >> 16)


def _unpack_bf16_pairs(p):
    lo = lax.bitcast_convert_type(p << 16, F32).astype(BF16)
    hi = lax.bitcast_convert_type(p & jnp.uint32(0xFFFF0000), F32).astype(BF16)
    return lo, hi


def _router_body(x_ref, g_ref, w_ref, b_ref, h_ref, e_ref, wt_ref, rk_ref, cnt_ref, carry_ref,
                 *, n_exp, n_grp, per_grp):
    i = pl.program_id(0)

    @pl.when(i == 0)
    def _():
        carry_ref[...] = jnp.zeros_like(carry_ref)

    x = x_ref[...]
    tm = x.shape[0]
    hn = x * lax.rsqrt(jnp.mean(x * x, axis=-1, keepdims=True) + RMS_EPS) * g_ref[...]
    h_ref[...] = _pack_bf16_pairs(hn)
    logits = jnp.dot(hn, w_ref[...], precision=HIGHEST, preferred_element_type=F32) + b_ref[...]
    lane = lax.broadcasted_iota(jnp.int32, (tm, LANES), 1)
    big = jnp.int32(LANES)

    def top1(vals):
        mx = jnp.max(vals, axis=-1, keepdims=True)
        idx = jnp.min(jnp.where(vals == mx, lane, big), axis=-1, keepdims=True)
        return mx, idx

    gl = jnp.where((lane >= n_exp) & (lane < n_exp + n_grp), logits, NEG_INF)
    gmax, gidx = top1(gl)
    g_top_p = 1.0 / jnp.sum(jnp.exp(gl - gmax), axis=-1, keepdims=True)
    grp = gidx - n_exp
    el = jnp.where((lane >= grp * per_grp) & (lane < (grp + 1) * per_grp), logits, NEG_INF)
    m1, i1 = top1(el)
    m2, i2 = top1(jnp.where(lane == i1, NEG_INF, el))
    e21 = jnp.exp(m2 - m1)
    w1 = g_top_p / (1.0 + e21)
    w2 = w1 * e21
    e_ref[...] = jnp.where(lane == 0, i1, jnp.where(lane == 1, i2, 0))[:, :8]
    wt_ref[...] = jnp.where(lane == 0, w1, jnp.where(lane == 1, w2, 0.0))[:, :8]

    oh1 = (lane == i1).astype(F32)
    oh2 = (lane == i2).astype(F32)
    ri = lax.broadcasted_iota(jnp.int32, (tm, tm), 0)
    ci = lax.broadcasted_iota(jnp.int32, (tm, tm), 1)
    strict = (ci < ri).astype(BF16)
    cum1 = jnp.dot(strict, oh1.astype(BF16), preferred_element_type=F32)
    cum2 = jnp.dot(strict, oh2.astype(BF16), preferred_element_type=F32)
    tot1 = jnp.sum(oh1, axis=0, keepdims=True)
    tot2 = jnp.sum(oh2, axis=0, keepdims=True)
    carry = carry_ref[...]
    r1 = jnp.sum(oh1 * (cum1 + carry), axis=-1, keepdims=True)
    r2 = jnp.sum(oh2 * (cum2 + carry + tot1), axis=-1, keepdims=True)
    rk_ref[...] = jnp.where(lane == 0, r1, jnp.where(lane == 1, r2, 0.0))[:, :8].astype(jnp.int32)
    carry = carry + tot1 + tot2
    carry_ref[...] = carry
    cnt_ref[...] = jnp.broadcast_to(carry, cnt_ref.shape).astype(jnp.int32)


def moe_router(x, gain, w_r, b_r, n_exp, n_grp):
    m, d = x.shape
    tm = _pick(m, 256, 8)
    row = lambda i: (i, 0)
    const = lambda i: (0, 0)
    return pl.pallas_call(
        functools.partial(_router_body, n_exp=n_exp, n_grp=n_grp, per_grp=n_exp // n_grp),
        out_shape=(jax.ShapeDtypeStruct((m, d // 2), jnp.uint32),
                   jax.ShapeDtypeStruct((m, 8), jnp.int32),
                   jax.ShapeDtypeStruct((m, 8), F32),
                   jax.ShapeDtypeStruct((m, 8), jnp.int32),
                   jax.ShapeDtypeStruct((8, LANES), jnp.int32)),
        grid=(m // tm,),
        in_specs=[pl.BlockSpec((tm, d), row), pl.BlockSpec((1, d), const),
                  pl.BlockSpec((d, LANES), const), pl.BlockSpec((1, LANES), const)],
        out_specs=(pl.BlockSpec((tm, d // 2), row), pl.BlockSpec((tm, 8), row),
                   pl.BlockSpec((tm, 8), row), pl.BlockSpec((tm, 8), row),
                   pl.BlockSpec((8, LANES), const)),
        scratch_shapes=[pltpu.VMEM((1, LANES), F32)],
        compiler_params=_cparams(("arbitrary",)),
        name="moe_router",
    )(x, gain.reshape(1, d), w_r, b_r)


def _dispatch_body(p1_ref, p2_ref, h_ref, xs_in_ref, xs_ref, sem, *, tokens):
    del xs_in_ref
    base = pl.program_id(0) * tokens

    def copies(t):
        src = h_ref.at[pl.ds(t, 1), :]
        return (pltpu.make_async_copy(src, xs_ref.at[pl.ds(p1_ref[base + t], 1), :], sem),
                pltpu.make_async_copy(src, xs_ref.at[pl.ds(p2_ref[base + t], 1), :], sem))

    def start(t, c):
        a, b = copies(t)
        a.start()
        b.start()
        return c

    def wait(t, c):
        a, b = copies(t)
        a.wait()
        b.wait()
        return c

    lax.fori_loop(0, tokens, start, 0, unroll=8)
    lax.fori_loop(0, tokens, wait, 0, unroll=8)


def moe_dispatch(h, pos1, pos2, rows):
    m, d = h.shape
    tokens = _pick(m, 256, 8)
    xs0 = jnp.zeros((rows, d), h.dtype)
    return pl.pallas_call(
        functools.partial(_dispatch_body, tokens=tokens),
        out_shape=jax.ShapeDtypeStruct((rows, d), h.dtype),
        grid_spec=pltpu.PrefetchScalarGridSpec(
            num_scalar_prefetch=2,
            grid=(m // tokens,),
            in_specs=[pl.BlockSpec((tokens, d), lambda i, p1, p2: (i, 0)),
                      pl.BlockSpec(memory_space=pl.ANY)],
            out_specs=pl.BlockSpec(memory_space=pl.ANY),
            scratch_shapes=[pltpu.SemaphoreType.DMA(())]),
        input_output_aliases={3: 0},
        compiler_params=_cparams(("arbitrary",)),
        name="moe_dispatch",
    )(pos1, pos2, h, xs0)


def _experts_up_body(te_ref, ts_ref, tv_ref, x_ref, wg_ref, wu_ref, h_ref):
    del te_ref, ts_ref
    i = pl.program_id(1)
    half = x_ref.shape[1]

    @pl.when(tv_ref[i] > 0)
    def _():
        lo, hi = _unpack_bf16_pairs(x_ref[...])

        def proj(w_ref):
            return (jnp.dot(lo, w_ref[:half, :].astype(BF16), preferred_element_type=F32)
                    + jnp.dot(hi, w_ref[half:, :].astype(BF16), preferred_element_type=F32))

        a = proj(wg_ref)
        u = proj(wu_ref)
        h_ref[...] = ((a / (1.0 + jnp.exp(-a))) * u).astype(h_ref.dtype)

    @pl.when(tv_ref[i] == 0)
    def _():
        h_ref[...] = jnp.zeros_like(h_ref)


def _experts_down_body(te_ref, ts_ref, tv_ref, h_ref, wd_ref, o_ref):
    del te_ref, ts_ref
    i = pl.program_id(1)

    @pl.when(tv_ref[i] > 0)
    def _():
        o_ref[...] = jnp.dot(h_ref[...], wd_ref[...].astype(BF16), preferred_element_type=F32)

    @pl.when(tv_ref[i] == 0)
    def _():
        o_ref[...] = jnp.zeros_like(o_ref)


def moe_experts(xs, tile_expert, tile_src, tile_valid, w_gate, w_up, w_down, layer, tm):
    rows = xs.shape[0]
    d, ff = w_gate.shape[2], w_gate.shape[3]
    n_tiles = rows // tm
    tf = _pick(ff, 384)
    td = _pick(d, 1024)
    prefetch = (tile_expert, tile_src, tile_valid)
    w_map = lambda j, i, te, ts, tv: (layer, te[i], 0, j)
    hid = pl.pallas_call(
        _experts_up_body,
        out_shape=jax.ShapeDtypeStruct((rows, ff), BF16),
        grid_spec=pltpu.PrefetchScalarGridSpec(
            num_scalar_prefetch=3,
            grid=(ff // tf, n_tiles),
            in_specs=[pl.BlockSpec((tm, d // 2), lambda j, i, te, ts, tv: (ts[i], 0)),
                      pl.BlockSpec((None, None, d, tf), w_map),
                      pl.BlockSpec((None, None, d, tf), w_map)],
            out_specs=pl.BlockSpec((tm, tf), lambda j, i, te, ts, tv: (i, j))),
        compiler_params=_cparams(("arbitrary", "arbitrary")),
        name="moe_experts_up",
    )(*prefetch, xs, w_gate, w_up)
    return pl.pallas_call(
        _experts_down_body,
        out_shape=jax.ShapeDtypeStruct((rows, d), F32),
        grid_spec=pltpu.PrefetchScalarGridSpec(
            num_scalar_prefetch=3,
            grid=(d // td, n_tiles),
            in_specs=[pl.BlockSpec((tm, ff), lambda j, i, te, ts, tv: (ts[i], 0)),
                      pl.BlockSpec((None, None, ff, td), w_map)],
            out_specs=pl.BlockSpec((tm, td), lambda j, i, te, ts, tv: (i, j))),
        compiler_params=_cparams(("arbitrary", "arbitrary")),
        name="moe_experts_down",
    )(*prefetch, hid, w_down)


def _combine_body(p1_ref, p2_ref, x_ref, wt_ref, g_ref, ys_ref, *rest, tokens, emit_x):
    if emit_x:
        xo_ref, no_ref, buf_ref, sem = rest
    else:
        no_ref, buf_ref, sem = rest
    base = pl.program_id(0) * tokens

    def copies(t):
        return (pltpu.make_async_copy(ys_ref.at[pl.ds(p1_ref[base + t], 1), :],
                                      buf_ref.at[0, pl.ds(t, 1), :], sem),
                pltpu.make_async_copy(ys_ref.at[pl.ds(p2_ref[base + t], 1), :],
                                      buf_ref.at[1, pl.ds(t, 1), :], sem))

    def start(t, c):
        a, b = copies(t)
        a.start()
        b.start()
        return c

    def wait(t, c):
        a, b = copies(t)
        a.wait()
        b.wait()
        return c

    lax.fori_loop(0, tokens, start, 0, unroll=8)
    lax.fori_loop(0, tokens, wait, 0, unroll=8)
    wt = wt_ref[...]
    x = x_ref[...] + wt[:, 0:1] * buf_ref[0] + wt[:, 1:2] * buf_ref[1]
    if emit_x:
        xo_ref[...] = x
    y = x * lax.rsqrt(jnp.mean(x * x, axis=-1, keepdims=True) + RMS_EPS)
    no_ref[...] = (y * g_ref[...]).astype(no_ref.dtype)


def moe_combine(x, ys, pos1, pos2, wts, gain, norm_dtype, emit_x):
    m, d = x.shape
    tokens = _pick(m, 256, 8)
    row = lambda i, p1, p2: (i, 0)
    out_shape = [jax.ShapeDtypeStruct((m, d), norm_dtype)]
    out_specs = [pl.BlockSpec((tokens, d), row)]
    if emit_x:
        out_shape.insert(0, jax.ShapeDtypeStruct((m, d), F32))
        out_specs.insert(0, pl.BlockSpec((tokens, d), row))
    return pl.pallas_call(
        functools.partial(_combine_body, tokens=tokens, emit_x=emit_x),
        out_shape=tuple(out_shape),
        grid_spec=pltpu.PrefetchScalarGridSpec(
            num_scalar_prefetch=2,
            grid=(m // tokens,),
            in_specs=[pl.BlockSpec((tokens, d), row),
                      pl.BlockSpec((tokens, 8), row),
                      pl.BlockSpec((1, d), lambda i, p1, p2: (0, 0)),
                      pl.BlockSpec(memory_space=pl.ANY)],
            out_specs=tuple(out_specs),
            scratch_shapes=[pltpu.VMEM((2, tokens, d), F32), pltpu.SemaphoreType.DMA(())]),
        compiler_params=_cparams(("arbitrary",)),
        name="moe_combine",
    )(pos1, pos2, x, wts, gain.reshape(1, d), ys)


def _rope_partner(w_rot):
    half = MLA_ROPE_DIM // 2
    return jnp.concatenate([-w_rot[..., half:], w_rot[..., :half]], axis=-1)


def _repack_body(a_ref, b_ref, o_ref, *, shift):
    tn = o_ref.shape[1]
    win = jnp.concatenate([a_ref[...], b_ref[...]], axis=1)
    o_ref[...] = pltpu.roll(win, tn + LANES - shift, axis=1)[:, :tn].astype(o_ref.dtype)


def repack_columns(w3, layer, col_start, width):
    rows = w3.shape[1]
    shift = col_start % LANES
    base = col_start - shift
    assert shift != 0 and width % LANES == 0
    tn = _pick(math.gcd(base, width) if base else width, 1024)
    tr = _pick(rows, 1024, 8)
    return pl.pallas_call(
        functools.partial(_repack_body, shift=shift),
        out_shape=jax.ShapeDtypeStruct((rows, width), BF16),
        grid=(rows // tr, width // tn),
        in_specs=[pl.BlockSpec((None, tr, tn), lambda i, j: (layer, i, base // tn + j)),
                  pl.BlockSpec((None, tr, LANES),
                               lambda i, j: (layer, i, (base + (j + 1) * tn) // LANES))],
        out_specs=pl.BlockSpec((tr, tn), lambda i, j: (i, j)),
        compiler_params=_cparams(("parallel", "parallel")),
        name="repack_columns",
    )(w3, w3)


def _layer_weights(w_in3, layer, w_uq, heads_mla):
    w_in = w_in3[layer]
    d = w_in.shape[0]
    gla_qk = GLA_HEADS * GLA_DK
    gla_w = GLA_HEADS * GLA_DV
    swa_q = SWA_Q_HEADS * SWA_HEAD_DIM
    swa_kv = SWA_KV_HEADS * SWA_HEAD_DIM
    sizes = (MLA_Q_LORA, MLA_KV_LORA, MLA_ROPE_DIM, gla_qk, gla_qk, gla_w, GLA_GATE_RANK, gla_w,
             swa_q, swa_kv, swa_kv, N_BRANCHES * d)
    offs = [int(o) for o in np.concatenate([[0], np.cumsum(sizes)])]
    seg = lambda i: w_in[:, offs[i]:offs[i + 1]]
    zpad = lambda n: jnp.zeros((d, n), w_in.dtype)
    kr = seg(2)
    w_a = jnp.concatenate([w_in[:, :offs[2]], kr, zpad(LANES - MLA_ROPE_DIM),
                           _rope_partner(kr), zpad(LANES - MLA_ROPE_DIM),
                           seg(6), zpad(LANES - GLA_GATE_RANK)], axis=1).astype(BF16)
    w_qkv = repack_columns(w_in3, layer, offs[3], offs[6] - offs[3])
    w_gg = repack_columns(w_in3, layer, offs[7], offs[8] - offs[7])
    w_s = repack_columns(w_in3, layer, offs[8], offs[11] - offs[8])
    w_gates = repack_columns(w_in3, layer, offs[11], offs[12] - offs[11])
    qk = MLA_NOPE_DIM + MLA_ROPE_DIM
    wq = w_uq.reshape(-1, heads_mla, qk)
    pad = jnp.zeros((wq.shape[0], heads_mla, MLA_HEAD_PAD - qk), w_uq.dtype)
    w_q1 = jnp.concatenate([wq, pad], axis=-1).reshape(wq.shape[0], -1).astype(BF16)
    rot = wq[..., MLA_NOPE_DIM:]
    pad2 = jnp.zeros((wq.shape[0], heads_mla, LANES - MLA_ROPE_DIM), w_uq.dtype)
    w_q2 = jnp.concatenate([_rope_partner(rot), pad2], axis=-1).reshape(wq.shape[0], -1).astype(BF16)
    return w_a, w_qkv, w_gg, w_s, w_gates, w_q1.T, w_q2.T


def _moe_layer(x, gain, w_rg, b_rg, w_re, b_re, w_gate, w_up, w_down, layer, next_gain, norm_dtype,
               emit_x):
    m, d = x.shape
    n_exp, n_grp = w_re.shape[1], w_rg.shape[1]
    w_r = jnp.concatenate([w_re, w_rg, jnp.zeros((d, LANES - n_exp - n_grp), F32)], axis=1)
    b_r = jnp.concatenate([b_re, b_rg, jnp.zeros((LANES - n_exp - n_grp,), F32)]).reshape(1, LANES)
    hn, e_idx, wts, rank, counts = moe_router(x, gain, w_r, b_r, n_exp, n_grp)
    tm = 512 if m >= 2048 else 64
    cnt = counts[0, :n_exp]
    padded = ((cnt + tm - 1) // tm) * tm
    ends = jnp.cumsum(padded)
    offs = ends - padded
    n_tiles = (TOP_K_IN_GROUP * m) // tm + n_exp
    tile_start = jnp.arange(n_tiles, dtype=jnp.int32) * tm
    tile_valid = (tile_start < ends[-1]).astype(jnp.int32)
    last_tile = jnp.maximum(ends[-1] // tm - 1, 0).astype(jnp.int32)
    tile_src = jnp.minimum(jnp.arange(n_tiles, dtype=jnp.int32), last_tile)
    tile_expert = jnp.sum((ends[None, :] <= (tile_src * tm)[:, None]).astype(jnp.int32), axis=1)
    tile_expert = jnp.minimum(tile_expert, n_exp - 1)
    pos = offs[e_idx[:, :2]].astype(jnp.int32) + rank[:, :2]
    pos1, pos2 = pos[:, 0], pos[:, 1]
    xs = moe_dispatch(hn, pos1, pos2, n_tiles * tm)
    ys = moe_experts(xs, tile_expert, tile_src, tile_valid, w_gate, w_up, w_down, layer, tm)
    return moe_combine(x, ys, pos1, pos2, wts, next_gain, norm_dtype, emit_x)


def kernel(x, positions, attn_norm, w_in, gate_bias, mla_q_norm, mla_w_uq, mla_kv_norm, mla_w_ukv,
           gla_w_a2, gla_b_a, gla_out_norm, swa_sinks, w_br_mla, w_br_gla, w_br_swa, w_out,
           ffn_norm, w_router_group, b_router_group, w_router_expert, b_router_expert,
           w_gate, w_up, w_down, final_norm):
    batch, seq, d = x.shape
    depth = w_in.shape[0]
    m = batch * seq
    heads_mla = mla_w_uq.shape[2] // (MLA_NOPE_DIM + MLA_ROPE_DIM)
    xf = x.reshape(m, d)
    pos = positions.reshape(m, 1).astype(jnp.int32)
    half = MLA_ROPE_DIM // 2
    inv_freq = ROPE_THETA ** (-jnp.arange(0, MLA_ROPE_DIM, 2, dtype=F32) / MLA_ROPE_DIM)
    freq = jnp.concatenate([inv_freq, inv_freq, jnp.zeros((LANES - 2 * half,), F32)]).reshape(1, LANES)
    a_blk = (MLA_Q_LORA + MLA_KV_LORA) // LANES + 2

    h = rmsnorm(xf, attn_norm[0], BF16)
    out = None
    for l in range(depth):
        w_a, w_qkv, w_gg, w_s, w_gates, w_q1t, w_q2t = _layer_weights(w_in, l, mla_w_uq[l], heads_mla)
        z_a = matmul(h, w_a, F32, name="in_proj_mla")
        z_qkv = matmul(h, w_qkv, BF16, name="in_proj_gla_qkv")
        z_gg = matmul(h, w_gg, BF16, name="in_proj_gla_gate")
        z_s = matmul(h, w_s, BF16, name="in_proj_swa")
        w_kv = mla_w_ukv[l].reshape(-1, heads_mla, MLA_NOPE_DIM + MLA_V_DIM)
        w_kn = w_kv[..., :MLA_NOPE_DIM].reshape(w_kv.shape[0], -1).astype(BF16)
        w_vt = w_kv[..., MLA_NOPE_DIM:].reshape(w_kv.shape[0], -1).T.astype(BF16)
        qt, k, vt = mla_prep(z_a, pos, mla_q_norm[l], mla_kv_norm[l], freq, w_q1t, w_q2t,
                             w_kn, w_vt, heads_mla)
        y_mla = mla_attention(qt, k, vt, batch, seq, heads_mla)
        w_a2p = jnp.concatenate([gla_w_a2[l], jnp.zeros((LANES - GLA_GATE_RANK, gla_w_a2.shape[2]), F32)], axis=0)
        y_gla = gla_attention(z_qkv, z_gg, z_a, a_blk, w_a2p, gla_b_a[l], gla_out_norm[l], batch, seq, GLA_HEADS)
        y_swa = swa_attention(z_s, swa_sinks[l], batch, seq, SWA_Q_HEADS, SWA_KV_HEADS)
        w_brs = (w_br_mla[l].astype(BF16), w_br_gla[l].astype(BF16), w_br_swa[l].astype(BF16))
        merged = merge_branches(h, (y_mla, y_gla, y_swa), w_gates, w_brs, gate_bias[l])
        xf = matmul(merged, w_out[l].astype(BF16), F32, residual=xf, name="out_proj")
        last = l == depth - 1
        res = _moe_layer(xf, ffn_norm[l], w_router_group[l], b_router_group[l],
                         w_router_expert[l], b_router_expert[l], w_gate, w_up, w_down, l,
                         final_norm if last else attn_norm[l + 1],
                         F32 if last else BF16, emit_x=not last)
        if last:
            out = res[0]
        else:
            xf, h = res
    return out.reshape(batch, seq, d)
```

```python
import functools
import math

import numpy as np
import jax
import jax.numpy as jnp
from jax import lax
from jax.experimental import pallas as pl
from jax.experimental.pallas import tpu as pltpu

F32 = jnp.float32
BF16 = jnp.bfloat16
HIGHEST = lax.Precision.HIGHEST

RMS_EPS = 1e-6
NEG_INF = -1e30
MLA_HEADS = 16
MLA_Q_LORA = 1024
MLA_KV_LORA = 512
MLA_NOPE_DIM = 128
MLA_ROPE_DIM = 64
MLA_V_DIM = 128
ROPE_THETA = 10000.0
GLA_HEADS = 8
GLA_DK = 128
GLA_DV = 256
GLA_GATE_RANK = 16
GLA_GATE_TAU = 16.0
SWA_Q_HEADS = 32
SWA_KV_HEADS = 4
SWA_HEAD_DIM = 64
SWA_WINDOW = 128
N_BRANCHES = 3
N_GROUPS = 4
EXPERTS_PER_GROUP = 8
TOP_K_IN_GROUP = 2

LANES = 128
MLA_HEAD_PAD = 256
GLA_SUB = 16
VMEM_LIMIT = 60 * 1024 * 1024


def _cparams(sem, vmem=VMEM_LIMIT):
    return pltpu.CompilerParams(dimension_semantics=sem, vmem_limit_bytes=vmem)


def _pick(n, cap, mult=LANES):
    best = None
    for t in range(mult, min(n, cap) + 1, mult):
        if n % t == 0:
            best = t
    assert best is not None, (n, cap, mult)
    return best


def _rmsnorm_body(x_ref, g_ref, o_ref):
    x = x_ref[...]
    y = x * lax.rsqrt(jnp.mean(x * x, axis=-1, keepdims=True) + RMS_EPS)
    o_ref[...] = (y * g_ref[...]).astype(o_ref.dtype)


def rmsnorm(x, gain, out_dtype):
    m, d = x.shape
    tm = _pick(m, 512, 8)
    return pl.pallas_call(
        _rmsnorm_body,
        out_shape=jax.ShapeDtypeStruct((m, d), out_dtype),
        grid=(m // tm,),
        in_specs=[pl.BlockSpec((tm, d), lambda i: (i, 0)),
                  pl.BlockSpec((1, d), lambda i: (0, 0))],
        out_specs=pl.BlockSpec((tm, d), lambda i: (i, 0)),
        compiler_params=_cparams(("parallel",)),
        name="rmsnorm",
    )(x, gain.reshape(1, d))


def _matmul_body(a_ref, w_ref, o_ref):
    w = w_ref[...].astype(BF16)
    o_ref[...] = jnp.dot(a_ref[...], w, preferred_element_type=F32).astype(o_ref.dtype)


def _matmul_res_body(a_ref, w_ref, r_ref, o_ref):
    acc = jnp.dot(a_ref[...], w_ref[...].astype(BF16), preferred_element_type=F32)
    o_ref[...] = (r_ref[...] + acc).astype(o_ref.dtype)


def matmul(a, w, out_dtype, residual=None, tm_cap=512, tn_cap=1024, name="matmul"):
    m, k = a.shape
    n = w.shape[1]
    tm = _pick(m, tm_cap, 16)
    tn = _pick(n, tn_cap)
    in_specs = [pl.BlockSpec((tm, k), lambda j, i: (i, 0)),
                pl.BlockSpec((k, tn), lambda j, i: (0, j))]
    args = [a, w]
    body = _matmul_body
    if residual is not None:
        in_specs.append(pl.BlockSpec((tm, tn), lambda j, i: (i, j)))
        args.append(residual)
        body = _matmul_res_body
    return pl.pallas_call(
        body,
        out_shape=jax.ShapeDtypeStruct((m, n), out_dtype),
        grid=(n // tn, m // tm),
        in_specs=in_specs,
        out_specs=pl.BlockSpec((tm, tn), lambda j, i: (i, j)),
        compiler_params=_cparams(("parallel", "parallel")),
        name=name,
    )(*args)


def _rope_tables(pos_ref, freq_ref):
    ang = pos_ref[...].astype(F32) * freq_ref[...]
    return jnp.cos(ang), jnp.sin(ang)


def _mla_q_body(c_ref, pos_ref, g_ref, freq_ref, w1t_ref, w2t_ref, qt_ref, *, heads, scale):
    c = c_ref[...]
    cn = (c * lax.rsqrt(jnp.mean(c * c, axis=-1, keepdims=True) + RMS_EPS) * g_ref[...]).astype(BF16)
    nt = (((1,), (1,)), ((), ()))
    q = lax.dot_general(w1t_ref[...], cn, nt, preferred_element_type=F32)
    qp = lax.dot_general(w2t_ref[...], cn, nt, preferred_element_type=F32)
    ang = freq_ref[...] * pos_ref[...].astype(F32)
    cs, sn = jnp.cos(ang), jnp.sin(ang)
    for h in range(heads):
        a = h * MLA_HEAD_PAD
        qt_ref[a:a + LANES, :] = (q[a:a + LANES] * scale).astype(BF16)
        rot = q[a + LANES:a + 2 * LANES] * cs + qp[h * LANES:(h + 1) * LANES] * sn
        qt_ref[a + LANES:a + 2 * LANES, :] = (rot * scale).astype(BF16)


def _mla_kv_body(c_ref, kr_ref, krp_ref, pos_ref, g_ref, freq_ref, wk_ref, wvt_ref, k_ref, vt_ref,
                 *, heads):
    c = c_ref[...]
    cn = (c * lax.rsqrt(jnp.mean(c * c, axis=-1, keepdims=True) + RMS_EPS) * g_ref[...]).astype(BF16)
    kn = jnp.dot(cn, wk_ref[...], preferred_element_type=F32)
    vt_ref[...] = lax.dot_general(wvt_ref[...], cn, (((1,), (1,)), ((), ())),
                                  preferred_element_type=F32).astype(BF16)
    cs, sn = _rope_tables(pos_ref, freq_ref)
    krot = (kr_ref[...] * cs + krp_ref[...] * sn).astype(BF16)
    for h in range(heads):
        a = h * MLA_HEAD_PAD
        k_ref[:, a:a + LANES] = kn[:, h * LANES:(h + 1) * LANES].astype(BF16)
        k_ref[:, a + LANES:a + 2 * LANES] = krot


def mla_prep(z_a, z_r, pos, q_norm, kv_norm, freq, w_q1t, w_q2t, w_kn, w_vt, heads):
    m = z_a.shape[0]
    ql, kvl = w_q1t.shape[1], w_kn.shape[0]
    tm = _pick(m, 256, LANES)
    scale = (MLA_NOPE_DIM + MLA_ROPE_DIM) ** -0.5
    row = lambda i: (i, 0)
    const = lambda i: (0, 0)
    qt = pl.pallas_call(
        functools.partial(_mla_q_body, heads=heads, scale=scale),
        out_shape=jax.ShapeDtypeStruct((heads * MLA_HEAD_PAD, m), BF16),
        grid=(m // tm,),
        in_specs=[pl.BlockSpec((tm, ql), row),
                  pl.BlockSpec((1, tm), lambda i: (0, i)),
                  pl.BlockSpec((1, ql), const),
                  pl.BlockSpec((LANES, 1), const),
                  pl.BlockSpec(w_q1t.shape, const),
                  pl.BlockSpec(w_q2t.shape, const)],
        out_specs=pl.BlockSpec((heads * MLA_HEAD_PAD, tm), lambda i: (0, i)),
        compiler_params=_cparams(("parallel",)),
        name="mla_q_prep",
    )(z_a, pos.reshape(1, m), q_norm.reshape(1, ql), freq.reshape(LANES, 1), w_q1t, w_q2t)
    k, vt = pl.pallas_call(
        functools.partial(_mla_kv_body, heads=heads),
        out_shape=(jax.ShapeDtypeStruct((m, heads * MLA_HEAD_PAD), BF16),
                   jax.ShapeDtypeStruct((heads * MLA_V_DIM, m), BF16)),
        grid=(m // tm,),
        in_specs=[pl.BlockSpec((tm, kvl), lambda i: (i, ql // kvl)),
                  pl.BlockSpec((tm, LANES), lambda i: (i, 0)),
                  pl.BlockSpec((tm, LANES), lambda i: (i, 1)),
                  pl.BlockSpec((tm, 1), row),
                  pl.BlockSpec((1, kvl), const),
                  pl.BlockSpec((1, LANES), const),
                  pl.BlockSpec(w_kn.shape, const),
                  pl.BlockSpec(w_vt.shape, const)],
        out_specs=(pl.BlockSpec((tm, heads * MLA_HEAD_PAD), row),
                   pl.BlockSpec((heads * MLA_V_DIM, tm), lambda i: (0, i))),
        compiler_params=_cparams(("parallel",)),
        name="mla_kv_prep",
    )(z_a, z_r, z_r, pos, kv_norm.reshape(1, kvl), freq, w_kn, w_vt)
    return qt, k, vt


def _mla_attn_body(qt_ref, k_ref, vt_ref, o_ref, m_ref, l_ref, acc_ref, *, tq, sub):
    del sub
    qi = pl.program_id(2)
    m_ref[...] = jnp.full_like(m_ref, NEG_INF)
    l_ref[...] = jnp.zeros_like(l_ref)
    acc_ref[...] = jnp.zeros_like(acc_ref)

    def scores(j):
        k0 = pl.multiple_of(j * tq, tq)
        return jnp.dot(k_ref[pl.ds(k0, tq), :], qt_ref[...], preferred_element_type=F32)

    def accumulate(j, st):
        k0 = pl.multiple_of(j * tq, tq)
        m_old = m_ref[...]
        m_new = jnp.maximum(m_old, jnp.max(st, axis=0, keepdims=True))
        alpha = jnp.exp(m_old - m_new)
        p = jnp.exp(st - m_new)
        l_ref[...] = alpha * l_ref[...] + jnp.sum(p, axis=0, keepdims=True)
        acc_ref[...] = alpha * acc_ref[...] + jnp.dot(vt_ref[:, pl.ds(k0, tq)], p.astype(BF16),
                                                      preferred_element_type=F32)
        m_ref[...] = m_new

    def full_tile(j, st):
        st_next = scores(j + 1)
        accumulate(j, st)
        return st_next

    st = lax.fori_loop(0, qi, full_tile, scores(0))
    tri = (lax.broadcasted_iota(jnp.int32, (tq, tq), 0)
           <= lax.broadcasted_iota(jnp.int32, (tq, tq), 1))
    accumulate(qi, jnp.where(tri, st, NEG_INF))
    o_ref[...] = (acc_ref[...] / l_ref[...]).T.astype(o_ref.dtype)


def mla_attention(qt, k, vt, batch, seq, heads):
    tq = _pick(seq, 512, LANES)
    sub = tq // 2
    nq = seq // tq
    return pl.pallas_call(
        functools.partial(_mla_attn_body, tq=tq, sub=sub),
        out_shape=jax.ShapeDtypeStruct((batch * seq, heads * MLA_V_DIM), BF16),
        grid=(batch, heads, nq),
        in_specs=[pl.BlockSpec((MLA_HEAD_PAD, tq), lambda b, h, qi: (h, b * nq + qi)),
                  pl.BlockSpec((seq, MLA_HEAD_PAD), lambda b, h, qi: (b, h)),
                  pl.BlockSpec((MLA_V_DIM, seq), lambda b, h, qi: (h, b))],
        out_specs=pl.BlockSpec((tq, MLA_V_DIM), lambda b, h, qi: (b * nq + qi, h)),
        scratch_shapes=[pltpu.VMEM((1, tq), F32), pltpu.VMEM((1, tq), F32),
                        pltpu.VMEM((MLA_V_DIM, tq), F32)],
        compiler_params=_cparams(("parallel", "parallel", "parallel")),
        name="mla_attention",
    )(qt, k, vt)


def _swa_body(sink_ref, q_ref, kp_ref, kc_ref, vp_ref, vc_ref, o_ref, *, group):
    w = SWA_WINDOW
    n = pl.program_id(1)
    scale = SWA_HEAD_DIM ** -0.5
    lane = lax.broadcasted_iota(jnp.int32, (2 * w, LANES), 1)
    lo = lane < SWA_HEAD_DIM
    qlane_lo = lax.broadcasted_iota(jnp.int32, (w, LANES), 1) < SWA_HEAD_DIM
    ri = lax.broadcasted_iota(jnp.int32, (2 * w, 2 * w), 0) % w
    cj = lax.broadcasted_iota(jnp.int32, (2 * w, 2 * w), 1)
    valid = (cj > ri) & (cj <= ri + w) & ((cj >= w) | (n > 0))
    row_first = lax.broadcasted_iota(jnp.int32, (2 * w, 1), 0) < w
    kfull = jnp.concatenate([kp_ref[...], kc_ref[...]], axis=0).astype(F32)
    vfull = jnp.concatenate([vp_ref[...], vc_ref[...]], axis=0).astype(F32)
    n_chunks = kfull.shape[1] // LANES
    for c in range(n_chunks):
        kc = kfull[:, c * LANES:(c + 1) * LANES]
        vc = vfull[:, c * LANES:(c + 1) * LANES]
        kr = pltpu.roll(kc, SWA_HEAD_DIM, axis=1)
        vr = pltpu.roll(vc, SWA_HEAD_DIM, axis=1)
        for sub in range(2):
            hk = 2 * c + sub
            k2 = (jnp.where(lo, kc, kr) if sub == 0 else jnp.where(lo, kr, kc)).astype(BF16)
            v2 = (jnp.where(lo, vc, vr) if sub == 0 else jnp.where(lo, vr, vc)).astype(BF16)
            for j in range(group // 2):
                chunk = hk * (group // 2) + j
                hq = hk * group + 2 * j
                qp = q_ref[:, chunk * LANES:(chunk + 1) * LANES].astype(F32)
                qa = jnp.where(qlane_lo, qp, 0.0)
                qb = jnp.where(qlane_lo, 0.0, qp)
                qs = jnp.concatenate([qa, qb], axis=0).astype(BF16)
                s = lax.dot_general(qs, k2, (((1,), (1,)), ((), ())),
                                    preferred_element_type=F32) * scale
                s = jnp.where(valid, s, NEG_INF)
                sink = jnp.where(row_first, sink_ref[hq], sink_ref[hq + 1])
                mx = jnp.maximum(jnp.max(s, axis=-1, keepdims=True), sink)
                p = jnp.exp(s - mx)
                denom = jnp.sum(p, axis=-1, keepdims=True) + jnp.exp(sink - mx)
                o2 = jnp.dot((p / denom).astype(BF16), v2, preferred_element_type=F32)
                o_ref[:, chunk * LANES:(chunk + 1) * LANES] = jnp.where(
                    qlane_lo, o2[:w], o2[w:]).astype(o_ref.dtype)


def swa_attention(z_s, sinks, batch, seq, q_heads, kv_heads):
    w = SWA_WINDOW
    nb = seq // w
    qw = q_heads * SWA_HEAD_DIM
    kw = kv_heads * SWA_HEAD_DIM
    assert qw % kw == 0 and kw % LANES == 0 and (q_heads // kv_heads) % 2 == 0
    kb = qw // kw
    cur = lambda blk: (lambda b, n, s: (b * nb + n, blk))
    prev = lambda blk: (lambda b, n, s: (b * nb + jnp.maximum(n - 1, 0), blk))
    return pl.pallas_call(
        functools.partial(_swa_body, group=q_heads // kv_heads),
        out_shape=jax.ShapeDtypeStruct((batch * seq, qw), BF16),
        grid_spec=pltpu.PrefetchScalarGridSpec(
            num_scalar_prefetch=1,
            grid=(batch, nb),
            in_specs=[pl.BlockSpec((w, qw), cur(0)),
                      pl.BlockSpec((w, kw), prev(kb)),
                      pl.BlockSpec((w, kw), cur(kb)),
                      pl.BlockSpec((w, kw), prev(kb + 1)),
                      pl.BlockSpec((w, kw), cur(kb + 1))],
            out_specs=pl.BlockSpec((w, qw), cur(0))),
        compiler_params=_cparams(("parallel", "arbitrary")),
        name="swa_attention",
    )(sinks, z_s, z_s, z_s, z_s, z_s)


def _gla_body(q_ref, k_ref, v_ref, g_ref, a_ref, wa_ref, ba_ref, gn_ref, o_ref,
              state_ref, b_ref, *, chunk, tokens, n_heads):
    c_sz = chunk
    dk, dv = GLA_DK, GLA_DV
    t = pl.program_id(2)

    @pl.when(t == 0)
    def _():
        state_ref[...] = jnp.zeros_like(state_ref)

    z = jnp.dot(a_ref[...], wa_ref[...], precision=HIGHEST, preferred_element_type=F32) + ba_ref[...]
    log_a = (jnp.minimum(z, 0.0) - jnp.log(1.0 + jnp.exp(-jnp.abs(z)))) * (1.0 / GLA_GATE_TAU)
    ri = lax.broadcasted_iota(jnp.int32, (tokens, tokens), 0)
    ci = lax.broadcasted_iota(jnp.int32, (tokens, tokens), 1)
    tri = ((ci <= ri) & (ci // c_sz == ri // c_sz)).astype(BF16)
    hi = log_a.astype(BF16)
    r1 = log_a - hi.astype(F32)
    mid = r1.astype(BF16)
    lo = (r1 - mid.astype(F32)).astype(BF16)
    b_ref[...] = (jnp.dot(tri, hi, preferred_element_type=F32)
                  + jnp.dot(tri, mid, preferred_element_type=F32)
                  + jnp.dot(tri, lo, preferred_element_type=F32))

    qscale = GLA_DK ** -0.5
    sub = GLA_SUB
    n_sub = c_sz // sub
    rowid = lax.broadcasted_iota(jnp.int32, (sub, 1), 0)

    def head_chunk(c0, hh):
        rows = pl.ds(c0, c_sz)
        kcols = slice(hh * dk, (hh + 1) * dk)
        vcols = slice(hh * dv, (hh + 1) * dv)
        b = b_ref[rows, kcols]
        q = q_ref[rows, kcols].astype(F32) * qscale
        k = k_ref[rows, kcols].astype(F32)
        v = v_ref[rows, vcols]
        vf = v.astype(F32)
        state_t = state_ref[hh]
        o_inter = lax.dot_general((q * jnp.exp(b)).astype(BF16), state_t.astype(BF16),
                                  (((1,), (1,)), ((), ())), preferred_element_type=F32)
        pieces = []
        for i in range(n_sub):
            r0 = i * sub
            b_i, q_i, k_i, v_i = b[r0:r0 + sub], q[r0:r0 + sub], k[r0:r0 + sub], vf[r0:r0 + sub]
            acc = jnp.zeros((sub, dv), F32)
            for j in range(sub):
                wgt = jnp.exp(b_i - b_i[j:j + 1, :])
                col = jnp.sum(q_i * k_i[j:j + 1, :] * wgt, axis=-1, keepdims=True)
                col = jnp.where(rowid >= j, col, 0.0)
                acc = acc + col * v_i[j:j + 1, :]
            if i > 0:
                ref_b = b[r0 - 1:r0, :]
                qh = (q_i * jnp.exp(b_i - ref_b)).astype(BF16)
                kh = (k[:r0] * jnp.exp(ref_b - b[:r0])).astype(BF16)
                att = lax.dot_general(qh, kh, (((1,), (1,)), ((), ())), preferred_element_type=F32)
                acc = acc + jnp.dot(att.astype(BF16), v[:r0], preferred_element_type=F32)
            pieces.append(acc)
        o = o_inter + jnp.concatenate(pieces, axis=0)
        b_last = b[c_sz - 1:c_sz, :]
        k_dec = (k * jnp.exp(b_last - b)).astype(BF16)
        upd = lax.dot_general(v, k_dec, (((0,), (0,)), ((), ())), preferred_element_type=F32)
        state_ref[hh] = state_t * jnp.exp(b_last) + upd
        y = o * lax.rsqrt(jnp.mean(o * o, axis=-1, keepdims=True) + RMS_EPS) * gn_ref[...]
        g = g_ref[rows, vcols].astype(F32)
        y = y * (g / (1.0 + jnp.exp(-g)))
        o_ref[rows, vcols] = y.astype(o_ref.dtype)

    def chunk_step(c, carry):
        c0 = pl.multiple_of(c * c_sz, c_sz)
        for hh in range(n_heads):
            head_chunk(c0, hh)
        return carry

    lax.fori_loop(0, tokens // c_sz, chunk_step, 0)


def gla_attention(z_qkv, z_gg, z_a, a_blk, w_a2p, b_a, out_norm, batch, seq, heads, chunk=64):
    dk, dv = GLA_DK, GLA_DV
    tokens = _pick(seq, 256, chunk)
    nt = seq // tokens
    nh = 2
    assert heads % nh == 0
    wk, wv = nh * dk, nh * dv
    k_off = heads * dk // wk
    v_off = 2 * heads * dk // wv
    row = lambda blk_fn: (lambda b, h, t: (b * nt + t, blk_fn(h)))
    return pl.pallas_call(
        functools.partial(_gla_body, chunk=chunk, tokens=tokens, n_heads=nh),
        out_shape=jax.ShapeDtypeStruct((batch * seq, heads * dv), BF16),
        grid=(batch, heads // nh, nt),
        in_specs=[pl.BlockSpec((tokens, wk), row(lambda h: h)),
                  pl.BlockSpec((tokens, wk), row(lambda h: k_off + h)),
                  pl.BlockSpec((tokens, wv), row(lambda h: v_off + h)),
                  pl.BlockSpec((tokens, wv), row(lambda h: h)),
                  pl.BlockSpec((tokens, LANES), row(lambda h: a_blk)),
                  pl.BlockSpec((LANES, wk), lambda b, h, t: (0, h)),
                  pl.BlockSpec((1, wk), lambda b, h, t: (0, h)),
                  pl.BlockSpec((1, dv), lambda b, h, t: (0, 0))],
        out_specs=pl.BlockSpec((tokens, wv), row(lambda h: h)),
        scratch_shapes=[pltpu.VMEM((nh, dv, dk), F32), pltpu.VMEM((tokens, wk), F32)],
        compiler_params=_cparams(("parallel", "parallel", "arbitrary")),
        name="gla_attention",
    )(z_qkv, z_qkv, z_qkv, z_gg, z_a, w_a2p, b_a.reshape(1, -1), out_norm.reshape(1, dv))


def _merge_body(h_ref, y0_ref, y1_ref, y2_ref, wg0_ref, wg1_ref, wg2_ref,
                wb0_ref, wb1_ref, wb2_ref, bias_ref, o_ref):
    h = h_ref[...]
    acc = None
    branches = ((y0_ref, wg0_ref, wb0_ref), (y1_ref, wg1_ref, wb1_ref), (y2_ref, wg2_ref, wb2_ref))
    for b, (y_ref, wg_ref, wb_ref) in enumerate(branches):
        logit = jnp.dot(h, wg_ref[...], preferred_element_type=F32) + bias_ref[b:b + 1, :]
        gate = 1.0 / (1.0 + jnp.exp(-logit))
        term = gate * jnp.dot(y_ref[...], wb_ref[...], preferred_element_type=F32)
        acc = term if acc is None else acc + term
    o_ref[...] = acc.astype(o_ref.dtype)


def merge_branches(h, ys, w_gates, w_brs, bias3):
    m, d = h.shape
    bw = ys[0].shape[1]
    tm = _pick(m, 512, 16)
    tn = _pick(d, 256)
    nj = d // tn
    gate_spec = lambda b: pl.BlockSpec((d, tn), lambda j, i: (0, b * nj + j))
    return pl.pallas_call(
        _merge_body,
        out_shape=jax.ShapeDtypeStruct((m, d), BF16),
        grid=(nj, m // tm),
        in_specs=[pl.BlockSpec((tm, d), lambda j, i: (i, 0))]
        + [pl.BlockSpec((tm, bw), lambda j, i: (i, 0))] * N_BRANCHES
        + [gate_spec(b) for b in range(N_BRANCHES)]
        + [pl.BlockSpec((bw, tn), lambda j, i: (0, j))] * N_BRANCHES
        + [pl.BlockSpec((N_BRANCHES, tn), lambda j, i: (0, j))],
        out_specs=pl.BlockSpec((tm, tn), lambda j, i: (i, j)),
        compiler_params=_cparams(("parallel", "parallel")),
        name="merge_branches",
    )(h, *ys, w_gates, w_gates, w_gates, *w_brs, bias3)


def _pack_bf16_pairs(x):
    c = x.shape[1] // 2
    bits = lax.bitcast_convert_type(x.astype(jnp.bfloat16).astype(F32), jnp.uint32)
    return (bits[:, c:] & jnp.uint32(0xFFFF0000)) | (bits[:, :c] >> 16)


def _unpack_bf16_pairs(p):
    lo = lax.bitcast_convert_type(p << 16, F32).astype(BF16)
    hi = lax.bitcast_convert_type(p & jnp.uint32(0xFFFF0000), F32).astype(BF16)
    return lo, hi


def _router_body(x_ref, g_ref, w_ref, b_ref, h_ref, e_ref, wt_ref, rk_ref, cnt_ref, carry_ref,
                 *, n_exp, n_grp, per_grp):
    i = pl.program_id(0)

    @pl.when(i == 0)
    def _():
        carry_ref[...] = jnp.zeros_like(carry_ref)

    x = x_ref[...]
    tm = x.shape[0]
    hn = x * lax.rsqrt(jnp.mean(x * x, axis=-1, keepdims=True) + RMS_EPS) * g_ref[...]
    h_ref[...] = _pack_bf16_pairs(hn)
    logits = jnp.dot(hn, w_ref[...], precision=HIGHEST, preferred_element_type=F32) + b_ref[...]
    lane = lax.broadcasted_iota(jnp.int32, (tm, LANES), 1)
    big = jnp.int32(LANES)

    def top1(vals):
        mx = jnp.max(vals, axis=-1, keepdims=True)
        idx = jnp.min(jnp.where(vals == mx, lane, big), axis=-1, keepdims=True)
        return mx, idx

    gl = jnp.where((lane >= n_exp) & (lane < n_exp + n_grp), logits, NEG_INF)
    gmax, gidx = top1(gl)
    g_top_p = 1.0 / jnp.sum(jnp.exp(gl - gmax), axis=-1, keepdims=True)
    grp = gidx - n_exp
    el = jnp.where((lane >= grp * per_grp) & (lane < (grp + 1) * per_grp), logits, NEG_INF)
    m1, i1 = top1(el)
    m2, i2 = top1(jnp.where(lane == i1, NEG_INF, el))
    e21 = jnp.exp(m2 - m1)
    w1 = g_top_p / (1.0 + e21)
    w2 = w1 * e21
    e_ref[...] = jnp.where(lane == 0, i1, jnp.where(lane == 1, i2, 0))[:, :8]
    wt_ref[...] = jnp.where(lane == 0, w1, jnp.where(lane == 1, w2, 0.0))[:, :8]

    oh1 = (lane == i1).astype(F32)
    oh2 = (lane == i2).astype(F32)
    ri = lax.broadcasted_iota(jnp.int32, (tm, tm), 0)
    ci = lax.broadcasted_iota(jnp.int32, (tm, tm), 1)
    strict = (ci < ri).astype(BF16)
    cum1 = jnp.dot(strict, oh1.astype(BF16), preferred_element_type=F32)
    cum2 = jnp.dot(strict, oh2.astype(BF16), preferred_element_type=F32)
    tot1 = jnp.sum(oh1, axis=0, keepdims=True)
    tot2 = jnp.sum(oh2, axis=0, keepdims=True)
    carry = carry_ref[...]
    r1 = jnp.sum(oh1 * (cum1 + carry), axis=-1, keepdims=True)
    r2 = jnp.sum(oh2 * (cum2 + carry + tot1), axis=-1, keepdims=True)
    rk_ref[...] = jnp.where(lane == 0, r1, jnp.where(lane == 1, r2, 0.0))[:, :8].astype(jnp.int32)
    carry = carry + tot1 + tot2
    carry_ref[...] = carry
    cnt_ref[...] = jnp.broadcast_to(carry, cnt_ref.shape).astype(jnp.int32)


def moe_router(x, gain, w_r, b_r, n_exp, n_grp):
    m, d = x.shape
    tm = _pick(m, 256, 8)
    row = lambda i: (i, 0)
    const = lambda i: (0, 0)
    return pl.pallas_call(
        functools.partial(_router_body, n_exp=n_exp, n_grp=n_grp, per_grp=n_exp // n_grp),
        out_shape=(jax.ShapeDtypeStruct((m, d // 2), jnp.uint32),
                   jax.ShapeDtypeStruct((m, 8), jnp.int32),
                   jax.ShapeDtypeStruct((m, 8), F32),
                   jax.ShapeDtypeStruct((m, 8), jnp.int32),
                   jax.ShapeDtypeStruct((8, LANES), jnp.int32)),
        grid=(m // tm,),
        in_specs=[pl.BlockSpec((tm, d), row), pl.BlockSpec((1, d), const),
                  pl.BlockSpec((d, LANES), const), pl.BlockSpec((1, LANES), const)],
        out_specs=(pl.BlockSpec((tm, d // 2), row), pl.BlockSpec((tm, 8), row),
                   pl.BlockSpec((tm, 8), row), pl.BlockSpec((tm, 8), row),
                   pl.BlockSpec((8, LANES), const)),
        scratch_shapes=[pltpu.VMEM((1, LANES), F32)],
        compiler_params=_cparams(("arbitrary",)),
        name="moe_router",
    )(x, gain.reshape(1, d), w_r, b_r)


def _dispatch_body(p1_ref, p2_ref, h_ref, xs_in_ref, xs_ref, sem, *, tokens):
    del xs_in_ref
    base = pl.program_id(0) * tokens

    def copies(t):
        src = h_ref.at[pl.ds(t, 1), :]
        return (pltpu.make_async_copy(src, xs_ref.at[pl.ds(p1_ref[base + t], 1), :], sem),
                pltpu.make_async_copy(src, xs_ref.at[pl.ds(p2_ref[base + t], 1), :], sem))

    def start(t, c):
        a, b = copies(t)
        a.start()
        b.start()
        return c

    def wait(t, c):
        a, b = copies(t)
        a.wait()
        b.wait()
        return c

    lax.fori_loop(0, tokens, start, 0, unroll=8)
    lax.fori_loop(0, tokens, wait, 0, unroll=8)


def moe_dispatch(h, pos1, pos2, rows):
    m, d = h.shape
    tokens = _pick(m, 256, 8)
    xs0 = jnp.zeros((rows, d), h.dtype)
    return pl.pallas_call(
        functools.partial(_dispatch_body, tokens=tokens),
        out_shape=jax.ShapeDtypeStruct((rows, d), h.dtype),
        grid_spec=pltpu.PrefetchScalarGridSpec(
            num_scalar_prefetch=2,
            grid=(m // tokens,),
            in_specs=[pl.BlockSpec((tokens, d), lambda i, p1, p2: (i, 0)),
                      pl.BlockSpec(memory_space=pl.ANY)],
            out_specs=pl.BlockSpec(memory_space=pl.ANY),
            scratch_shapes=[pltpu.SemaphoreType.DMA(())]),
        input_output_aliases={3: 0},
        compiler_params=_cparams(("arbitrary",)),
        name="moe_dispatch",
    )(pos1, pos2, h, xs0)


def _experts_up_body(te_ref, ts_ref, tv_ref, x_ref, wg_ref, wu_ref, h_ref):
    del te_ref, ts_ref
    i = pl.program_id(1)
    half = x_ref.shape[1]

    @pl.when(tv_ref[i] > 0)
    def _():
        lo, hi = _unpack_bf16_pairs(x_ref[...])

        def proj(w_ref):
            return (jnp.dot(lo, w_ref[:half, :].astype(BF16), preferred_element_type=F32)
                    + jnp.dot(hi, w_ref[half:, :].astype(BF16), preferred_element_type=F32))

        a = proj(wg_ref)
        u = proj(wu_ref)
        h_ref[...] = ((a / (1.0 + jnp.exp(-a))) * u).astype(h_ref.dtype)

    @pl.when(tv_ref[i] == 0)
    def _():
        h_ref[...] = jnp.zeros_like(h_ref)


def _experts_down_body(te_ref, ts_ref, tv_ref, h_ref, wd_ref, o_ref):
    del te_ref, ts_ref
    i = pl.program_id(1)

    @pl.when(tv_ref[i] > 0)
    def _():
        o_ref[...] = jnp.dot(h_ref[...], wd_ref[...].astype(BF16), preferred_element_type=F32)

    @pl.when(tv_ref[i] == 0)
    def _():
        o_ref[...] = jnp.zeros_like(o_ref)


def moe_experts(xs, tile_expert, tile_src, tile_valid, w_gate, w_up, w_down, layer, tm):
    rows = xs.shape[0]
    d, ff = w_gate.shape[2], w_gate.shape[3]
    n_tiles = rows // tm
    tf = _pick(ff, 384)
    td = _pick(d, 2048)
    prefetch = (tile_expert, tile_src, tile_valid)
    w_map = lambda j, i, te, ts, tv: (layer, te[i], 0, j)
    hid = pl.pallas_call(
        _experts_up_body,
        out_shape=jax.ShapeDtypeStruct((rows, ff), BF16),
        grid_spec=pltpu.PrefetchScalarGridSpec(
            num_scalar_prefetch=3,
            grid=(ff // tf, n_tiles),
            in_specs=[pl.BlockSpec((tm, d // 2), lambda j, i, te, ts, tv: (ts[i], 0)),
                      pl.BlockSpec((None, None, d, tf), w_map),
                      pl.BlockSpec((None, None, d, tf), w_map)],
            out_specs=pl.BlockSpec((tm, tf), lambda j, i, te, ts, tv: (i, j))),
        compiler_params=_cparams(("arbitrary", "arbitrary")),
        name="moe_experts_up",
    )(*prefetch, xs, w_gate, w_up)
    return pl.pallas_call(
        _experts_down_body,
        out_shape=jax.ShapeDtypeStruct((rows, d), F32),
        grid_spec=pltpu.PrefetchScalarGridSpec(
            num_scalar_prefetch=3,
            grid=(d // td, n_tiles),
            in_specs=[pl.BlockSpec((tm, ff), lambda j, i, te, ts, tv: (ts[i], 0)),
                      pl.BlockSpec((None, None, ff, td), w_map)],
            out_specs=pl.BlockSpec((tm, td), lambda j, i, te, ts, tv: (i, j))),
        compiler_params=_cparams(("arbitrary", "arbitrary")),
        name="moe_experts_down",
    )(*prefetch, hid, w_down)


def _combine_body(p1_ref, p2_ref, x_ref, wt_ref, g_ref, ys_ref, *rest, tokens, emit_x):
    if emit_x:
        xo_ref, no_ref, buf_ref, sem = rest
    else:
        no_ref, buf_ref, sem = rest
    base = pl.program_id(0) * tokens

    def copies(t):
        return (pltpu.make_async_copy(ys_ref.at[pl.ds(p1_ref[base + t], 1), :],
                                      buf_ref.at[0, pl.ds(t, 1), :], sem),
                pltpu.make_async_copy(ys_ref.at[pl.ds(p2_ref[base + t], 1), :],
                                      buf_ref.at[1, pl.ds(t, 1), :], sem))

    def start(t, c):
        a, b = copies(t)
        a.start()
        b.start()
        return c

    def wait(t, c):
        a, b = copies(t)
        a.wait()
        b.wait()
        return c

    lax.fori_loop(0, tokens, start, 0, unroll=8)
    lax.fori_loop(0, tokens, wait, 0, unroll=8)
    wt = wt_ref[...]
    x = x_ref[...] + wt[:, 0:1] * buf_ref[0] + wt[:, 1:2] * buf_ref[1]
    if emit_x:
        xo_ref[...] = x
    y = x * lax.rsqrt(jnp.mean(x * x, axis=-1, keepdims=True) + RMS_EPS)
    no_ref[...] = (y * g_ref[...]).astype(no_ref.dtype)


def moe_combine(x, ys, pos1, pos2, wts, gain, norm_dtype, emit_x):
    m, d = x.shape
    tokens = _pick(m, 256, 8)
    row = lambda i, p1, p2: (i, 0)
    out_shape = [jax.ShapeDtypeStruct((m, d), norm_dtype)]
    out_specs = [pl.BlockSpec((tokens, d), row)]
    if emit_x:
        out_shape.insert(0, jax.ShapeDtypeStruct((m, d), F32))
        out_specs.insert(0, pl.BlockSpec((tokens, d), row))
    return pl.pallas_call(
        functools.partial(_combine_body, tokens=tokens, emit_x=emit_x),
        out_shape=tuple(out_shape),
        grid_spec=pltpu.PrefetchScalarGridSpec(
            num_scalar_prefetch=2,
            grid=(m // tokens,),
            in_specs=[pl.BlockSpec((tokens, d), row),
                      pl.BlockSpec((tokens, 8), row),
                      pl.BlockSpec((1, d), lambda i, p1, p2: (0, 0)),
                      pl.BlockSpec(memory_space=pl.ANY)],
            out_specs=tuple(out_specs),
            scratch_shapes=[pltpu.VMEM((2, tokens, d), F32), pltpu.SemaphoreType.DMA(())]),
        compiler_params=_cparams(("arbitrary",)),
        name="moe_combine",
    )(pos1, pos2, x, wts, gain.reshape(1, d), ys)


def _rope_partner(w_rot):
    half = MLA_ROPE_DIM // 2
    return jnp.concatenate([-w_rot[..., half:], w_rot[..., :half]], axis=-1)


def _repack_body(a_ref, o_ref):
    o_ref[...] = a_ref[0].T.astype(o_ref.dtype)


def repack_columns(wt3, layer, col_start, width):
    k = wt3.shape[2]
    assert col_start % 8 == 0 and width % LANES == 0
    tn = _pick(width, 512)
    return pl.pallas_call(
        _repack_body,
        out_shape=jax.ShapeDtypeStruct((k, width), BF16),
        grid=(width // tn,),
        in_specs=[pl.BlockSpec((pl.Element(1), pl.Element(tn), pl.Element(k)),
                               lambda j: (layer, pl.multiple_of(col_start + j * tn, 8), 0))],
        out_specs=pl.BlockSpec((k, tn), lambda j: (0, j)),
        compiler_params=_cparams(("parallel",)),
        name="repack_columns",
    )(wt3)


def _layer_weights(wt3, layer, w_uq, heads_mla):
    d = wt3.shape[2]
    gla_qk = GLA_HEADS * GLA_DK
    gla_w = GLA_HEADS * GLA_DV
    swa_q = SWA_Q_HEADS * SWA_HEAD_DIM
    swa_kv = SWA_KV_HEADS * SWA_HEAD_DIM
    sizes = (MLA_Q_LORA, MLA_KV_LORA, MLA_ROPE_DIM, gla_qk, gla_qk, gla_w, GLA_GATE_RANK, gla_w,
             swa_q, swa_kv, swa_kv, N_BRANCHES * d)
    offs = [int(o) for o in np.concatenate([[0], np.cumsum(sizes)])]
    cols = lambda a, b: wt3[layer, a:b, :]
    zcols = lambda n: jnp.zeros((n, d), wt3.dtype)
    kr = cols(offs[2], offs[3])
    half = MLA_ROPE_DIM // 2
    kr_partner = jnp.concatenate([-kr[half:], kr[:half]], axis=0)
    w_a = repack_columns(wt3, layer, 0, offs[2])
    w_r = jnp.concatenate([kr, zcols(LANES - MLA_ROPE_DIM), kr_partner, zcols(LANES - MLA_ROPE_DIM),
                           cols(offs[6], offs[7]), zcols(LANES - GLA_GATE_RANK)], axis=0).T
    w_qkv = repack_columns(wt3, layer, offs[3], offs[6] - offs[3])
    w_gg = repack_columns(wt3, layer, offs[7], offs[8] - offs[7])
    w_s = repack_columns(wt3, layer, offs[8], offs[11] - offs[8])
    w_gates = repack_columns(wt3, layer, offs[11], offs[12] - offs[11])
    qk = MLA_NOPE_DIM + MLA_ROPE_DIM
    wq = w_uq.reshape(-1, heads_mla, qk)
    pad = jnp.zeros((wq.shape[0], heads_mla, MLA_HEAD_PAD - qk), w_uq.dtype)
    w_q1 = jnp.concatenate([wq, pad], axis=-1).reshape(wq.shape[0], -1).astype(BF16)
    rot = wq[..., MLA_NOPE_DIM:]
    pad2 = jnp.zeros((wq.shape[0], heads_mla, LANES - MLA_ROPE_DIM), w_uq.dtype)
    w_q2 = jnp.concatenate([_rope_partner(rot), pad2], axis=-1).reshape(wq.shape[0], -1).astype(BF16)
    return w_a, w_r, w_qkv, w_gg, w_s, w_gates, w_q1.T, w_q2.T


def _moe_layer(x, gain, w_rg, b_rg, w_re, b_re, w_gate, w_up, w_down, layer, next_gain, norm_dtype,
               emit_x):
    m, d = x.shape
    n_exp, n_grp = w_re.shape[1], w_rg.shape[1]
    w_r = jnp.concatenate([w_re, w_rg, jnp.zeros((d, LANES - n_exp - n_grp), F32)], axis=1)
    b_r = jnp.concatenate([b_re, b_rg, jnp.zeros((LANES - n_exp - n_grp,), F32)]).reshape(1, LANES)
    hn, e_idx, wts, rank, counts = moe_router(x, gain, w_r, b_r, n_exp, n_grp)
    tm = 512 if m >= 2048 else 64
    cnt = counts[0, :n_exp]
    padded = ((cnt + tm - 1) // tm) * tm
    ends = jnp.cumsum(padded)
    offs = ends - padded
    n_tiles = (TOP_K_IN_GROUP * m) // tm + n_exp
    tile_start = jnp.arange(n_tiles, dtype=jnp.int32) * tm
    tile_valid = (tile_start < ends[-1]).astype(jnp.int32)
    last_tile = jnp.maximum(ends[-1] // tm - 1, 0).astype(jnp.int32)
    tile_src = jnp.minimum(jnp.arange(n_tiles, dtype=jnp.int32), last_tile)
    tile_expert = jnp.sum((ends[None, :] <= (tile_src * tm)[:, None]).astype(jnp.int32), axis=1)
    tile_expert = jnp.minimum(tile_expert, n_exp - 1)
    pos = offs[e_idx[:, :2]].astype(jnp.int32) + rank[:, :2]
    pos1, pos2 = pos[:, 0], pos[:, 1]
    xs = moe_dispatch(hn, pos1, pos2, n_tiles * tm)
    ys = moe_experts(xs, tile_expert, tile_src, tile_valid, w_gate, w_up, w_down, layer, tm)
    return moe_combine(x, ys, pos1, pos2, wts, next_gain, norm_dtype, emit_x)


def kernel(x, positions, attn_norm, w_in, gate_bias, mla_q_norm, mla_w_uq, mla_kv_norm, mla_w_ukv,
           gla_w_a2, gla_b_a, gla_out_norm, swa_sinks, w_br_mla, w_br_gla, w_br_swa, w_out,
           ffn_norm, w_router_group, b_router_group, w_router_expert, b_router_expert,
           w_gate, w_up, w_down, final_norm):
    batch, seq, d = x.shape
    depth = w_in.shape[0]
    m = batch * seq
    heads_mla = mla_w_uq.shape[2] // (MLA_NOPE_DIM + MLA_ROPE_DIM)
    xf = x.reshape(m, d)
    pos = positions.reshape(m, 1).astype(jnp.int32)
    half = MLA_ROPE_DIM // 2
    inv_freq = ROPE_THETA ** (-jnp.arange(0, MLA_ROPE_DIM, 2, dtype=F32) / MLA_ROPE_DIM)
    freq = jnp.concatenate([inv_freq, inv_freq, jnp.zeros((LANES - 2 * half,), F32)]).reshape(1, LANES)

    w_in_t = jnp.swapaxes(w_in, 1, 2)
    h = rmsnorm(xf, attn_norm[0], BF16)
    out = None
    for l in range(depth):
        w_a, w_r, w_qkv, w_gg, w_s, w_gates, w_q1t, w_q2t = _layer_weights(
            w_in_t, l, mla_w_uq[l], heads_mla)
        z_a = matmul(h, w_a, F32, name="in_proj_mla")
        z_r = matmul(h, w_r, F32, name="in_proj_rope_gate")
        z_qkv = matmul(h, w_qkv, BF16, name="in_proj_gla_qkv")
        z_gg = matmul(h, w_gg, BF16, name="in_proj_gla_gate")
        z_s = matmul(h, w_s, BF16, name="in_proj_swa")
        w_kv = mla_w_ukv[l].reshape(-1, heads_mla, MLA_NOPE_DIM + MLA_V_DIM)
        w_kn = w_kv[..., :MLA_NOPE_DIM].reshape(w_kv.shape[0], -1).astype(BF16)
        w_vt = w_kv[..., MLA_NOPE_DIM:].reshape(w_kv.shape[0], -1).T.astype(BF16)
        qt, k, vt = mla_prep(z_a, z_r, pos, mla_q_norm[l], mla_kv_norm[l], freq, w_q1t, w_q2t,
                             w_kn, w_vt, heads_mla)
        y_mla = mla_attention(qt, k, vt, batch, seq, heads_mla)
        w_a2p = jnp.concatenate([gla_w_a2[l], jnp.zeros((LANES - GLA_GATE_RANK, gla_w_a2.shape[2]), F32)], axis=0)
        y_gla = gla_attention(z_qkv, z_gg, z_r, 2, w_a2p, gla_b_a[l], gla_out_norm[l], batch, seq, GLA_HEADS)
        y_swa = swa_attention(z_s, swa_sinks[l], batch, seq, SWA_Q_HEADS, SWA_KV_HEADS)
        w_brs = (w_br_mla[l].astype(BF16), w_br_gla[l].astype(BF16), w_br_swa[l].astype(BF16))
        merged = merge_branches(h, (y_mla, y_gla, y_swa), w_gates, w_brs, gate_bias[l])
        xf = matmul(merged, w_out[l].astype(BF16), F32, residual=xf, name="out_proj")
        last = l == depth - 1
        res = _moe_layer(xf, ffn_norm[l], w_router_group[l], b_router_group[l],
                         w_router_expert[l], b_router_expert[l], w_gate, w_up, w_down, l,
                         final_norm if last else attn_norm[l + 1],
                         F32 if last else BF16, emit_x=not last)
        if last:
            out = res[0]
        else:
            xf, h = res
    return out.reshape(batch, seq, d)
```

```python
import functools
import math

import numpy as np
import jax
import jax.numpy as jnp
from jax import lax
from jax.experimental import pallas as pl
from jax.experimental.pallas import tpu as pltpu

F32 = jnp.float32
BF16 = jnp.bfloat16
HIGHEST = lax.Precision.HIGHEST

RMS_EPS = 1e-6
NEG_INF = -1e30
MLA_HEADS = 16
MLA_Q_LORA = 1024
MLA_KV_LORA = 512
MLA_NOPE_DIM = 128
MLA_ROPE_DIM = 64
MLA_V_DIM = 128
ROPE_THETA = 10000.0
GLA_HEADS = 8
GLA_DK = 128
GLA_DV = 256
GLA_GATE_RANK = 16
GLA_GATE_TAU = 16.0
SWA_Q_HEADS = 32
SWA_KV_HEADS = 4
SWA_HEAD_DIM = 64
SWA_WINDOW = 128
N_BRANCHES = 3
N_GROUPS = 4
EXPERTS_PER_GROUP = 8
TOP_K_IN_GROUP = 2

LANES = 128
MXU_WIDTH = 256
MLA_HEAD_PAD = 256
GLA_SUB = 16
VMEM_LIMIT = 60 * 1024 * 1024


def _cparams(sem, vmem=VMEM_LIMIT):
    return pltpu.CompilerParams(dimension_semantics=sem, vmem_limit_bytes=vmem)


def _pick(n, cap, mult=LANES):
    best = None
    for t in range(mult, min(n, cap) + 1, mult):
        if n % t == 0:
            best = t
    assert best is not None, (n, cap, mult)
    return best


def _rmsnorm_body(x_ref, g_ref, o_ref):
    x = x_ref[...]
    y = x * lax.rsqrt(jnp.mean(x * x, axis=-1, keepdims=True) + RMS_EPS)
    o_ref[...] = (y * g_ref[...]).astype(o_ref.dtype)


def rmsnorm(x, gain, out_dtype):
    m, d = x.shape
    tm = _pick(m, 512, 8)
    return pl.pallas_call(
        _rmsnorm_body,
        out_shape=jax.ShapeDtypeStruct((m, d), out_dtype),
        grid=(m // tm,),
        in_specs=[pl.BlockSpec((tm, d), lambda i: (i, 0)),
                  pl.BlockSpec((1, d), lambda i: (0, 0))],
        out_specs=pl.BlockSpec((tm, d), lambda i: (i, 0)),
        compiler_params=_cparams(("parallel",)),
        name="rmsnorm",
    )(x, gain.reshape(1, d))


def _matmul_body(a_ref, w_ref, o_ref):
    w = w_ref[...].astype(BF16)
    o_ref[...] = jnp.dot(a_ref[...], w, preferred_element_type=F32).astype(o_ref.dtype)


def _matmul_res_body(a_ref, w_ref, r_ref, o_ref):
    acc = jnp.dot(a_ref[...], w_ref[...].astype(BF16), preferred_element_type=F32)
    o_ref[...] = (r_ref[...] + acc).astype(o_ref.dtype)


def matmul(a, w, out_dtype, residual=None, tm_cap=512, tn_cap=1024, name="matmul"):
    m, k = a.shape
    n = w.shape[1]
    tm = _pick(m, tm_cap, 16)
    tn = _pick(n, tn_cap, MXU_WIDTH if n % MXU_WIDTH == 0 else LANES)
    in_specs = [pl.BlockSpec((tm, k), lambda j, i: (i, 0)),
                pl.BlockSpec((k, tn), lambda j, i: (0, j))]
    args = [a, w]
    body = _matmul_body
    if residual is not None:
        in_specs.append(pl.BlockSpec((tm, tn), lambda j, i: (i, j)))
        args.append(residual)
        body = _matmul_res_body
    return pl.pallas_call(
        body,
        out_shape=jax.ShapeDtypeStruct((m, n), out_dtype),
        grid=(n // tn, m // tm),
        in_specs=in_specs,
        out_specs=pl.BlockSpec((tm, tn), lambda j, i: (i, j)),
        compiler_params=_cparams(("parallel", "parallel")),
        name=name,
    )(*args)


def _rope_tables(pos_ref, freq_ref):
    ang = pos_ref[...].astype(F32) * freq_ref[...]
    return jnp.cos(ang), jnp.sin(ang)


def _mla_q_body(c_ref, pos_ref, g_ref, freq_ref, w1t_ref, w2t_ref, qt_ref, *, heads, scale):
    c = c_ref[...]
    cn = (c * lax.rsqrt(jnp.mean(c * c, axis=-1, keepdims=True) + RMS_EPS) * g_ref[...]).astype(BF16)
    nt = (((1,), (1,)), ((), ()))
    q = lax.dot_general(w1t_ref[...], cn, nt, preferred_element_type=F32)
    qp = lax.dot_general(w2t_ref[...], cn, nt, preferred_element_type=F32)
    ang = freq_ref[...] * pos_ref[...].astype(F32)
    cs, sn = jnp.cos(ang), jnp.sin(ang)
    for h in range(heads):
        a = h * MLA_HEAD_PAD
        qt_ref[a:a + LANES, :] = (q[a:a + LANES] * scale).astype(BF16)
        rot = q[a + LANES:a + 2 * LANES] * cs + qp[h * LANES:(h + 1) * LANES] * sn
        qt_ref[a + LANES:a + 2 * LANES, :] = (rot * scale).astype(BF16)


def _mla_kv_body(c_ref, kr_ref, krp_ref, pos_ref, g_ref, freq_ref, wk_ref, wvt_ref, k_ref, vt_ref,
                 *, heads):
    c = c_ref[...]
    cn = (c * lax.rsqrt(jnp.mean(c * c, axis=-1, keepdims=True) + RMS_EPS) * g_ref[...]).astype(BF16)
    kn = jnp.dot(cn, wk_ref[...], preferred_element_type=F32)
    vt_ref[...] = lax.dot_general(wvt_ref[...], cn, (((1,), (1,)), ((), ())),
                                  preferred_element_type=F32).astype(BF16)
    cs, sn = _rope_tables(pos_ref, freq_ref)
    krot = (kr_ref[...] * cs + krp_ref[...] * sn).astype(BF16)
    for h in range(heads):
        a = h * MLA_HEAD_PAD
        k_ref[:, a:a + LANES] = kn[:, h * LANES:(h + 1) * LANES].astype(BF16)
        k_ref[:, a + LANES:a + 2 * LANES] = krot


def mla_prep(z_a, z_r, pos, q_norm, kv_norm, freq, w_q1t, w_q2t, w_kn, w_vt, heads):
    m = z_a.shape[0]
    ql, kvl = w_q1t.shape[1], w_kn.shape[0]
    tm = _pick(m, 256, LANES)
    scale = (MLA_NOPE_DIM + MLA_ROPE_DIM) ** -0.5
    row = lambda i: (i, 0)
    const = lambda i: (0, 0)
    qt = pl.pallas_call(
        functools.partial(_mla_q_body, heads=heads, scale=scale),
        out_shape=jax.ShapeDtypeStruct((heads * MLA_HEAD_PAD, m), BF16),
        grid=(m // tm,),
        in_specs=[pl.BlockSpec((tm, ql), row),
                  pl.BlockSpec((1, tm), lambda i: (0, i)),
                  pl.BlockSpec((1, ql), const),
                  pl.BlockSpec((LANES, 1), const),
                  pl.BlockSpec(w_q1t.shape, const),
                  pl.BlockSpec(w_q2t.shape, const)],
        out_specs=pl.BlockSpec((heads * MLA_HEAD_PAD, tm), lambda i: (0, i)),
        compiler_params=_cparams(("parallel",)),
        name="mla_q_prep",
    )(z_a, pos.reshape(1, m), q_norm.reshape(1, ql), freq.reshape(LANES, 1), w_q1t, w_q2t)
    k, vt = pl.pallas_call(
        functools.partial(_mla_kv_body, heads=heads),
        out_shape=(jax.ShapeDtypeStruct((m, heads * MLA_HEAD_PAD), BF16),
                   jax.ShapeDtypeStruct((heads * MLA_V_DIM, m), BF16)),
        grid=(m // tm,),
        in_specs=[pl.BlockSpec((tm, kvl), lambda i: (i, ql // kvl)),
                  pl.BlockSpec((tm, LANES), lambda i: (i, 0)),
                  pl.BlockSpec((tm, LANES), lambda i: (i, 1)),
                  pl.BlockSpec((tm, 1), row),
                  pl.BlockSpec((1, kvl), const),
                  pl.BlockSpec((1, LANES), const),
                  pl.BlockSpec(w_kn.shape, const),
                  pl.BlockSpec(w_vt.shape, const)],
        out_specs=(pl.BlockSpec((tm, heads * MLA_HEAD_PAD), row),
                   pl.BlockSpec((heads * MLA_V_DIM, tm), lambda i: (0, i))),
        compiler_params=_cparams(("parallel",)),
        name="mla_kv_prep",
    )(z_a, z_r, z_r, pos, kv_norm.reshape(1, kvl), freq, w_kn, w_vt)
    return qt, k, vt


def _mla_attn_body(qt_ref, k_ref, vt_ref, o_ref, m_ref, l_ref, acc_ref, *, tq, sub):
    del sub
    qi = pl.program_id(2)
    m_ref[...] = jnp.full_like(m_ref, NEG_INF)
    l_ref[...] = jnp.zeros_like(l_ref)
    acc_ref[...] = jnp.zeros_like(acc_ref)

    def scores(j):
        k0 = pl.multiple_of(j * tq, tq)
        return jnp.dot(k_ref[pl.ds(k0, tq), :], qt_ref[...], preferred_element_type=F32)

    def accumulate(j, st):
        k0 = pl.multiple_of(j * tq, tq)
        m_old = m_ref[...]
        m_new = jnp.maximum(m_old, jnp.max(st, axis=0, keepdims=True))
        alpha = jnp.exp(m_old - m_new)
        p = jnp.exp(st - m_new)
        l_ref[...] = alpha * l_ref[...] + jnp.sum(p, axis=0, keepdims=True)
        acc_ref[...] = alpha * acc_ref[...] + jnp.dot(vt_ref[:, pl.ds(k0, tq)], p.astype(BF16),
                                                      preferred_element_type=F32)
        m_ref[...] = m_new

    def full_tile(j, st):
        st_next = scores(j + 1)
        accumulate(j, st)
        return st_next

    st = lax.fori_loop(0, qi, full_tile, scores(0))
    tri = (lax.broadcasted_iota(jnp.int32, (tq, tq), 0)
           <= lax.broadcasted_iota(jnp.int32, (tq, tq), 1))
    accumulate(qi, jnp.where(tri, st, NEG_INF))
    o_ref[...] = (acc_ref[...] / l_ref[...]).T.astype(o_ref.dtype)


def mla_attention(qt, k, vt, batch, seq, heads):
    tq = _pick(seq, 512, LANES)
    sub = tq // 2
    nq = seq // tq
    return pl.pallas_call(
        functools.partial(_mla_attn_body, tq=tq, sub=sub),
        out_shape=jax.ShapeDtypeStruct((batch * seq, heads * MLA_V_DIM), BF16),
        grid=(batch, heads, nq),
        in_specs=[pl.BlockSpec((MLA_HEAD_PAD, tq), lambda b, h, qi: (h, b * nq + qi)),
                  pl.BlockSpec((seq, MLA_HEAD_PAD), lambda b, h, qi: (b, h)),
                  pl.BlockSpec((MLA_V_DIM, seq), lambda b, h, qi: (h, b))],
        out_specs=pl.BlockSpec((tq, MLA_V_DIM), lambda b, h, qi: (b * nq + qi, h)),
        scratch_shapes=[pltpu.VMEM((1, tq), F32), pltpu.VMEM((1, tq), F32),
                        pltpu.VMEM((MLA_V_DIM, tq), F32)],
        compiler_params=_cparams(("parallel", "parallel", "parallel")),
        name="mla_attention",
    )(qt, k, vt)


def _swa_body(sink_ref, q_ref, kp_ref, kc_ref, vp_ref, vc_ref, o_ref, *, group):
    w = SWA_WINDOW
    n = pl.program_id(1)
    scale = SWA_HEAD_DIM ** -0.5
    lane = lax.broadcasted_iota(jnp.int32, (2 * w, LANES), 1)
    lo = lane < SWA_HEAD_DIM
    qlane_lo = lax.broadcasted_iota(jnp.int32, (w, LANES), 1) < SWA_HEAD_DIM
    ri = lax.broadcasted_iota(jnp.int32, (2 * w, 2 * w), 0) % w
    cj = lax.broadcasted_iota(jnp.int32, (2 * w, 2 * w), 1)
    valid = (cj > ri) & (cj <= ri + w) & ((cj >= w) | (n > 0))
    row_first = lax.broadcasted_iota(jnp.int32, (2 * w, 1), 0) < w
    kfull = jnp.concatenate([kp_ref[...], kc_ref[...]], axis=0).astype(F32)
    vfull = jnp.concatenate([vp_ref[...], vc_ref[...]], axis=0).astype(F32)
    n_chunks = kfull.shape[1] // LANES
    for c in range(n_chunks):
        kc = kfull[:, c * LANES:(c + 1) * LANES]
        vc = vfull[:, c * LANES:(c + 1) * LANES]
        kr = pltpu.roll(kc, SWA_HEAD_DIM, axis=1)
        vr = pltpu.roll(vc, SWA_HEAD_DIM, axis=1)
        for sub in range(2):
            hk = 2 * c + sub
            k2 = (jnp.where(lo, kc, kr) if sub == 0 else jnp.where(lo, kr, kc)).astype(BF16)
            v2 = (jnp.where(lo, vc, vr) if sub == 0 else jnp.where(lo, vr, vc)).astype(BF16)
            for j in range(group // 2):
                chunk = hk * (group // 2) + j
                hq = hk * group + 2 * j
                qp = q_ref[:, chunk * LANES:(chunk + 1) * LANES].astype(F32)
                qa = jnp.where(qlane_lo, qp, 0.0)
                qb = jnp.where(qlane_lo, 0.0, qp)
                qs = jnp.concatenate([qa, qb], axis=0).astype(BF16)
                s = lax.dot_general(qs, k2, (((1,), (1,)), ((), ())),
                                    preferred_element_type=F32) * scale
                s = jnp.where(valid, s, NEG_INF)
                sink = jnp.where(row_first, sink_ref[hq], sink_ref[hq + 1])
                mx = jnp.maximum(jnp.max(s, axis=-1, keepdims=True), sink)
                p = jnp.exp(s - mx)
                denom = jnp.sum(p, axis=-1, keepdims=True) + jnp.exp(sink - mx)
                o2 = jnp.dot((p / denom).astype(BF16), v2, preferred_element_type=F32)
                o_ref[:, chunk * LANES:(chunk + 1) * LANES] = jnp.where(
                    qlane_lo, o2[:w], o2[w:]).astype(o_ref.dtype)


def swa_attention(z_s, sinks, batch, seq, q_heads, kv_heads):
    w = SWA_WINDOW
    nb = seq // w
    qw = q_heads * SWA_HEAD_DIM
    kw = kv_heads * SWA_HEAD_DIM
    assert qw % kw == 0 and kw % LANES == 0 and (q_heads // kv_heads) % 2 == 0
    kb = qw // kw
    cur = lambda blk: (lambda b, n, s: (b * nb + n, blk))
    prev = lambda blk: (lambda b, n, s: (b * nb + jnp.maximum(n - 1, 0), blk))
    return pl.pallas_call(
        functools.partial(_swa_body, group=q_heads // kv_heads),
        out_shape=jax.ShapeDtypeStruct((batch * seq, qw), BF16),
        grid_spec=pltpu.PrefetchScalarGridSpec(
            num_scalar_prefetch=1,
            grid=(batch, nb),
            in_specs=[pl.BlockSpec((w, qw), cur(0)),
                      pl.BlockSpec((w, kw), prev(kb)),
                      pl.BlockSpec((w, kw), cur(kb)),
                      pl.BlockSpec((w, kw), prev(kb + 1)),
                      pl.BlockSpec((w, kw), cur(kb + 1))],
            out_specs=pl.BlockSpec((w, qw), cur(0))),
        compiler_params=_cparams(("parallel", "arbitrary")),
        name="swa_attention",
    )(sinks, z_s, z_s, z_s, z_s, z_s)


def _gla_body(q_ref, k_ref, v_ref, g_ref, a_ref, wa_ref, ba_ref, gn_ref, o_ref,
              state_ref, b_ref, *, chunk, tokens, n_heads):
    c_sz = chunk
    dk, dv = GLA_DK, GLA_DV
    t = pl.program_id(2)

    @pl.when(t == 0)
    def _():
        state_ref[...] = jnp.zeros_like(state_ref)

    z = jnp.dot(a_ref[...], wa_ref[...], precision=HIGHEST, preferred_element_type=F32) + ba_ref[...]
    log_a = (jnp.minimum(z, 0.0) - jnp.log(1.0 + jnp.exp(-jnp.abs(z)))) * (1.0 / GLA_GATE_TAU)
    ri = lax.broadcasted_iota(jnp.int32, (tokens, tokens), 0)
    ci = lax.broadcasted_iota(jnp.int32, (tokens, tokens), 1)
    tri = ((ci <= ri) & (ci // c_sz == ri // c_sz)).astype(BF16)
    hi = log_a.astype(BF16)
    r1 = log_a - hi.astype(F32)
    mid = r1.astype(BF16)
    lo = (r1 - mid.astype(F32)).astype(BF16)
    b_ref[...] = (jnp.dot(tri, hi, preferred_element_type=F32)
                  + jnp.dot(tri, mid, preferred_element_type=F32)
                  + jnp.dot(tri, lo, preferred_element_type=F32))

    qscale = GLA_DK ** -0.5
    sub = GLA_SUB
    n_sub = c_sz // sub
    rowid = lax.broadcasted_iota(jnp.int32, (sub, 1), 0)

    def head_chunk(c0, hh):
        rows = pl.ds(c0, c_sz)
        kcols = slice(hh * dk, (hh + 1) * dk)
        vcols = slice(hh * dv, (hh + 1) * dv)
        b = b_ref[rows, kcols]
        q = q_ref[rows, kcols].astype(F32) * qscale
        k = k_ref[rows, kcols].astype(F32)
        v = v_ref[rows, vcols]
        vf = v.astype(F32)
        state_t = state_ref[hh]
        o_inter = lax.dot_general((q * jnp.exp(b)).astype(BF16), state_t.astype(BF16),
                                  (((1,), (1,)), ((), ())), preferred_element_type=F32)
        pieces = []
        for i in range(n_sub):
            r0 = i * sub
            b_i, q_i, k_i, v_i = b[r0:r0 + sub], q[r0:r0 + sub], k[r0:r0 + sub], vf[r0:r0 + sub]
            acc = jnp.zeros((sub, dv), F32)
            for j in range(sub):
                wgt = jnp.exp(b_i - b_i[j:j + 1, :])
                col = jnp.sum(q_i * k_i[j:j + 1, :] * wgt, axis=-1, keepdims=True)
                col = jnp.where(rowid >= j, col, 0.0)
                acc = acc + col * v_i[j:j + 1, :]
            if i > 0:
                ref_b = b[r0 - 1:r0, :]
                qh = (q_i * jnp.exp(b_i - ref_b)).astype(BF16)
                kh = (k[:r0] * jnp.exp(ref_b - b[:r0])).astype(BF16)
                att = lax.dot_general(qh, kh, (((1,), (1,)), ((), ())), preferred_element_type=F32)
                acc = acc + jnp.dot(att.astype(BF16), v[:r0], preferred_element_type=F32)
            pieces.append(acc)
        o = o_inter + jnp.concatenate(pieces, axis=0)
        b_last = b[c_sz - 1:c_sz, :]
        k_dec = (k * jnp.exp(b_last - b)).astype(BF16)
        upd = lax.dot_general(v, k_dec, (((0,), (0,)), ((), ())), preferred_element_type=F32)
        state_ref[hh] = state_t * jnp.exp(b_last) + upd
        y = o * lax.rsqrt(jnp.mean(o * o, axis=-1, keepdims=True) + RMS_EPS) * gn_ref[...]
        g = g_ref[rows, vcols].astype(F32)
        y = y * (g / (1.0 + jnp.exp(-g)))
        o_ref[rows, vcols] = y.astype(o_ref.dtype)

    def chunk_step(c, carry):
        c0 = pl.multiple_of(c * c_sz, c_sz)
        for hh in range(n_heads):
            head_chunk(c0, hh)
        return carry

    lax.fori_loop(0, tokens // c_sz, chunk_step, 0)


def gla_attention(z_qkv, z_gg, z_a, a_blk, w_a2p, b_a, out_norm, batch, seq, heads, chunk=64):
    dk, dv = GLA_DK, GLA_DV
    tokens = _pick(seq, 256, chunk)
    nt = seq // tokens
    nh = math.gcd(heads, 4)
    wk, wv = nh * dk, nh * dv
    k_off = heads * dk // wk
    v_off = 2 * heads * dk // wv
    row = lambda blk_fn: (lambda b, h, t: (b * nt + t, blk_fn(h)))
    return pl.pallas_call(
        functools.partial(_gla_body, chunk=chunk, tokens=tokens, n_heads=nh),
        out_shape=jax.ShapeDtypeStruct((batch * seq, heads * dv), BF16),
        grid=(batch, heads // nh, nt),
        in_specs=[pl.BlockSpec((tokens, wk), row(lambda h: h)),
                  pl.BlockSpec((tokens, wk), row(lambda h: k_off + h)),
                  pl.BlockSpec((tokens, wv), row(lambda h: v_off + h)),
                  pl.BlockSpec((tokens, wv), row(lambda h: h)),
                  pl.BlockSpec((tokens, LANES), row(lambda h: a_blk)),
                  pl.BlockSpec((LANES, wk), lambda b, h, t: (0, h)),
                  pl.BlockSpec((1, wk), lambda b, h, t: (0, h)),
                  pl.BlockSpec((1, dv), lambda b, h, t: (0, 0))],
        out_specs=pl.BlockSpec((tokens, wv), row(lambda h: h)),
        scratch_shapes=[pltpu.VMEM((nh, dv, dk), F32), pltpu.VMEM((tokens, wk), F32)],
        compiler_params=_cparams(("parallel", "parallel", "arbitrary")),
        name="gla_attention",
    )(z_qkv, z_qkv, z_qkv, z_gg, z_a, w_a2p, b_a.reshape(1, -1), out_norm.reshape(1, dv))


def _merge_body(h_ref, y0_ref, y1_ref, y2_ref, wg0_ref, wg1_ref, wg2_ref,
                wb0_ref, wb1_ref, wb2_ref, bias_ref, o_ref):
    h = h_ref[...]
    acc = None
    branches = ((y0_ref, wg0_ref, wb0_ref), (y1_ref, wg1_ref, wb1_ref), (y2_ref, wg2_ref, wb2_ref))
    for b, (y_ref, wg_ref, wb_ref) in enumerate(branches):
        logit = jnp.dot(h, wg_ref[...], preferred_element_type=F32) + bias_ref[b:b + 1, :]
        gate = 1.0 / (1.0 + jnp.exp(-logit))
        term = gate * jnp.dot(y_ref[...], wb_ref[...], preferred_element_type=F32)
        acc = term if acc is None else acc + term
    o_ref[...] = acc.astype(o_ref.dtype)


def merge_branches(h, ys, w_gates, w_brs, bias3):
    m, d = h.shape
    bw = ys[0].shape[1]
    tm = _pick(m, 512, 16)
    tn = _pick(d, 256)
    nj = d // tn
    gate_spec = lambda b: pl.BlockSpec((d, tn), lambda j, i: (0, b * nj + j))
    return pl.pallas_call(
        _merge_body,
        out_shape=jax.ShapeDtypeStruct((m, d), BF16),
        grid=(nj, m // tm),
        in_specs=[pl.BlockSpec((tm, d), lambda j, i: (i, 0))]
        + [pl.BlockSpec((tm, bw), lambda j, i: (i, 0))] * N_BRANCHES
        + [gate_spec(b) for b in range(N_BRANCHES)]
        + [pl.BlockSpec((bw, tn), lambda j, i: (0, j))] * N_BRANCHES
        + [pl.BlockSpec((N_BRANCHES, tn), lambda j, i: (0, j))],
        out_specs=pl.BlockSpec((tm, tn), lambda j, i: (i, j)),
        compiler_params=_cparams(("parallel", "parallel")),
        name="merge_branches",
    )(h, *ys, w_gates, w_gates, w_gates, *w_brs, bias3)


def _pack_bf16_pairs(x):
    c = x.shape[1] // 2
    bits = lax.bitcast_convert_type(x.astype(jnp.bfloat16).astype(F32), jnp.uint32)
    return (bits[:, c:] & jnp.uint32(0xFFFF0000)) | (bits[:, :c] >> 16)


def _unpack_bf16_pairs(p):
    lo = lax.bitcast_convert_type(p << 16, F32).astype(BF16)
    hi = lax.bitcast_convert_type(p & jnp.uint32(0xFFFF0000), F32).astype(BF16)
    return lo, hi


def _router_body(x_ref, g_ref, w_ref, b_ref, h_ref, e_ref, wt_ref, rk_ref, cnt_ref, carry_ref,
                 *, n_exp, n_grp, per_grp):
    i = pl.program_id(0)

    @pl.when(i == 0)
    def _():
        carry_ref[...] = jnp.zeros_like(carry_ref)

    x = x_ref[...]
    tm = x.shape[0]
    hn = x * lax.rsqrt(jnp.mean(x * x, axis=-1, keepdims=True) + RMS_EPS) * g_ref[...]
    h_ref[...] = _pack_bf16_pairs(hn)
    logits = jnp.dot(hn, w_ref[...], precision=HIGHEST, preferred_element_type=F32) + b_ref[...]
    lane = lax.broadcasted_iota(jnp.int32, (tm, LANES), 1)
    big = jnp.int32(LANES)

    def top1(vals):
        mx = jnp.max(vals, axis=-1, keepdims=True)
        idx = jnp.min(jnp.where(vals == mx, lane, big), axis=-1, keepdims=True)
        return mx, idx

    gl = jnp.where((lane >= n_exp) & (lane < n_exp + n_grp), logits, NEG_INF)
    gmax, gidx = top1(gl)
    g_top_p = 1.0 / jnp.sum(jnp.exp(gl - gmax), axis=-1, keepdims=True)
    grp = gidx - n_exp
    el = jnp.where((lane >= grp * per_grp) & (lane < (grp + 1) * per_grp), logits, NEG_INF)
    m1, i1 = top1(el)
    m2, i2 = top1(jnp.where(lane == i1, NEG_INF, el))
    e21 = jnp.exp(m2 - m1)
    w1 = g_top_p / (1.0 + e21)
    w2 = w1 * e21
    e_ref[...] = jnp.where(lane == 0, i1, jnp.where(lane == 1, i2, 0))[:, :8]
    wt_ref[...] = jnp.where(lane == 0, w1, jnp.where(lane == 1, w2, 0.0))[:, :8]

    oh1 = (lane == i1).astype(F32)
    oh2 = (lane == i2).astype(F32)
    ri = lax.broadcasted_iota(jnp.int32, (tm, tm), 0)
    ci = lax.broadcasted_iota(jnp.int32, (tm, tm), 1)
    strict = (ci < ri).astype(BF16)
    cum1 = jnp.dot(strict, oh1.astype(BF16), preferred_element_type=F32)
    cum2 = jnp.dot(strict, oh2.astype(BF16), preferred_element_type=F32)
    tot1 = jnp.sum(oh1, axis=0, keepdims=True)
    tot2 = jnp.sum(oh2, axis=0, keepdims=True)
    carry = carry_ref[...]
    r1 = jnp.sum(oh1 * (cum1 + carry), axis=-1, keepdims=True)
    r2 = jnp.sum(oh2 * (cum2 + carry + tot1), axis=-1, keepdims=True)
    rk_ref[...] = jnp.where(lane == 0, r1, jnp.where(lane == 1, r2, 0.0))[:, :8].astype(jnp.int32)
    carry = carry + tot1 + tot2
    carry_ref[...] = carry
    cnt_ref[...] = jnp.broadcast_to(carry, cnt_ref.shape).astype(jnp.int32)


def moe_router(x, gain, w_r, b_r, n_exp, n_grp):
    m, d = x.shape
    tm = _pick(m, 256, 8)
    row = lambda i: (i, 0)
    const = lambda i: (0, 0)
    return pl.pallas_call(
        functools.partial(_router_body, n_exp=n_exp, n_grp=n_grp, per_grp=n_exp // n_grp),
        out_shape=(jax.ShapeDtypeStruct((m, d // 2), jnp.uint32),
                   jax.ShapeDtypeStruct((m, 8), jnp.int32),
                   jax.ShapeDtypeStruct((m, 8), F32),
                   jax.ShapeDtypeStruct((m, 8), jnp.int32),
                   jax.ShapeDtypeStruct((8, LANES), jnp.int32)),
        grid=(m // tm,),
        in_specs=[pl.BlockSpec((tm, d), row), pl.BlockSpec((1, d), const),
                  pl.BlockSpec((d, LANES), const), pl.BlockSpec((1, LANES), const)],
        out_specs=(pl.BlockSpec((tm, d // 2), row), pl.BlockSpec((tm, 8), row),
                   pl.BlockSpec((tm, 8), row), pl.BlockSpec((tm, 8), row),
                   pl.BlockSpec((8, LANES), const)),
        scratch_shapes=[pltpu.VMEM((1, LANES), F32)],
        compiler_params=_cparams(("arbitrary",)),
        name="moe_router",
    )(x, gain.reshape(1, d), w_r, b_r)


def _dispatch_body(p1_ref, p2_ref, h_ref, xs_in_ref, xs_ref, sem, *, tokens):
    del xs_in_ref
    base = pl.program_id(0) * tokens

    def copies(t):
        src = h_ref.at[pl.ds(t, 1), :]
        return (pltpu.make_async_copy(src, xs_ref.at[pl.ds(p1_ref[base + t], 1), :], sem),
                pltpu.make_async_copy(src, xs_ref.at[pl.ds(p2_ref[base + t], 1), :], sem))

    def start(t, c):
        a, b = copies(t)
        a.start()
        b.start()
        return c

    def wait(t, c):
        a, b = copies(t)
        a.wait()
        b.wait()
        return c

    lax.fori_loop(0, tokens, start, 0, unroll=8)
    lax.fori_loop(0, tokens, wait, 0, unroll=8)


def moe_dispatch(h, pos1, pos2, rows):
    m, d = h.shape
    tokens = _pick(m, 256, 8)
    xs0 = jnp.zeros((rows, d), h.dtype)
    return pl.pallas_call(
        functools.partial(_dispatch_body, tokens=tokens),
        out_shape=jax.ShapeDtypeStruct((rows, d), h.dtype),
        grid_spec=pltpu.PrefetchScalarGridSpec(
            num_scalar_prefetch=2,
            grid=(m // tokens,),
            in_specs=[pl.BlockSpec((tokens, d), lambda i, p1, p2: (i, 0)),
                      pl.BlockSpec(memory_space=pl.ANY)],
            out_specs=pl.BlockSpec(memory_space=pl.ANY),
            scratch_shapes=[pltpu.SemaphoreType.DMA(())]),
        input_output_aliases={3: 0},
        compiler_params=_cparams(("arbitrary",)),
        name="moe_dispatch",
    )(pos1, pos2, h, xs0)


def _experts_up_body(te_ref, ts_ref, tv_ref, x_ref, wg_ref, wu_ref, h_ref):
    del te_ref, ts_ref
    i = pl.program_id(1)
    half = x_ref.shape[1]

    @pl.when(tv_ref[i] > 0)
    def _():
        lo, hi = _unpack_bf16_pairs(x_ref[...])

        def proj(w_ref):
            return (jnp.dot(lo, w_ref[:half, :].astype(BF16), preferred_element_type=F32)
                    + jnp.dot(hi, w_ref[half:, :].astype(BF16), preferred_element_type=F32))

        a = proj(wg_ref)
        u = proj(wu_ref)
        h_ref[...] = ((a / (1.0 + jnp.exp(-a))) * u).astype(h_ref.dtype)

    @pl.when(tv_ref[i] == 0)
    def _():
        h_ref[...] = jnp.zeros_like(h_ref)


def _experts_down_body(te_ref, ts_ref, tv_ref, h_ref, wd_ref, o_ref):
    del te_ref, ts_ref
    i = pl.program_id(1)

    @pl.when(tv_ref[i] > 0)
    def _():
        o_ref[...] = jnp.dot(h_ref[...], wd_ref[...].astype(BF16), preferred_element_type=F32)

    @pl.when(tv_ref[i] == 0)
    def _():
        o_ref[...] = jnp.zeros_like(o_ref)


def moe_experts(xs, tile_expert, tile_src, tile_valid, w_gate, w_up, w_down, layer, tm):
    rows = xs.shape[0]
    d, ff = w_gate.shape[2], w_gate.shape[3]
    n_tiles = rows // tm
    tf = _pick(ff, 384)
    td = _pick(d, 2048)
    prefetch = (tile_expert, tile_src, tile_valid)
    w_map = lambda j, i, te, ts, tv: (layer, te[i], 0, j)
    hid = pl.pallas_call(
        _experts_up_body,
        out_shape=jax.ShapeDtypeStruct((rows, ff), BF16),
        grid_spec=pltpu.PrefetchScalarGridSpec(
            num_scalar_prefetch=3,
            grid=(ff // tf, n_tiles),
            in_specs=[pl.BlockSpec((tm, d // 2), lambda j, i, te, ts, tv: (ts[i], 0)),
                      pl.BlockSpec((None, None, d, tf), w_map),
                      pl.BlockSpec((None, None, d, tf), w_map)],
            out_specs=pl.BlockSpec((tm, tf), lambda j, i, te, ts, tv: (i, j))),
        compiler_params=_cparams(("arbitrary", "arbitrary")),
        name="moe_experts_up",
    )(*prefetch, xs, w_gate, w_up)
    return pl.pallas_call(
        _experts_down_body,
        out_shape=jax.ShapeDtypeStruct((rows, d), F32),
        grid_spec=pltpu.PrefetchScalarGridSpec(
            num_scalar_prefetch=3,
            grid=(d // td, n_tiles),
            in_specs=[pl.BlockSpec((tm, ff), lambda j, i, te, ts, tv: (ts[i], 0)),
                      pl.BlockSpec((None, None, ff, td), w_map)],
            out_specs=pl.BlockSpec((tm, td), lambda j, i, te, ts, tv: (i, j))),
        compiler_params=_cparams(("arbitrary", "arbitrary")),
        name="moe_experts_down",
    )(*prefetch, hid, w_down)


def _combine_body(p1_ref, p2_ref, x_ref, wt_ref, g_ref, ys_ref, *rest, tokens, emit_x):
    if emit_x:
        xo_ref, no_ref, buf_ref, sem = rest
    else:
        no_ref, buf_ref, sem = rest
    base = pl.program_id(0) * tokens

    def copies(t):
        return (pltpu.make_async_copy(ys_ref.at[pl.ds(p1_ref[base + t], 1), :],
                                      buf_ref.at[0, pl.ds(t, 1), :], sem),
                pltpu.make_async_copy(ys_ref.at[pl.ds(p2_ref[base + t], 1), :],
                                      buf_ref.at[1, pl.ds(t, 1), :], sem))

    def start(t, c):
        a, b = copies(t)
        a.start()
        b.start()
        return c

    def wait(t, c):
        a, b = copies(t)
        a.wait()
        b.wait()
        return c

    lax.fori_loop(0, tokens, start, 0, unroll=8)
    lax.fori_loop(0, tokens, wait, 0, unroll=8)
    wt = wt_ref[...]
    x = x_ref[...] + wt[:, 0:1] * buf_ref[0] + wt[:, 1:2] * buf_ref[1]
    if emit_x:
        xo_ref[...] = x
    y = x * lax.rsqrt(jnp.mean(x * x, axis=-1, keepdims=True) + RMS_EPS)
    no_ref[...] = (y * g_ref[...]).astype(no_ref.dtype)


def moe_combine(x, ys, pos1, pos2, wts, gain, norm_dtype, emit_x):
    m, d = x.shape
    tokens = _pick(m, 256, 8)
    row = lambda i, p1, p2: (i, 0)
    out_shape = [jax.ShapeDtypeStruct((m, d), norm_dtype)]
    out_specs = [pl.BlockSpec((tokens, d), row)]
    if emit_x:
        out_shape.insert(0, jax.ShapeDtypeStruct((m, d), F32))
        out_specs.insert(0, pl.BlockSpec((tokens, d), row))
    return pl.pallas_call(
        functools.partial(_combine_body, tokens=tokens, emit_x=emit_x),
        out_shape=tuple(out_shape),
        grid_spec=pltpu.PrefetchScalarGridSpec(
            num_scalar_prefetch=2,
            grid=(m // tokens,),
            in_specs=[pl.BlockSpec((tokens, d), row),
                      pl.BlockSpec((tokens, 8), row),
                      pl.BlockSpec((1, d), lambda i, p1, p2: (0, 0)),
                      pl.BlockSpec(memory_space=pl.ANY)],
            out_specs=tuple(out_specs),
            scratch_shapes=[pltpu.VMEM((2, tokens, d), F32), pltpu.SemaphoreType.DMA(())]),
        compiler_params=_cparams(("arbitrary",)),
        name="moe_combine",
    )(pos1, pos2, x, wts, gain.reshape(1, d), ys)


def _rope_partner(w_rot):
    half = MLA_ROPE_DIM // 2
    return jnp.concatenate([-w_rot[..., half:], w_rot[..., :half]], axis=-1)


def _repack_body(a_ref, o_ref):
    o_ref[...] = a_ref[0].T.astype(o_ref.dtype)


def repack_columns(wt3, layer, col_start, width):
    k = wt3.shape[2]
    assert col_start % 8 == 0 and width % LANES == 0
    tn = _pick(width, 512)
    return pl.pallas_call(
        _repack_body,
        out_shape=jax.ShapeDtypeStruct((k, width), BF16),
        grid=(width // tn,),
        in_specs=[pl.BlockSpec((pl.Element(1), pl.Element(tn), pl.Element(k)),
                               lambda j: (layer, pl.multiple_of(col_start + j * tn, 8), 0))],
        out_specs=pl.BlockSpec((k, tn), lambda j: (0, j)),
        compiler_params=_cparams(("parallel",)),
        name="repack_columns",
    )(wt3)


def _layer_weights(wt3, layer, w_uq, heads_mla):
    d = wt3.shape[2]
    gla_qk = GLA_HEADS * GLA_DK
    gla_w = GLA_HEADS * GLA_DV
    swa_q = SWA_Q_HEADS * SWA_HEAD_DIM
    swa_kv = SWA_KV_HEADS * SWA_HEAD_DIM
    sizes = (MLA_Q_LORA, MLA_KV_LORA, MLA_ROPE_DIM, gla_qk, gla_qk, gla_w, GLA_GATE_RANK, gla_w,
             swa_q, swa_kv, swa_kv, N_BRANCHES * d)
    offs = [int(o) for o in np.concatenate([[0], np.cumsum(sizes)])]
    cols = lambda a, b: wt3[layer, a:b, :]
    zcols = lambda n: jnp.zeros((n, d), wt3.dtype)
    kr = cols(offs[2], offs[3])
    half = MLA_ROPE_DIM // 2
    kr_partner = jnp.concatenate([-kr[half:], kr[:half]], axis=0)
    w_a = repack_columns(wt3, layer, 0, offs[2])
    w_r = jnp.concatenate([kr, zcols(LANES - MLA_ROPE_DIM), kr_partner, zcols(LANES - MLA_ROPE_DIM),
                           cols(offs[6], offs[7]), zcols(LANES - GLA_GATE_RANK)], axis=0).T
    w_qkv = repack_columns(wt3, layer, offs[3], offs[6] - offs[3])
    w_gg = repack_columns(wt3, layer, offs[7], offs[8] - offs[7])
    w_s = repack_columns(wt3, layer, offs[8], offs[11] - offs[8])
    w_gates = repack_columns(wt3, layer, offs[11], offs[12] - offs[11])
    qk = MLA_NOPE_DIM + MLA_ROPE_DIM
    wq = w_uq.reshape(-1, heads_mla, qk)
    pad = jnp.zeros((wq.shape[0], heads_mla, MLA_HEAD_PAD - qk), w_uq.dtype)
    w_q1 = jnp.concatenate([wq, pad], axis=-1).reshape(wq.shape[0], -1).astype(BF16)
    rot = wq[..., MLA_NOPE_DIM:]
    pad2 = jnp.zeros((wq.shape[0], heads_mla, LANES - MLA_ROPE_DIM), w_uq.dtype)
    w_q2 = jnp.concatenate([_rope_partner(rot), pad2], axis=-1).reshape(wq.shape[0], -1).astype(BF16)
    return w_a, w_r, w_qkv, w_gg, w_s, w_gates, w_q1.T, w_q2.T


def _moe_layer(x, gain, w_rg, b_rg, w_re, b_re, w_gate, w_up, w_down, layer, next_gain, norm_dtype,
               emit_x):
    m, d = x.shape
    n_exp, n_grp = w_re.shape[1], w_rg.shape[1]
    w_r = jnp.concatenate([w_re, w_rg, jnp.zeros((d, LANES - n_exp - n_grp), F32)], axis=1)
    b_r = jnp.concatenate([b_re, b_rg, jnp.zeros((LANES - n_exp - n_grp,), F32)]).reshape(1, LANES)
    hn, e_idx, wts, rank, counts = moe_router(x, gain, w_r, b_r, n_exp, n_grp)
    tm = 640 if m >= 2048 else 64
    cnt = counts[0, :n_exp]
    padded = ((cnt + tm - 1) // tm) * tm
    ends = jnp.cumsum(padded)
    offs = ends - padded
    n_tiles = -(-(TOP_K_IN_GROUP * m) // tm) + n_exp
    tile_start = jnp.arange(n_tiles, dtype=jnp.int32) * tm
    tile_valid = (tile_start < ends[-1]).astype(jnp.int32)
    last_tile = jnp.maximum(ends[-1] // tm - 1, 0).astype(jnp.int32)
    tile_src = jnp.minimum(jnp.arange(n_tiles, dtype=jnp.int32), last_tile)
    tile_expert = jnp.sum((ends[None, :] <= (tile_src * tm)[:, None]).astype(jnp.int32), axis=1)
    tile_expert = jnp.minimum(tile_expert, n_exp - 1)
    pos = offs[e_idx[:, :2]].astype(jnp.int32) + rank[:, :2]
    pos1, pos2 = pos[:, 0], pos[:, 1]
    xs = moe_dispatch(hn, pos1, pos2, n_tiles * tm)
    ys = moe_experts(xs, tile_expert, tile_src, tile_valid, w_gate, w_up, w_down, layer, tm)
    return moe_combine(x, ys, pos1, pos2, wts, next_gain, norm_dtype, emit_x)


def kernel(x, positions, attn_norm, w_in, gate_bias, mla_q_norm, mla_w_uq, mla_kv_norm, mla_w_ukv,
           gla_w_a2, gla_b_a, gla_out_norm, swa_sinks, w_br_mla, w_br_gla, w_br_swa, w_out,
           ffn_norm, w_router_group, b_router_group, w_router_expert, b_router_expert,
           w_gate, w_up, w_down, final_norm):
    batch, seq, d = x.shape
    depth = w_in.shape[0]
    m = batch * seq
    heads_mla = mla_w_uq.shape[2] // (MLA_NOPE_DIM + MLA_ROPE_DIM)
    xf = x.reshape(m, d)
    pos = positions.reshape(m, 1).astype(jnp.int32)
    half = MLA_ROPE_DIM // 2
    inv_freq = ROPE_THETA ** (-jnp.arange(0, MLA_ROPE_DIM, 2, dtype=F32) / MLA_ROPE_DIM)
    freq = jnp.concatenate([inv_freq, inv_freq, jnp.zeros((LANES - 2 * half,), F32)]).reshape(1, LANES)

    w_in_t = jnp.swapaxes(w_in, 1, 2)
    h = rmsnorm(xf, attn_norm[0], BF16)
    out = None
    for l in range(depth):
        w_a, w_r, w_qkv, w_gg, w_s, w_gates, w_q1t, w_q2t = _layer_weights(
            w_in_t, l, mla_w_uq[l], heads_mla)
        z_a = matmul(h, w_a, F32, name="in_proj_mla")
        z_r = matmul(h, w_r, F32, name="in_proj_rope_gate")
        z_qkv = matmul(h, w_qkv, BF16, name="in_proj_gla_qkv")
        z_gg = matmul(h, w_gg, BF16, name="in_proj_gla_gate")
        z_s = matmul(h, w_s, BF16, name="in_proj_swa")
        w_kv = mla_w_ukv[l].reshape(-1, heads_mla, MLA_NOPE_DIM + MLA_V_DIM)
        w_kn = w_kv[..., :MLA_NOPE_DIM].reshape(w_kv.shape[0], -1).astype(BF16)
        w_vt = w_kv[..., MLA_NOPE_DIM:].reshape(w_kv.shape[0], -1).T.astype(BF16)
        qt, k, vt = mla_prep(z_a, z_r, pos, mla_q_norm[l], mla_kv_norm[l], freq, w_q1t, w_q2t,
                             w_kn, w_vt, heads_mla)
        y_mla = mla_attention(qt, k, vt, batch, seq, heads_mla)
        w_a2p = jnp.concatenate([gla_w_a2[l], jnp.zeros((LANES - GLA_GATE_RANK, gla_w_a2.shape[2]), F32)], axis=0)
        y_gla = gla_attention(z_qkv, z_gg, z_r, 2, w_a2p, gla_b_a[l], gla_out_norm[l], batch, seq, GLA_HEADS)
        y_swa = swa_attention(z_s, swa_sinks[l], batch, seq, SWA_Q_HEADS, SWA_KV_HEADS)
        w_brs = (w_br_mla[l].astype(BF16), w_br_gla[l].astype(BF16), w_br_swa[l].astype(BF16))
        merged = merge_branches(h, (y_mla, y_gla, y_swa), w_gates, w_brs, gate_bias[l])
        xf = matmul(merged, w_out[l].astype(BF16), F32, residual=xf, name="out_proj")
        last = l == depth - 1
        res = _moe_layer(xf, ffn_norm[l], w_router_group[l], b_router_group[l],
                         w_router_expert[l], b_router_expert[l], w_gate, w_up, w_down, l,
                         final_norm if last else attn_norm[l + 1],
                         F32 if last else BF16, emit_x=not last)
        if last:
            out = res[0]
        else:
            xf, h = res
    return out.reshape(batch, seq, d)
```

```python
import functools
import math

import numpy as np
import jax
import jax.numpy as jnp
from jax import lax
from jax.experimental import pallas as pl
from jax.experimental.pallas import tpu as pltpu

F32 = jnp.float32
BF16 = jnp.bfloat16
HIGHEST = lax.Precision.HIGHEST

RMS_EPS = 1e-6
NEG_INF = -1e30
MLA_HEADS = 16
MLA_Q_LORA = 1024
MLA_KV_LORA = 512
MLA_NOPE_DIM = 128
MLA_ROPE_DIM = 64
MLA_V_DIM = 128
ROPE_THETA = 10000.0
GLA_HEADS = 8
GLA_DK = 128
GLA_DV = 256
GLA_GATE_RANK = 16
GLA_GATE_TAU = 16.0
SWA_Q_HEADS = 32
SWA_KV_HEADS = 4
SWA_HEAD_DIM = 64
SWA_WINDOW = 128
N_BRANCHES = 3
N_GROUPS = 4
EXPERTS_PER_GROUP = 8
TOP_K_IN_GROUP = 2

LANES = 128
MXU_WIDTH = 256
MLA_HEAD_PAD = 256
GLA_SUB = 16
VMEM_LIMIT = 60 * 1024 * 1024


def _cparams(sem, vmem=VMEM_LIMIT):
    return pltpu.CompilerParams(dimension_semantics=sem, vmem_limit_bytes=vmem)


def _pick(n, cap, mult=LANES):
    best = None
    for t in range(mult, min(n, cap) + 1, mult):
        if n % t == 0:
            best = t
    assert best is not None, (n, cap, mult)
    return best


def _rmsnorm_body(x_ref, g_ref, o_ref):
    x = x_ref[...]
    y = x * lax.rsqrt(jnp.mean(x * x, axis=-1, keepdims=True) + RMS_EPS)
    o_ref[...] = (y * g_ref[...]).astype(o_ref.dtype)


def rmsnorm(x, gain, out_dtype):
    m, d = x.shape
    tm = _pick(m, 512, 8)
    return pl.pallas_call(
        _rmsnorm_body,
        out_shape=jax.ShapeDtypeStruct((m, d), out_dtype),
        grid=(m // tm,),
        in_specs=[pl.BlockSpec((tm, d), lambda i: (i, 0)),
                  pl.BlockSpec((1, d), lambda i: (0, 0))],
        out_specs=pl.BlockSpec((tm, d), lambda i: (i, 0)),
        compiler_params=_cparams(("parallel",)),
        name="rmsnorm",
    )(x, gain.reshape(1, d))


def _matmul_body(a_ref, w_ref, o_ref):
    w = w_ref[...].astype(BF16)
    o_ref[...] = jnp.dot(a_ref[...], w, preferred_element_type=F32).astype(o_ref.dtype)


def _matmul_res_body(a_ref, w_ref, r_ref, o_ref):
    acc = jnp.dot(a_ref[...], w_ref[...].astype(BF16), preferred_element_type=F32)
    o_ref[...] = (r_ref[...] + acc).astype(o_ref.dtype)


def matmul(a, w, out_dtype, residual=None, tm_cap=512, tn_cap=1024, name="matmul"):
    m, k = a.shape
    n = w.shape[1]
    tm = _pick(m, tm_cap, 16)
    tn = _pick(n, tn_cap, MXU_WIDTH if n % MXU_WIDTH == 0 else LANES)
    in_specs = [pl.BlockSpec((tm, k), lambda j, i: (i, 0)),
                pl.BlockSpec((k, tn), lambda j, i: (0, j))]
    args = [a, w]
    body = _matmul_body
    if residual is not None:
        in_specs.append(pl.BlockSpec((tm, tn), lambda j, i: (i, j)))
        args.append(residual)
        body = _matmul_res_body
    return pl.pallas_call(
        body,
        out_shape=jax.ShapeDtypeStruct((m, n), out_dtype),
        grid=(n // tn, m // tm),
        in_specs=in_specs,
        out_specs=pl.BlockSpec((tm, tn), lambda j, i: (i, j)),
        compiler_params=_cparams(("parallel", "parallel")),
        name=name,
    )(*args)


def _rope_tables(pos_ref, freq_ref):
    ang = pos_ref[...].astype(F32) * freq_ref[...]
    return jnp.cos(ang), jnp.sin(ang)


def _mla_q_body(c_ref, pos_ref, g_ref, freq_ref, w1t_ref, w2t_ref, qt_ref, *, heads, scale):
    c = c_ref[...]
    cn = (c * lax.rsqrt(jnp.mean(c * c, axis=-1, keepdims=True) + RMS_EPS) * g_ref[...]).astype(BF16)
    nt = (((1,), (1,)), ((), ()))
    q = lax.dot_general(w1t_ref[...], cn, nt, preferred_element_type=F32)
    qp = lax.dot_general(w2t_ref[...], cn, nt, preferred_element_type=F32)
    ang = freq_ref[...] * pos_ref[...].astype(F32)
    cs, sn = jnp.cos(ang), jnp.sin(ang)
    for h in range(heads):
        a = h * MLA_HEAD_PAD
        qt_ref[a:a + LANES, :] = (q[a:a + LANES] * scale).astype(BF16)
        rot = q[a + LANES:a + 2 * LANES] * cs + qp[h * LANES:(h + 1) * LANES] * sn
        qt_ref[a + LANES:a + 2 * LANES, :] = (rot * scale).astype(BF16)


def _mla_kv_body(c_ref, kr_ref, krp_ref, pos_ref, g_ref, freq_ref, wk_ref, wvt_ref, k_ref, vt_ref,
                 *, heads):
    c = c_ref[...]
    cn = (c * lax.rsqrt(jnp.mean(c * c, axis=-1, keepdims=True) + RMS_EPS) * g_ref[...]).astype(BF16)
    kn = jnp.dot(cn, wk_ref[...], preferred_element_type=F32)
    vt_ref[...] = lax.dot_general(wvt_ref[...], cn, (((1,), (1,)), ((), ())),
                                  preferred_element_type=F32).astype(BF16)
    cs, sn = _rope_tables(pos_ref, freq_ref)
    krot = (kr_ref[...] * cs + krp_ref[...] * sn).astype(BF16)
    for h in range(heads):
        a = h * MLA_HEAD_PAD
        k_ref[:, a:a + LANES] = kn[:, h * LANES:(h + 1) * LANES].astype(BF16)
        k_ref[:, a + LANES:a + 2 * LANES] = krot


def mla_prep(z_a, z_r, pos, q_norm, kv_norm, freq, w_q1t, w_q2t, w_kn, w_vt, heads):
    m = z_a.shape[0]
    ql, kvl = w_q1t.shape[1], w_kn.shape[0]
    tm = _pick(m, 256, LANES)
    scale = (MLA_NOPE_DIM + MLA_ROPE_DIM) ** -0.5
    row = lambda i: (i, 0)
    const = lambda i: (0, 0)
    qt = pl.pallas_call(
        functools.partial(_mla_q_body, heads=heads, scale=scale),
        out_shape=jax.ShapeDtypeStruct((heads * MLA_HEAD_PAD, m), BF16),
        grid=(m // tm,),
        in_specs=[pl.BlockSpec((tm, ql), row),
                  pl.BlockSpec((1, tm), lambda i: (0, i)),
                  pl.BlockSpec((1, ql), const),
                  pl.BlockSpec((LANES, 1), const),
                  pl.BlockSpec(w_q1t.shape, const),
                  pl.BlockSpec(w_q2t.shape, const)],
        out_specs=pl.BlockSpec((heads * MLA_HEAD_PAD, tm), lambda i: (0, i)),
        compiler_params=_cparams(("parallel",)),
        name="mla_q_prep",
    )(z_a, pos.reshape(1, m), q_norm.reshape(1, ql), freq.reshape(LANES, 1), w_q1t, w_q2t)
    k, vt = pl.pallas_call(
        functools.partial(_mla_kv_body, heads=heads),
        out_shape=(jax.ShapeDtypeStruct((m, heads * MLA_HEAD_PAD), BF16),
                   jax.ShapeDtypeStruct((heads * MLA_V_DIM, m), BF16)),
        grid=(m // tm,),
        in_specs=[pl.BlockSpec((tm, kvl), lambda i: (i, ql // kvl)),
                  pl.BlockSpec((tm, LANES), lambda i: (i, 0)),
                  pl.BlockSpec((tm, LANES), lambda i: (i, 1)),
                  pl.BlockSpec((tm, 1), row),
                  pl.BlockSpec((1, kvl), const),
                  pl.BlockSpec((1, LANES), const),
                  pl.BlockSpec(w_kn.shape, const),
                  pl.BlockSpec(w_vt.shape, const)],
        out_specs=(pl.BlockSpec((tm, heads * MLA_HEAD_PAD), row),
                   pl.BlockSpec((heads * MLA_V_DIM, tm), lambda i: (0, i))),
        compiler_params=_cparams(("parallel",)),
        name="mla_kv_prep",
    )(z_a, z_r, z_r, pos, kv_norm.reshape(1, kvl), freq, w_kn, w_vt)
    return qt, k, vt


def _mla_attn_body(qt_ref, k_ref, vt_ref, o_ref, m_ref, l_ref, acc_ref, *, tq, sub):
    del sub
    qi = pl.program_id(2)
    n_heads = m_ref.shape[0]
    hd, vd = MLA_HEAD_PAD, MLA_V_DIM
    m_ref[...] = jnp.full_like(m_ref, NEG_INF)
    l_ref[...] = jnp.zeros_like(l_ref)
    acc_ref[...] = jnp.zeros_like(acc_ref)

    def scores(j, h):
        k0 = pl.multiple_of(j * tq, tq)
        return jnp.dot(k_ref[pl.ds(k0, tq), h * hd:(h + 1) * hd], qt_ref[h * hd:(h + 1) * hd, :],
                       preferred_element_type=F32)

    def accumulate(j, h, st):
        k0 = pl.multiple_of(j * tq, tq)
        m_old = m_ref[h]
        m_new = jnp.maximum(m_old, jnp.max(st, axis=0, keepdims=True))
        alpha = jnp.exp(m_old - m_new)
        p = jnp.exp(st - m_new)
        l_ref[h] = alpha * l_ref[h] + jnp.sum(p, axis=0, keepdims=True)
        acc_ref[h] = alpha * acc_ref[h] + jnp.dot(vt_ref[h * vd:(h + 1) * vd, pl.ds(k0, tq)],
                                                  p.astype(BF16), preferred_element_type=F32)
        m_ref[h] = m_new

    def full_tile(j, sts):
        nxt = tuple(scores(j + 1, h) for h in range(n_heads))
        for h in range(n_heads):
            accumulate(j, h, sts[h])
        return nxt

    sts = lax.fori_loop(0, qi, full_tile, tuple(scores(0, h) for h in range(n_heads)))
    tri = (lax.broadcasted_iota(jnp.int32, (tq, tq), 0)
           <= lax.broadcasted_iota(jnp.int32, (tq, tq), 1))
    for h in range(n_heads):
        accumulate(qi, h, jnp.where(tri, sts[h], NEG_INF))
        o_ref[:, h * vd:(h + 1) * vd] = (acc_ref[h] / l_ref[h]).T.astype(o_ref.dtype)


def mla_attention(qt, k, vt, batch, seq, heads):
    tq = _pick(seq, 512, LANES)
    sub = tq // 2
    nq = seq // tq
    nh = math.gcd(heads, 2)
    return pl.pallas_call(
        functools.partial(_mla_attn_body, tq=tq, sub=sub),
        out_shape=jax.ShapeDtypeStruct((batch * seq, heads * MLA_V_DIM), BF16),
        grid=(batch, heads // nh, nq),
        in_specs=[pl.BlockSpec((nh * MLA_HEAD_PAD, tq), lambda b, h, qi: (h, b * nq + qi)),
                  pl.BlockSpec((seq, nh * MLA_HEAD_PAD), lambda b, h, qi: (b, h)),
                  pl.BlockSpec((nh * MLA_V_DIM, seq), lambda b, h, qi: (h, b))],
        out_specs=pl.BlockSpec((tq, nh * MLA_V_DIM), lambda b, h, qi: (b * nq + qi, h)),
        scratch_shapes=[pltpu.VMEM((nh, 1, tq), F32), pltpu.VMEM((nh, 1, tq), F32),
                        pltpu.VMEM((nh, MLA_V_DIM, tq), F32)],
        compiler_params=_cparams(("parallel", "parallel", "parallel")),
        name="mla_attention",
    )(qt, k, vt)


def _swa_body(sink_ref, q_ref, kp_ref, kc_ref, vp_ref, vc_ref, o_ref, *, group):
    w = SWA_WINDOW
    n = pl.program_id(1)
    scale = SWA_HEAD_DIM ** -0.5
    lane = lax.broadcasted_iota(jnp.int32, (2 * w, LANES), 1)
    lo = lane < SWA_HEAD_DIM
    qlane_lo = lax.broadcasted_iota(jnp.int32, (w, LANES), 1) < SWA_HEAD_DIM
    ri = lax.broadcasted_iota(jnp.int32, (2 * w, 2 * w), 0) % w
    cj = lax.broadcasted_iota(jnp.int32, (2 * w, 2 * w), 1)
    valid = (cj > ri) & (cj <= ri + w) & ((cj >= w) | (n > 0))
    row_first = lax.broadcasted_iota(jnp.int32, (2 * w, 1), 0) < w
    kfull = jnp.concatenate([kp_ref[...], kc_ref[...]], axis=0).astype(F32)
    vfull = jnp.concatenate([vp_ref[...], vc_ref[...]], axis=0).astype(F32)
    n_chunks = kfull.shape[1] // LANES
    for c in range(n_chunks):
        kc = kfull[:, c * LANES:(c + 1) * LANES]
        vc = vfull[:, c * LANES:(c + 1) * LANES]
        kr = pltpu.roll(kc, SWA_HEAD_DIM, axis=1)
        vr = pltpu.roll(vc, SWA_HEAD_DIM, axis=1)
        for sub in range(2):
            hk = 2 * c + sub
            k2 = (jnp.where(lo, kc, kr) if sub == 0 else jnp.where(lo, kr, kc)).astype(BF16)
            v2 = (jnp.where(lo, vc, vr) if sub == 0 else jnp.where(lo, vr, vc)).astype(BF16)
            for j in range(group // 2):
                chunk = hk * (group // 2) + j
                hq = hk * group + 2 * j
                qp = q_ref[:, chunk * LANES:(chunk + 1) * LANES].astype(F32)
                qa = jnp.where(qlane_lo, qp, 0.0)
                qb = jnp.where(qlane_lo, 0.0, qp)
                qs = jnp.concatenate([qa, qb], axis=0).astype(BF16)
                s = lax.dot_general(qs, k2, (((1,), (1,)), ((), ())),
                                    preferred_element_type=F32) * scale
                s = jnp.where(valid, s, NEG_INF)
                sink = jnp.where(row_first, sink_ref[hq], sink_ref[hq + 1])
                mx = jnp.maximum(jnp.max(s, axis=-1, keepdims=True), sink)
                p = jnp.exp(s - mx)
                denom = jnp.sum(p, axis=-1, keepdims=True) + jnp.exp(sink - mx)
                o2 = jnp.dot((p / denom).astype(BF16), v2, preferred_element_type=F32)
                o_ref[:, chunk * LANES:(chunk + 1) * LANES] = jnp.where(
                    qlane_lo, o2[:w], o2[w:]).astype(o_ref.dtype)


def swa_attention(z_s, sinks, batch, seq, q_heads, kv_heads):
    w = SWA_WINDOW
    nb = seq // w
    qw = q_heads * SWA_HEAD_DIM
    kw = kv_heads * SWA_HEAD_DIM
    assert qw % kw == 0 and kw % LANES == 0 and (q_heads // kv_heads) % 2 == 0
    kb = qw // kw
    cur = lambda blk: (lambda b, n, s: (b * nb + n, blk))
    prev = lambda blk: (lambda b, n, s: (b * nb + jnp.maximum(n - 1, 0), blk))
    return pl.pallas_call(
        functools.partial(_swa_body, group=q_heads // kv_heads),
        out_shape=jax.ShapeDtypeStruct((batch * seq, qw), BF16),
        grid_spec=pltpu.PrefetchScalarGridSpec(
            num_scalar_prefetch=1,
            grid=(batch, nb),
            in_specs=[pl.BlockSpec((w, qw), cur(0)),
                      pl.BlockSpec((w, kw), prev(kb)),
                      pl.BlockSpec((w, kw), cur(kb)),
                      pl.BlockSpec((w, kw), prev(kb + 1)),
                      pl.BlockSpec((w, kw), cur(kb + 1))],
            out_specs=pl.BlockSpec((w, qw), cur(0))),
        compiler_params=_cparams(("parallel", "arbitrary")),
        name="swa_attention",
    )(sinks, z_s, z_s, z_s, z_s, z_s)


def _gla_body(q_ref, k_ref, v_ref, g_ref, a_ref, wa_ref, ba_ref, gn_ref, o_ref,
              state_ref, b_ref, *, chunk, tokens, n_heads):
    c_sz = chunk
    dk, dv = GLA_DK, GLA_DV
    t = pl.program_id(2)

    @pl.when(t == 0)
    def _():
        state_ref[...] = jnp.zeros_like(state_ref)

    z = jnp.dot(a_ref[...], wa_ref[...], precision=HIGHEST, preferred_element_type=F32) + ba_ref[...]
    log_a = (jnp.minimum(z, 0.0) - jnp.log(1.0 + jnp.exp(-jnp.abs(z)))) * (1.0 / GLA_GATE_TAU)
    ri = lax.broadcasted_iota(jnp.int32, (tokens, tokens), 0)
    ci = lax.broadcasted_iota(jnp.int32, (tokens, tokens), 1)
    tri = ((ci <= ri) & (ci // c_sz == ri // c_sz)).astype(BF16)
    hi = log_a.astype(BF16)
    r1 = log_a - hi.astype(F32)
    mid = r1.astype(BF16)
    lo = (r1 - mid.astype(F32)).astype(BF16)
    b_ref[...] = (jnp.dot(tri, hi, preferred_element_type=F32)
                  + jnp.dot(tri, mid, preferred_element_type=F32)
                  + jnp.dot(tri, lo, preferred_element_type=F32))

    qscale = GLA_DK ** -0.5
    sub = GLA_SUB
    n_sub = c_sz // sub
    rowid = lax.broadcasted_iota(jnp.int32, (sub, 1), 0)

    def head_chunk(c0, hh):
        rows = pl.ds(c0, c_sz)
        kcols = slice(hh * dk, (hh + 1) * dk)
        vcols = slice(hh * dv, (hh + 1) * dv)
        b = b_ref[rows, kcols]
        q = q_ref[rows, kcols].astype(F32) * qscale
        k = k_ref[rows, kcols].astype(F32)
        v = v_ref[rows, vcols]
        vf = v.astype(F32)
        state_t = state_ref[hh]
        o_inter = lax.dot_general((q * jnp.exp(b)).astype(BF16), state_t.astype(BF16),
                                  (((1,), (1,)), ((), ())), preferred_element_type=F32)
        pieces = []
        for i in range(n_sub):
            r0 = i * sub
            b_i, q_i, k_i, v_i = b[r0:r0 + sub], q[r0:r0 + sub], k[r0:r0 + sub], vf[r0:r0 + sub]
            acc = jnp.zeros((sub, dv), F32)
            for j in range(sub):
                wgt = jnp.exp(b_i - b_i[j:j + 1, :])
                col = jnp.sum(q_i * k_i[j:j + 1, :] * wgt, axis=-1, keepdims=True)
                col = jnp.where(rowid >= j, col, 0.0)
                acc = acc + col * v_i[j:j + 1, :]
            if i > 0:
                ref_b = b[r0 - 1:r0, :]
                qh = (q_i * jnp.exp(b_i - ref_b)).astype(BF16)
                kh = (k[:r0] * jnp.exp(ref_b - b[:r0])).astype(BF16)
                att = lax.dot_general(qh, kh, (((1,), (1,)), ((), ())), preferred_element_type=F32)
                acc = acc + jnp.dot(att.astype(BF16), v[:r0], preferred_element_type=F32)
            pieces.append(acc)
        o = o_inter + jnp.concatenate(pieces, axis=0)
        b_last = b[c_sz - 1:c_sz, :]
        k_dec = (k * jnp.exp(b_last - b)).astype(BF16)
        upd = lax.dot_general(v, k_dec, (((0,), (0,)), ((), ())), preferred_element_type=F32)
        state_ref[hh] = state_t * jnp.exp(b_last) + upd
        y = o * lax.rsqrt(jnp.mean(o * o, axis=-1, keepdims=True) + RMS_EPS) * gn_ref[...]
        g = g_ref[rows, vcols].astype(F32)
        y = y * (g / (1.0 + jnp.exp(-g)))
        o_ref[rows, vcols] = y.astype(o_ref.dtype)

    def chunk_step(c, carry):
        c0 = pl.multiple_of(c * c_sz, c_sz)
        for hh in range(n_heads):
            head_chunk(c0, hh)
        return carry

    lax.fori_loop(0, tokens // c_sz, chunk_step, 0)


def gla_attention(z_qkv, z_gg, z_a, a_blk, w_a2p, b_a, out_norm, batch, seq, heads, chunk=64):
    dk, dv = GLA_DK, GLA_DV
    tokens = _pick(seq, 256, chunk)
    nt = seq // tokens
    nh = math.gcd(heads, 8)
    wk, wv = nh * dk, nh * dv
    k_off = heads * dk // wk
    v_off = 2 * heads * dk // wv
    row = lambda blk_fn: (lambda b, h, t: (b * nt + t, blk_fn(h)))
    return pl.pallas_call(
        functools.partial(_gla_body, chunk=chunk, tokens=tokens, n_heads=nh),
        out_shape=jax.ShapeDtypeStruct((batch * seq, heads * dv), BF16),
        grid=(batch, heads // nh, nt),
        in_specs=[pl.BlockSpec((tokens, wk), row(lambda h: h)),
                  pl.BlockSpec((tokens, wk), row(lambda h: k_off + h)),
                  pl.BlockSpec((tokens, wv), row(lambda h: v_off + h)),
                  pl.BlockSpec((tokens, wv), row(lambda h: h)),
                  pl.BlockSpec((tokens, LANES), row(lambda h: a_blk)),
                  pl.BlockSpec((LANES, wk), lambda b, h, t: (0, h)),
                  pl.BlockSpec((1, wk), lambda b, h, t: (0, h)),
                  pl.BlockSpec((1, dv), lambda b, h, t: (0, 0))],
        out_specs=pl.BlockSpec((tokens, wv), row(lambda h: h)),
        scratch_shapes=[pltpu.VMEM((nh, dv, dk), F32), pltpu.VMEM((tokens, wk), F32)],
        compiler_params=_cparams(("parallel", "parallel", "arbitrary")),
        name="gla_attention",
    )(z_qkv, z_qkv, z_qkv, z_gg, z_a, w_a2p, b_a.reshape(1, -1), out_norm.reshape(1, dv))


def _merge_body(h_ref, y0_ref, y1_ref, y2_ref, wg0_ref, wg1_ref, wg2_ref,
                wb0_ref, wb1_ref, wb2_ref, bias_ref, o_ref):
    h = h_ref[...]
    acc = None
    branches = ((y0_ref, wg0_ref, wb0_ref), (y1_ref, wg1_ref, wb1_ref), (y2_ref, wg2_ref, wb2_ref))
    for b, (y_ref, wg_ref, wb_ref) in enumerate(branches):
        logit = jnp.dot(h, wg_ref[...], preferred_element_type=F32) + bias_ref[b:b + 1, :]
        gate = 1.0 / (1.0 + jnp.exp(-logit))
        term = gate * jnp.dot(y_ref[...], wb_ref[...], preferred_element_type=F32)
        acc = term if acc is None else acc + term
    o_ref[...] = acc.astype(o_ref.dtype)


def merge_branches(h, ys, w_gates, w_brs, bias3):
    m, d = h.shape
    bw = ys[0].shape[1]
    tm = _pick(m, 512, 16)
    tn = _pick(d, 256)
    nj = d // tn
    gate_spec = lambda b: pl.BlockSpec((d, tn), lambda j, i: (0, b * nj + j))
    return pl.pallas_call(
        _merge_body,
        out_shape=jax.ShapeDtypeStruct((m, d), BF16),
        grid=(nj, m // tm),
        in_specs=[pl.BlockSpec((tm, d), lambda j, i: (i, 0))]
        + [pl.BlockSpec((tm, bw), lambda j, i: (i, 0))] * N_BRANCHES
        + [gate_spec(b) for b in range(N_BRANCHES)]
        + [pl.BlockSpec((bw, tn), lambda j, i: (0, j))] * N_BRANCHES
        + [pl.BlockSpec((N_BRANCHES, tn), lambda j, i: (0, j))],
        out_specs=pl.BlockSpec((tm, tn), lambda j, i: (i, j)),
        compiler_params=_cparams(("parallel", "parallel")),
        name="merge_branches",
    )(h, *ys, w_gates, w_gates, w_gates, *w_brs, bias3)


def _pack_bf16_pairs(x):
    c = x.shape[1] // 2
    bits = lax.bitcast_convert_type(x.astype(jnp.bfloat16).astype(F32), jnp.uint32)
    return (bits[:, c:] & jnp.uint32(0xFFFF0000)) | (bits[:, :c] >> 16)


def _unpack_bf16_pairs(p):
    lo = lax.bitcast_convert_type(p << 16, F32).astype(BF16)
    hi = lax.bitcast_convert_type(p & jnp.uint32(0xFFFF0000), F32).astype(BF16)
    return lo, hi


def _router_body(x_ref, g_ref, w_ref, b_ref, h_ref, e_ref, wt_ref, rk_ref, cnt_ref, carry_ref,
                 *, n_exp, n_grp, per_grp):
    i = pl.program_id(0)

    @pl.when(i == 0)
    def _():
        carry_ref[...] = jnp.zeros_like(carry_ref)

    x = x_ref[...]
    tm = x.shape[0]
    hn = x * lax.rsqrt(jnp.mean(x * x, axis=-1, keepdims=True) + RMS_EPS) * g_ref[...]
    h_ref[...] = _pack_bf16_pairs(hn)
    logits = jnp.dot(hn, w_ref[...], precision=HIGHEST, preferred_element_type=F32) + b_ref[...]
    lane = lax.broadcasted_iota(jnp.int32, (tm, LANES), 1)
    big = jnp.int32(LANES)

    def top1(vals):
        mx = jnp.max(vals, axis=-1, keepdims=True)
        idx = jnp.min(jnp.where(vals == mx, lane, big), axis=-1, keepdims=True)
        return mx, idx

    gl = jnp.where((lane >= n_exp) & (lane < n_exp + n_grp), logits, NEG_INF)
    gmax, gidx = top1(gl)
    g_top_p = 1.0 / jnp.sum(jnp.exp(gl - gmax), axis=-1, keepdims=True)
    grp = gidx - n_exp
    el = jnp.where((lane >= grp * per_grp) & (lane < (grp + 1) * per_grp), logits, NEG_INF)
    m1, i1 = top1(el)
    m2, i2 = top1(jnp.where(lane == i1, NEG_INF, el))
    e21 = jnp.exp(m2 - m1)
    w1 = g_top_p / (1.0 + e21)
    w2 = w1 * e21
    e_ref[...] = jnp.where(lane == 0, i1, jnp.where(lane == 1, i2, 0))[:, :8]
    wt_ref[...] = jnp.where(lane == 0, w1, jnp.where(lane == 1, w2, 0.0))[:, :8]

    oh1 = (lane == i1).astype(F32)
    oh2 = (lane == i2).astype(F32)
    ri = lax.broadcasted_iota(jnp.int32, (tm, tm), 0)
    ci = lax.broadcasted_iota(jnp.int32, (tm, tm), 1)
    strict = (ci < ri).astype(BF16)
    cum1 = jnp.dot(strict, oh1.astype(BF16), preferred_element_type=F32)
    cum2 = jnp.dot(strict, oh2.astype(BF16), preferred_element_type=F32)
    tot1 = jnp.sum(oh1, axis=0, keepdims=True)
    tot2 = jnp.sum(oh2, axis=0, keepdims=True)
    carry = carry_ref[...]
    r1 = jnp.sum(oh1 * (cum1 + carry), axis=-1, keepdims=True)
    r2 = jnp.sum(oh2 * (cum2 + carry + tot1), axis=-1, keepdims=True)
    rk_ref[...] = jnp.where(lane == 0, r1, jnp.where(lane == 1, r2, 0.0))[:, :8].astype(jnp.int32)
    carry = carry + tot1 + tot2
    carry_ref[...] = carry
    cnt_ref[...] = jnp.broadcast_to(carry, cnt_ref.shape).astype(jnp.int32)


def moe_router(x, gain, w_r, b_r, n_exp, n_grp):
    m, d = x.shape
    tm = _pick(m, 256, 8)
    row = lambda i: (i, 0)
    const = lambda i: (0, 0)
    return pl.pallas_call(
        functools.partial(_router_body, n_exp=n_exp, n_grp=n_grp, per_grp=n_exp // n_grp),
        out_shape=(jax.ShapeDtypeStruct((m, d // 2), jnp.uint32),
                   jax.ShapeDtypeStruct((m, 8), jnp.int32),
                   jax.ShapeDtypeStruct((m, 8), F32),
                   jax.ShapeDtypeStruct((m, 8), jnp.int32),
                   jax.ShapeDtypeStruct((8, LANES), jnp.int32)),
        grid=(m // tm,),
        in_specs=[pl.BlockSpec((tm, d), row), pl.BlockSpec((1, d), const),
                  pl.BlockSpec((d, LANES), const), pl.BlockSpec((1, LANES), const)],
        out_specs=(pl.BlockSpec((tm, d // 2), row), pl.BlockSpec((tm, 8), row),
                   pl.BlockSpec((tm, 8), row), pl.BlockSpec((tm, 8), row),
                   pl.BlockSpec((8, LANES), const)),
        scratch_shapes=[pltpu.VMEM((1, LANES), F32)],
        compiler_params=_cparams(("arbitrary",)),
        name="moe_router",
    )(x, gain.reshape(1, d), w_r, b_r)


def _dispatch_body(p1_ref, p2_ref, h_ref, xs_in_ref, xs_ref, sem, *, tokens):
    del xs_in_ref
    base = pl.program_id(0) * tokens

    def copies(t):
        src = h_ref.at[pl.ds(t, 1), :]
        return (pltpu.make_async_copy(src, xs_ref.at[pl.ds(p1_ref[base + t], 1), :], sem),
                pltpu.make_async_copy(src, xs_ref.at[pl.ds(p2_ref[base + t], 1), :], sem))

    def start(t, c):
        a, b = copies(t)
        a.start()
        b.start()
        return c

    def wait(t, c):
        a, b = copies(t)
        a.wait()
        b.wait()
        return c

    lax.fori_loop(0, tokens, start, 0, unroll=8)
    lax.fori_loop(0, tokens, wait, 0, unroll=8)


def moe_dispatch(h, pos1, pos2, rows):
    m, d = h.shape
    tokens = _pick(m, 256, 8)
    xs0 = jnp.zeros((rows, d), h.dtype)
    return pl.pallas_call(
        functools.partial(_dispatch_body, tokens=tokens),
        out_shape=jax.ShapeDtypeStruct((rows, d), h.dtype),
        grid_spec=pltpu.PrefetchScalarGridSpec(
            num_scalar_prefetch=2,
            grid=(m // tokens,),
            in_specs=[pl.BlockSpec((tokens, d), lambda i, p1, p2: (i, 0)),
                      pl.BlockSpec(memory_space=pl.ANY)],
            out_specs=pl.BlockSpec(memory_space=pl.ANY),
            scratch_shapes=[pltpu.SemaphoreType.DMA(())]),
        input_output_aliases={3: 0},
        compiler_params=_cparams(("arbitrary",)),
        name="moe_dispatch",
    )(pos1, pos2, h, xs0)


def _experts_up_body(te_ref, ts_ref, tv_ref, x_ref, wg_ref, wu_ref, h_ref):
    del te_ref, ts_ref
    i = pl.program_id(1)
    half = x_ref.shape[1]

    @pl.when(tv_ref[i] > 0)
    def _():
        lo, hi = _unpack_bf16_pairs(x_ref[...])

        def proj(w_ref):
            return (jnp.dot(lo, w_ref[:half, :].astype(BF16), preferred_element_type=F32)
                    + jnp.dot(hi, w_ref[half:, :].astype(BF16), preferred_element_type=F32))

        a = proj(wg_ref)
        u = proj(wu_ref)
        h_ref[...] = ((a / (1.0 + jnp.exp(-a))) * u).astype(h_ref.dtype)

    @pl.when(tv_ref[i] == 0)
    def _():
        h_ref[...] = jnp.zeros_like(h_ref)


def _experts_down_body(te_ref, ts_ref, tv_ref, h_ref, wd_ref, o_ref):
    del te_ref, ts_ref
    i = pl.program_id(1)

    @pl.when(tv_ref[i] > 0)
    def _():
        o_ref[...] = jnp.dot(h_ref[...], wd_ref[...].astype(BF16), preferred_element_type=F32)

    @pl.when(tv_ref[i] == 0)
    def _():
        o_ref[...] = jnp.zeros_like(o_ref)


def moe_experts(xs, tile_expert, tile_src, tile_valid, w_gate, w_up, w_down, layer, tm):
    rows = xs.shape[0]
    d, ff = w_gate.shape[2], w_gate.shape[3]
    n_tiles = rows // tm
    tf = _pick(ff, 384)
    td = _pick(d, 2048)
    prefetch = (tile_expert, tile_src, tile_valid)
    w_map = lambda j, i, te, ts, tv: (layer, te[i], 0, j)
    hid = pl.pallas_call(
        _experts_up_body,
        out_shape=jax.ShapeDtypeStruct((rows, ff), BF16),
        grid_spec=pltpu.PrefetchScalarGridSpec(
            num_scalar_prefetch=3,
            grid=(ff // tf, n_tiles),
            in_specs=[pl.BlockSpec((tm, d // 2), lambda j, i, te, ts, tv: (ts[i], 0)),
                      pl.BlockSpec((None, None, d, tf), w_map),
                      pl.BlockSpec((None, None, d, tf), w_map)],
            out_specs=pl.BlockSpec((tm, tf), lambda j, i, te, ts, tv: (i, j))),
        compiler_params=_cparams(("arbitrary", "arbitrary")),
        name="moe_experts_up",
    )(*prefetch, xs, w_gate, w_up)
    return pl.pallas_call(
        _experts_down_body,
        out_shape=jax.ShapeDtypeStruct((rows, d), F32),
        grid_spec=pltpu.PrefetchScalarGridSpec(
            num_scalar_prefetch=3,
            grid=(d // td, n_tiles),
            in_specs=[pl.BlockSpec((tm, ff), lambda j, i, te, ts, tv: (ts[i], 0)),
                      pl.BlockSpec((None, None, ff, td), w_map)],
            out_specs=pl.BlockSpec((tm, td), lambda j, i, te, ts, tv: (i, j))),
        compiler_params=_cparams(("arbitrary", "arbitrary")),
        name="moe_experts_down",
    )(*prefetch, hid, w_down)


def _combine_body(p1_ref, p2_ref, x_ref, wt_ref, g_ref, ys_ref, *rest, tokens, emit_x):
    if emit_x:
        xo_ref, no_ref, buf_ref, sem = rest
    else:
        no_ref, buf_ref, sem = rest
    base = pl.program_id(0) * tokens

    def copies(t):
        return (pltpu.make_async_copy(ys_ref.at[pl.ds(p1_ref[base + t], 1), :],
                                      buf_ref.at[0, pl.ds(t, 1), :], sem),
                pltpu.make_async_copy(ys_ref.at[pl.ds(p2_ref[base + t], 1), :],
                                      buf_ref.at[1, pl.ds(t, 1), :], sem))

    def start(t, c):
        a, b = copies(t)
        a.start()
        b.start()
        return c

    def wait(t, c):
        a, b = copies(t)
        a.wait()
        b.wait()
        return c

    lax.fori_loop(0, tokens, start, 0, unroll=8)
    lax.fori_loop(0, tokens, wait, 0, unroll=8)
    wt = wt_ref[...]
    x = x_ref[...] + wt[:, 0:1] * buf_ref[0] + wt[:, 1:2] * buf_ref[1]
    if emit_x:
        xo_ref[...] = x
    y = x * lax.rsqrt(jnp.mean(x * x, axis=-1, keepdims=True) + RMS_EPS)
    no_ref[...] = (y * g_ref[...]).astype(no_ref.dtype)


def moe_combine(x, ys, pos1, pos2, wts, gain, norm_dtype, emit_x):
    m, d = x.shape
    tokens = _pick(m, 256, 8)
    row = lambda i, p1, p2: (i, 0)
    out_shape = [jax.ShapeDtypeStruct((m, d), norm_dtype)]
    out_specs = [pl.BlockSpec((tokens, d), row)]
    if emit_x:
        out_shape.insert(0, jax.ShapeDtypeStruct((m, d), F32))
        out_specs.insert(0, pl.BlockSpec((tokens, d), row))
    return pl.pallas_call(
        functools.partial(_combine_body, tokens=tokens, emit_x=emit_x),
        out_shape=tuple(out_shape),
        grid_spec=pltpu.PrefetchScalarGridSpec(
            num_scalar_prefetch=2,
            grid=(m // tokens,),
            in_specs=[pl.BlockSpec((tokens, d), row),
                      pl.BlockSpec((tokens, 8), row),
                      pl.BlockSpec((1, d), lambda i, p1, p2: (0, 0)),
                      pl.BlockSpec(memory_space=pl.ANY)],
            out_specs=tuple(out_specs),
            scratch_shapes=[pltpu.VMEM((2, tokens, d), F32), pltpu.SemaphoreType.DMA(())]),
        compiler_params=_cparams(("arbitrary",)),
        name="moe_combine",
    )(pos1, pos2, x, wts, gain.reshape(1, d), ys)


def _rope_partner(w_rot):
    half = MLA_ROPE_DIM // 2
    return jnp.concatenate([-w_rot[..., half:], w_rot[..., :half]], axis=-1)


def _repack_body(a_ref, o_ref):
    o_ref[...] = a_ref[0].T.astype(o_ref.dtype)


def repack_columns(wt3, layer, col_start, width):
    k = wt3.shape[2]
    assert col_start % 8 == 0 and width % LANES == 0
    tn = _pick(width, 512)
    return pl.pallas_call(
        _repack_body,
        out_shape=jax.ShapeDtypeStruct((k, width), BF16),
        grid=(width // tn,),
        in_specs=[pl.BlockSpec((pl.Element(1), pl.Element(tn), pl.Element(k)),
                               lambda j: (layer, pl.multiple_of(col_start + j * tn, 8), 0))],
        out_specs=pl.BlockSpec((k, tn), lambda j: (0, j)),
        compiler_params=_cparams(("parallel",)),
        name="repack_columns",
    )(wt3)


def _layer_weights(wt3, layer, w_uq, heads_mla):
    d = wt3.shape[2]
    gla_qk = GLA_HEADS * GLA_DK
    gla_w = GLA_HEADS * GLA_DV
    swa_q = SWA_Q_HEADS * SWA_HEAD_DIM
    swa_kv = SWA_KV_HEADS * SWA_HEAD_DIM
    sizes = (MLA_Q_LORA, MLA_KV_LORA, MLA_ROPE_DIM, gla_qk, gla_qk, gla_w, GLA_GATE_RANK, gla_w,
             swa_q, swa_kv, swa_kv, N_BRANCHES * d)
    offs = [int(o) for o in np.concatenate([[0], np.cumsum(sizes)])]
    cols = lambda a, b: wt3[layer, a:b, :]
    zcols = lambda n: jnp.zeros((n, d), wt3.dtype)
    kr = cols(offs[2], offs[3])
    half = MLA_ROPE_DIM // 2
    kr_partner = jnp.concatenate([-kr[half:], kr[:half]], axis=0)
    w_a = repack_columns(wt3, layer, 0, offs[2])
    w_r = jnp.concatenate([kr, zcols(LANES - MLA_ROPE_DIM), kr_partner, zcols(LANES - MLA_ROPE_DIM),
                           cols(offs[6], offs[7]), zcols(LANES - GLA_GATE_RANK)], axis=0).T
    w_qkv = repack_columns(wt3, layer, offs[3], offs[6] - offs[3])
    w_gg = repack_columns(wt3, layer, offs[7], offs[8] - offs[7])
    w_s = repack_columns(wt3, layer, offs[8], offs[11] - offs[8])
    w_gates = repack_columns(wt3, layer, offs[11], offs[12] - offs[11])
    qk = MLA_NOPE_DIM + MLA_ROPE_DIM
    wq = w_uq.reshape(-1, heads_mla, qk)
    pad = jnp.zeros((wq.shape[0], heads_mla, MLA_HEAD_PAD - qk), w_uq.dtype)
    w_q1 = jnp.concatenate([wq, pad], axis=-1).reshape(wq.shape[0], -1).astype(BF16)
    rot = wq[..., MLA_NOPE_DIM:]
    pad2 = jnp.zeros((wq.shape[0], heads_mla, LANES - MLA_ROPE_DIM), w_uq.dtype)
    w_q2 = jnp.concatenate([_rope_partner(rot), pad2], axis=-1).reshape(wq.shape[0], -1).astype(BF16)
    return w_a, w_r, w_qkv, w_gg, w_s, w_gates, w_q1.T, w_q2.T


def _moe_layer(x, gain, w_rg, b_rg, w_re, b_re, w_gate, w_up, w_down, layer, next_gain, norm_dtype,
               emit_x):
    m, d = x.shape
    n_exp, n_grp = w_re.shape[1], w_rg.shape[1]
    w_r = jnp.concatenate([w_re, w_rg, jnp.zeros((d, LANES - n_exp - n_grp), F32)], axis=1)
    b_r = jnp.concatenate([b_re, b_rg, jnp.zeros((LANES - n_exp - n_grp,), F32)]).reshape(1, LANES)
    hn, e_idx, wts, rank, counts = moe_router(x, gain, w_r, b_r, n_exp, n_grp)
    tm = 640 if m >= 2048 else 64
    cnt = counts[0, :n_exp]
    padded = ((cnt + tm - 1) // tm) * tm
    ends = jnp.cumsum(padded)
    offs = ends - padded
    n_tiles = -(-(TOP_K_IN_GROUP * m) // tm) + n_exp
    tile_start = jnp.arange(n_tiles, dtype=jnp.int32) * tm
    tile_valid = (tile_start < ends[-1]).astype(jnp.int32)
    last_tile = jnp.maximum(ends[-1] // tm - 1, 0).astype(jnp.int32)
    tile_src = jnp.minimum(jnp.arange(n_tiles, dtype=jnp.int32), last_tile)
    tile_expert = jnp.sum((ends[None, :] <= (tile_src * tm)[:, None]).astype(jnp.int32), axis=1)
    tile_expert = jnp.minimum(tile_expert, n_exp - 1)
    pos = offs[e_idx[:, :2]].astype(jnp.int32) + rank[:, :2]
    pos1, pos2 = pos[:, 0], pos[:, 1]
    xs = moe_dispatch(hn, pos1, pos2, n_tiles * tm)
    ys = moe_experts(xs, tile_expert, tile_src, tile_valid, w_gate, w_up, w_down, layer, tm)
    return moe_combine(x, ys, pos1, pos2, wts, next_gain, norm_dtype, emit_x)


def kernel(x, positions, attn_norm, w_in, gate_bias, mla_q_norm, mla_w_uq, mla_kv_norm, mla_w_ukv,
           gla_w_a2, gla_b_a, gla_out_norm, swa_sinks, w_br_mla, w_br_gla, w_br_swa, w_out,
           ffn_norm, w_router_group, b_router_group, w_router_expert, b_router_expert,
           w_gate, w_up, w_down, final_norm):
    batch, seq, d = x.shape
    depth = w_in.shape[0]
    m = batch * seq
    heads_mla = mla_w_uq.shape[2] // (MLA_NOPE_DIM + MLA_ROPE_DIM)
    xf = x.reshape(m, d)
    pos = positions.reshape(m, 1).astype(jnp.int32)
    half = MLA_ROPE_DIM // 2
    inv_freq = ROPE_THETA ** (-jnp.arange(0, MLA_ROPE_DIM, 2, dtype=F32) / MLA_ROPE_DIM)
    freq = jnp.concatenate([inv_freq, inv_freq, jnp.zeros((LANES - 2 * half,), F32)]).reshape(1, LANES)

    w_in_t = jnp.swapaxes(w_in, 1, 2)
    h = rmsnorm(xf, attn_norm[0], BF16)
    out = None
    for l in range(depth):
        w_a, w_r, w_qkv, w_gg, w_s, w_gates, w_q1t, w_q2t = _layer_weights(
            w_in_t, l, mla_w_uq[l], heads_mla)
        z_a = matmul(h, w_a, F32, name="in_proj_mla")
        z_r = matmul(h, w_r, F32, name="in_proj_rope_gate")
        z_qkv = matmul(h, w_qkv, BF16, name="in_proj_gla_qkv")
        z_gg = matmul(h, w_gg, BF16, name="in_proj_gla_gate")
        z_s = matmul(h, w_s, BF16, name="in_proj_swa")
        w_kv = mla_w_ukv[l].reshape(-1, heads_mla, MLA_NOPE_DIM + MLA_V_DIM)
        w_kn = w_kv[..., :MLA_NOPE_DIM].reshape(w_kv.shape[0], -1).astype(BF16)
        w_vt = w_kv[..., MLA_NOPE_DIM:].reshape(w_kv.shape[0], -1).T.astype(BF16)
        qt, k, vt = mla_prep(z_a, z_r, pos, mla_q_norm[l], mla_kv_norm[l], freq, w_q1t, w_q2t,
                             w_kn, w_vt, heads_mla)
        y_mla = mla_attention(qt, k, vt, batch, seq, heads_mla)
        w_a2p = jnp.concatenate([gla_w_a2[l], jnp.zeros((LANES - GLA_GATE_RANK, gla_w_a2.shape[2]), F32)], axis=0)
        y_gla = gla_attention(z_qkv, z_gg, z_r, 2, w_a2p, gla_b_a[l], gla_out_norm[l], batch, seq, GLA_HEADS)
        y_swa = swa_attention(z_s, swa_sinks[l], batch, seq, SWA_Q_HEADS, SWA_KV_HEADS)
        w_brs = (w_br_mla[l].astype(BF16), w_br_gla[l].astype(BF16), w_br_swa[l].astype(BF16))
        merged = merge_branches(h, (y_mla, y_gla, y_swa), w_gates, w_brs, gate_bias[l])
        xf = matmul(merged, w_out[l].astype(BF16), F32, residual=xf, name="out_proj")
        last = l == depth - 1
        res = _moe_layer(xf, ffn_norm[l], w_router_group[l], b_router_group[l],
                         w_router_expert[l], b_router_expert[l], w_gate, w_up, w_down, l,
                         final_norm if last else attn_norm[l + 1],
                         F32 if last else BF16, emit_x=not last)
        if last:
            out = res[0]
        else:
            xf, h = res
    return out.reshape(batch, seq, d)
```

```python
import functools
import math

import numpy as np
import jax
import jax.numpy as jnp
from jax import lax
from jax.experimental import pallas as pl
from jax.experimental.pallas import tpu as pltpu

F32 = jnp.float32
BF16 = jnp.bfloat16
HIGHEST = lax.Precision.HIGHEST

RMS_EPS = 1e-6
NEG_INF = -1e30
MLA_HEADS = 16
MLA_Q_LORA = 1024
MLA_KV_LORA = 512
MLA_NOPE_DIM = 128
MLA_ROPE_DIM = 64
MLA_V_DIM = 128
ROPE_THETA = 10000.0
GLA_HEADS = 8
GLA_DK = 128
GLA_DV = 256
GLA_GATE_RANK = 16
GLA_GATE_TAU = 16.0
SWA_Q_HEADS = 32
SWA_KV_HEADS = 4
SWA_HEAD_DIM = 64
SWA_WINDOW = 128
N_BRANCHES = 3
N_GROUPS = 4
EXPERTS_PER_GROUP = 8
TOP_K_IN_GROUP = 2

LANES = 128
MXU_WIDTH = 256
MLA_HEAD_PAD = 256
GLA_SUB = 16
VMEM_LIMIT = 60 * 1024 * 1024


def _cparams(sem, vmem=VMEM_LIMIT):
    return pltpu.CompilerParams(dimension_semantics=sem, vmem_limit_bytes=vmem)


def _pick(n, cap, mult=LANES):
    best = None
    for t in range(mult, min(n, cap) + 1, mult):
        if n % t == 0:
            best = t
    assert best is not None, (n, cap, mult)
    return best


def _rmsnorm_body(x_ref, g_ref, o_ref):
    x = x_ref[...]
    y = x * lax.rsqrt(jnp.mean(x * x, axis=-1, keepdims=True) + RMS_EPS)
    o_ref[...] = (y * g_ref[...]).astype(o_ref.dtype)


def rmsnorm(x, gain, out_dtype):
    m, d = x.shape
    tm = _pick(m, 512, 8)
    return pl.pallas_call(
        _rmsnorm_body,
        out_shape=jax.ShapeDtypeStruct((m, d), out_dtype),
        grid=(m // tm,),
        in_specs=[pl.BlockSpec((tm, d), lambda i: (i, 0)),
                  pl.BlockSpec((1, d), lambda i: (0, 0))],
        out_specs=pl.BlockSpec((tm, d), lambda i: (i, 0)),
        compiler_params=_cparams(("parallel",)),
        name="rmsnorm",
    )(x, gain.reshape(1, d))


def _matmul_body(a_ref, w_ref, o_ref):
    w = w_ref[...].astype(BF16)
    o_ref[...] = jnp.dot(a_ref[...], w, preferred_element_type=F32).astype(o_ref.dtype)


def _matmul_res_body(a_ref, w_ref, r_ref, o_ref):
    acc = jnp.dot(a_ref[...], w_ref[...].astype(BF16), preferred_element_type=F32)
    o_ref[...] = (r_ref[...] + acc).astype(o_ref.dtype)


def matmul(a, w, out_dtype, residual=None, tm_cap=512, tn_cap=1024, name="matmul"):
    m, k = a.shape
    n = w.shape[1]
    tm = _pick(m, tm_cap, 16)
    tn = _pick(n, tn_cap, MXU_WIDTH if n % MXU_WIDTH == 0 else LANES)
    in_specs = [pl.BlockSpec((tm, k), lambda j, i: (i, 0)),
                pl.BlockSpec((k, tn), lambda j, i: (0, j))]
    args = [a, w]
    body = _matmul_body
    if residual is not None:
        in_specs.append(pl.BlockSpec((tm, tn), lambda j, i: (i, j)))
        args.append(residual)
        body = _matmul_res_body
    return pl.pallas_call(
        body,
        out_shape=jax.ShapeDtypeStruct((m, n), out_dtype),
        grid=(n // tn, m // tm),
        in_specs=in_specs,
        out_specs=pl.BlockSpec((tm, tn), lambda j, i: (i, j)),
        compiler_params=_cparams(("parallel", "parallel")),
        name=name,
    )(*args)


def _rope_tables(pos_ref, freq_ref):
    ang = pos_ref[...].astype(F32) * freq_ref[...]
    return jnp.cos(ang), jnp.sin(ang)


def _mla_q_body(c_ref, pos_ref, g_ref, freq_ref, w1t_ref, w2t_ref, qt_ref, *, heads, scale):
    c = c_ref[...]
    cn = (c * lax.rsqrt(jnp.mean(c * c, axis=-1, keepdims=True) + RMS_EPS) * g_ref[...]).astype(BF16)
    nt = (((1,), (1,)), ((), ()))
    q = lax.dot_general(w1t_ref[...], cn, nt, preferred_element_type=F32)
    qp = lax.dot_general(w2t_ref[...], cn, nt, preferred_element_type=F32)
    ang = freq_ref[...] * pos_ref[...].astype(F32)
    cs, sn = jnp.cos(ang), jnp.sin(ang)
    for h in range(heads):
        a = h * MLA_HEAD_PAD
        qt_ref[a:a + LANES, :] = (q[a:a + LANES] * scale).astype(BF16)
        rot = q[a + LANES:a + 2 * LANES] * cs + qp[h * LANES:(h + 1) * LANES] * sn
        qt_ref[a + LANES:a + 2 * LANES, :] = (rot * scale).astype(BF16)


def _mla_kv_body(c_ref, kr_ref, krp_ref, pos_ref, g_ref, freq_ref, wk_ref, wvt_ref, k_ref, vt_ref,
                 *, heads):
    c = c_ref[...]
    cn = (c * lax.rsqrt(jnp.mean(c * c, axis=-1, keepdims=True) + RMS_EPS) * g_ref[...]).astype(BF16)
    kn = jnp.dot(cn, wk_ref[...], preferred_element_type=F32)
    vt_ref[...] = lax.dot_general(wvt_ref[...], cn, (((1,), (1,)), ((), ())),
                                  preferred_element_type=F32).astype(BF16)
    cs, sn = _rope_tables(pos_ref, freq_ref)
    krot = (kr_ref[...] * cs + krp_ref[...] * sn).astype(BF16)
    for h in range(heads):
        a = h * MLA_HEAD_PAD
        k_ref[:, a:a + LANES] = kn[:, h * LANES:(h + 1) * LANES].astype(BF16)
        k_ref[:, a + LANES:a + 2 * LANES] = krot


def mla_prep(z_a, z_r, pos, q_norm, kv_norm, freq, w_q1t, w_q2t, w_kn, w_vt, heads):
    m = z_a.shape[0]
    ql, kvl = w_q1t.shape[1], w_kn.shape[0]
    tm = _pick(m, 256, LANES)
    scale = (MLA_NOPE_DIM + MLA_ROPE_DIM) ** -0.5
    row = lambda i: (i, 0)
    const = lambda i: (0, 0)
    qt = pl.pallas_call(
        functools.partial(_mla_q_body, heads=heads, scale=scale),
        out_shape=jax.ShapeDtypeStruct((heads * MLA_HEAD_PAD, m), BF16),
        grid=(m // tm,),
        in_specs=[pl.BlockSpec((tm, ql), row),
                  pl.BlockSpec((1, tm), lambda i: (0, i)),
                  pl.BlockSpec((1, ql), const),
                  pl.BlockSpec((LANES, 1), const),
                  pl.BlockSpec(w_q1t.shape, const),
                  pl.BlockSpec(w_q2t.shape, const)],
        out_specs=pl.BlockSpec((heads * MLA_HEAD_PAD, tm), lambda i: (0, i)),
        compiler_params=_cparams(("parallel",)),
        name="mla_q_prep",
    )(z_a, pos.reshape(1, m), q_norm.reshape(1, ql), freq.reshape(LANES, 1), w_q1t, w_q2t)
    k, vt = pl.pallas_call(
        functools.partial(_mla_kv_body, heads=heads),
        out_shape=(jax.ShapeDtypeStruct((m, heads * MLA_HEAD_PAD), BF16),
                   jax.ShapeDtypeStruct((heads * MLA_V_DIM, m), BF16)),
        grid=(m // tm,),
        in_specs=[pl.BlockSpec((tm, kvl), lambda i: (i, ql // kvl)),
                  pl.BlockSpec((tm, LANES), lambda i: (i, 0)),
                  pl.BlockSpec((tm, LANES), lambda i: (i, 1)),
                  pl.BlockSpec((tm, 1), row),
                  pl.BlockSpec((1, kvl), const),
                  pl.BlockSpec((1, LANES), const),
                  pl.BlockSpec(w_kn.shape, const),
                  pl.BlockSpec(w_vt.shape, const)],
        out_specs=(pl.BlockSpec((tm, heads * MLA_HEAD_PAD), row),
                   pl.BlockSpec((heads * MLA_V_DIM, tm), lambda i: (0, i))),
        compiler_params=_cparams(("parallel",)),
        name="mla_kv_prep",
    )(z_a, z_r, z_r, pos, kv_norm.reshape(1, kvl), freq, w_kn, w_vt)
    return qt, k, vt


def _mla_attn_body(qt_ref, k_ref, vt_ref, o_ref, m_ref, l_ref, acc_ref, *, tq, sub):
    del sub
    qi = pl.program_id(2)
    n_heads = m_ref.shape[0]
    hd, vd = MLA_HEAD_PAD, MLA_V_DIM
    m_ref[...] = jnp.full_like(m_ref, NEG_INF)
    l_ref[...] = jnp.zeros_like(l_ref)
    acc_ref[...] = jnp.zeros_like(acc_ref)

    def scores(j, h):
        k0 = pl.multiple_of(j * tq, tq)
        return jnp.dot(k_ref[pl.ds(k0, tq), h * hd:(h + 1) * hd], qt_ref[h * hd:(h + 1) * hd, :],
                       preferred_element_type=F32)

    def accumulate(j, h, st):
        k0 = pl.multiple_of(j * tq, tq)
        m_old = m_ref[h]
        m_new = jnp.maximum(m_old, jnp.max(st, axis=0, keepdims=True))
        alpha = jnp.exp(m_old - m_new)
        p = jnp.exp(st - m_new)
        l_ref[h] = alpha * l_ref[h] + jnp.sum(p, axis=0, keepdims=True)
        acc_ref[h] = alpha * acc_ref[h] + jnp.dot(vt_ref[h * vd:(h + 1) * vd, pl.ds(k0, tq)],
                                                  p.astype(BF16), preferred_element_type=F32)
        m_ref[h] = m_new

    def full_tile(j, sts):
        nxt = tuple(scores(j + 1, h) for h in range(n_heads))
        for h in range(n_heads):
            accumulate(j, h, sts[h])
        return nxt

    sts = lax.fori_loop(0, qi, full_tile, tuple(scores(0, h) for h in range(n_heads)))
    tri = (lax.broadcasted_iota(jnp.int32, (tq, tq), 0)
           <= lax.broadcasted_iota(jnp.int32, (tq, tq), 1))
    for h in range(n_heads):
        accumulate(qi, h, jnp.where(tri, sts[h], NEG_INF))
        o_ref[:, h * vd:(h + 1) * vd] = (acc_ref[h] / l_ref[h]).T.astype(o_ref.dtype)


def mla_attention(qt, k, vt, batch, seq, heads):
    tq = _pick(seq, 512, LANES)
    sub = tq // 2
    nq = seq // tq
    nh = math.gcd(heads, 2)
    return pl.pallas_call(
        functools.partial(_mla_attn_body, tq=tq, sub=sub),
        out_shape=jax.ShapeDtypeStruct((batch * seq, heads * MLA_V_DIM), BF16),
        grid=(batch, heads // nh, nq),
        in_specs=[pl.BlockSpec((nh * MLA_HEAD_PAD, tq), lambda b, h, qi: (h, b * nq + qi)),
                  pl.BlockSpec((seq, nh * MLA_HEAD_PAD), lambda b, h, qi: (b, h)),
                  pl.BlockSpec((nh * MLA_V_DIM, seq), lambda b, h, qi: (h, b))],
        out_specs=pl.BlockSpec((tq, nh * MLA_V_DIM), lambda b, h, qi: (b * nq + qi, h)),
        scratch_shapes=[pltpu.VMEM((nh, 1, tq), F32), pltpu.VMEM((nh, 1, tq), F32),
                        pltpu.VMEM((nh, MLA_V_DIM, tq), F32)],
        compiler_params=_cparams(("parallel", "parallel", "parallel")),
        name="mla_attention",
    )(qt, k, vt)


def _swa_body(sink_ref, q_ref, kp_ref, kc_ref, vp_ref, vc_ref, o_ref, *, group):
    w = SWA_WINDOW
    n = pl.program_id(1)
    scale = SWA_HEAD_DIM ** -0.5
    lane = lax.broadcasted_iota(jnp.int32, (2 * w, LANES), 1)
    lo = lane < SWA_HEAD_DIM
    qlane_lo = lax.broadcasted_iota(jnp.int32, (w, LANES), 1) < SWA_HEAD_DIM
    ri = lax.broadcasted_iota(jnp.int32, (2 * w, 2 * w), 0) % w
    cj = lax.broadcasted_iota(jnp.int32, (2 * w, 2 * w), 1)
    valid = (cj > ri) & (cj <= ri + w) & ((cj >= w) | (n > 0))
    row_first = lax.broadcasted_iota(jnp.int32, (2 * w, 1), 0) < w
    kfull = jnp.concatenate([kp_ref[...], kc_ref[...]], axis=0).astype(F32)
    vfull = jnp.concatenate([vp_ref[...], vc_ref[...]], axis=0).astype(F32)
    n_chunks = kfull.shape[1] // LANES
    for c in range(n_chunks):
        kc = kfull[:, c * LANES:(c + 1) * LANES]
        vc = vfull[:, c * LANES:(c + 1) * LANES]
        kr = pltpu.roll(kc, SWA_HEAD_DIM, axis=1)
        vr = pltpu.roll(vc, SWA_HEAD_DIM, axis=1)
        for sub in range(2):
            hk = 2 * c + sub
            k2 = (jnp.where(lo, kc, kr) if sub == 0 else jnp.where(lo, kr, kc)).astype(BF16)
            v2 = (jnp.where(lo, vc, vr) if sub == 0 else jnp.where(lo, vr, vc)).astype(BF16)
            for j in range(group // 2):
                chunk = hk * (group // 2) + j
                hq = hk * group + 2 * j
                qp = q_ref[:, chunk * LANES:(chunk + 1) * LANES].astype(F32)
                qa = jnp.where(qlane_lo, qp, 0.0)
                qb = jnp.where(qlane_lo, 0.0, qp)
                qs = jnp.concatenate([qa, qb], axis=0).astype(BF16)
                s = lax.dot_general(qs, k2, (((1,), (1,)), ((), ())),
                                    preferred_element_type=F32) * scale
                s = jnp.where(valid, s, NEG_INF)
                sink = jnp.where(row_first, sink_ref[hq], sink_ref[hq + 1])
                mx = jnp.maximum(jnp.max(s, axis=-1, keepdims=True), sink)
                p = jnp.exp(s - mx)
                denom = jnp.sum(p, axis=-1, keepdims=True) + jnp.exp(sink - mx)
                o2 = jnp.dot((p / denom).astype(BF16), v2, preferred_element_type=F32)
                o_ref[:, chunk * LANES:(chunk + 1) * LANES] = jnp.where(
                    qlane_lo, o2[:w], o2[w:]).astype(o_ref.dtype)


def swa_attention(z_s, sinks, batch, seq, q_heads, kv_heads):
    w = SWA_WINDOW
    nb = seq // w
    qw = q_heads * SWA_HEAD_DIM
    kw = kv_heads * SWA_HEAD_DIM
    assert qw % kw == 0 and kw % LANES == 0 and (q_heads // kv_heads) % 2 == 0
    kb = qw // kw
    cur = lambda blk: (lambda b, n, s: (b * nb + n, blk))
    prev = lambda blk: (lambda b, n, s: (b * nb + jnp.maximum(n - 1, 0), blk))
    return pl.pallas_call(
        functools.partial(_swa_body, group=q_heads // kv_heads),
        out_shape=jax.ShapeDtypeStruct((batch * seq, qw), BF16),
        grid_spec=pltpu.PrefetchScalarGridSpec(
            num_scalar_prefetch=1,
            grid=(batch, nb),
            in_specs=[pl.BlockSpec((w, qw), cur(0)),
                      pl.BlockSpec((w, kw), prev(kb)),
                      pl.BlockSpec((w, kw), cur(kb)),
                      pl.BlockSpec((w, kw), prev(kb + 1)),
                      pl.BlockSpec((w, kw), cur(kb + 1))],
            out_specs=pl.BlockSpec((w, qw), cur(0))),
        compiler_params=_cparams(("parallel", "arbitrary")),
        name="swa_attention",
    )(sinks, z_s, z_s, z_s, z_s, z_s)


def _gla_body(q_ref, k_ref, v_ref, g_ref, a_ref, wa_ref, ba_ref, gn_ref, o_ref,
              state_ref, b_ref, *, chunk, tokens, n_heads):
    c_sz = chunk
    dk, dv = GLA_DK, GLA_DV
    t = pl.program_id(2)

    @pl.when(t == 0)
    def _():
        state_ref[...] = jnp.zeros_like(state_ref)

    z = jnp.dot(a_ref[...], wa_ref[...], precision=HIGHEST, preferred_element_type=F32) + ba_ref[...]
    log_a = (jnp.minimum(z, 0.0) - jnp.log(1.0 + jnp.exp(-jnp.abs(z)))) * (1.0 / GLA_GATE_TAU)
    ri = lax.broadcasted_iota(jnp.int32, (tokens, tokens), 0)
    ci = lax.broadcasted_iota(jnp.int32, (tokens, tokens), 1)
    tri = ((ci <= ri) & (ci // c_sz == ri // c_sz)).astype(BF16)
    hi = log_a.astype(BF16)
    r1 = log_a - hi.astype(F32)
    mid = r1.astype(BF16)
    lo = (r1 - mid.astype(F32)).astype(BF16)
    b_ref[...] = (jnp.dot(tri, hi, preferred_element_type=F32)
                  + jnp.dot(tri, mid, preferred_element_type=F32)
                  + jnp.dot(tri, lo, preferred_element_type=F32))

    qscale = GLA_DK ** -0.5
    sub = GLA_SUB
    n_sub = c_sz // sub
    rowid = lax.broadcasted_iota(jnp.int32, (sub, 1), 0)

    def head_chunk(c0, hh):
        rows = pl.ds(c0, c_sz)
        kcols = slice(hh * dk, (hh + 1) * dk)
        vcols = slice(hh * dv, (hh + 1) * dv)
        b = b_ref[rows, kcols]
        q = q_ref[rows, kcols].astype(F32) * qscale
        k = k_ref[rows, kcols].astype(F32)
        v = v_ref[rows, vcols]
        vf = v.astype(F32)
        state_t = state_ref[hh]
        o_inter = lax.dot_general((q * jnp.exp(b)).astype(BF16), state_t.astype(BF16),
                                  (((1,), (1,)), ((), ())), preferred_element_type=F32)
        pieces = []
        for i in range(n_sub):
            r0 = i * sub
            b_i, q_i, k_i, v_i = b[r0:r0 + sub], q[r0:r0 + sub], k[r0:r0 + sub], vf[r0:r0 + sub]
            acc = jnp.zeros((sub, dv), F32)
            for j in range(sub):
                wgt = jnp.exp(b_i - b_i[j:j + 1, :])
                col = jnp.sum(q_i * k_i[j:j + 1, :] * wgt, axis=-1, keepdims=True)
                col = jnp.where(rowid >= j, col, 0.0)
                acc = acc + col * v_i[j:j + 1, :]
            if i > 0:
                ref_b = b[r0 - 1:r0, :]
                qh = (q_i * jnp.exp(b_i - ref_b)).astype(BF16)
                kh = (k[:r0] * jnp.exp(ref_b - b[:r0])).astype(BF16)
                att = lax.dot_general(qh, kh, (((1,), (1,)), ((), ())), preferred_element_type=F32)
                acc = acc + jnp.dot(att.astype(BF16), v[:r0], preferred_element_type=F32)
            pieces.append(acc)
        o = o_inter + jnp.concatenate(pieces, axis=0)
        b_last = b[c_sz - 1:c_sz, :]
        k_dec = (k * jnp.exp(b_last - b)).astype(BF16)
        upd = lax.dot_general(v, k_dec, (((0,), (0,)), ((), ())), preferred_element_type=F32)
        state_ref[hh] = state_t * jnp.exp(b_last) + upd
        y = o * lax.rsqrt(jnp.mean(o * o, axis=-1, keepdims=True) + RMS_EPS) * gn_ref[...]
        g = g_ref[rows, vcols].astype(F32)
        y = y * (g / (1.0 + jnp.exp(-g)))
        o_ref[rows, vcols] = y.astype(o_ref.dtype)

    def chunk_step(c, carry):
        c0 = pl.multiple_of(c * c_sz, c_sz)
        for hh in range(n_heads):
            head_chunk(c0, hh)
        return carry

    lax.fori_loop(0, tokens // c_sz, chunk_step, 0)


def gla_attention(z_qkv, z_gg, z_a, a_blk, w_a2p, b_a, out_norm, batch, seq, heads, chunk=64):
    dk, dv = GLA_DK, GLA_DV
    tokens = _pick(seq, 256, chunk)
    nt = seq // tokens
    nh = math.gcd(heads, 8)
    wk, wv = nh * dk, nh * dv
    k_off = heads * dk // wk
    v_off = 2 * heads * dk // wv
    row = lambda blk_fn: (lambda b, h, t: (b * nt + t, blk_fn(h)))
    return pl.pallas_call(
        functools.partial(_gla_body, chunk=chunk, tokens=tokens, n_heads=nh),
        out_shape=jax.ShapeDtypeStruct((batch * seq, heads * dv), BF16),
        grid=(batch, heads // nh, nt),
        in_specs=[pl.BlockSpec((tokens, wk), row(lambda h: h)),
                  pl.BlockSpec((tokens, wk), row(lambda h: k_off + h)),
                  pl.BlockSpec((tokens, wv), row(lambda h: v_off + h)),
                  pl.BlockSpec((tokens, wv), row(lambda h: h)),
                  pl.BlockSpec((tokens, LANES), row(lambda h: a_blk)),
                  pl.BlockSpec((LANES, wk), lambda b, h, t: (0, h)),
                  pl.BlockSpec((1, wk), lambda b, h, t: (0, h)),
                  pl.BlockSpec((1, dv), lambda b, h, t: (0, 0))],
        out_specs=pl.BlockSpec((tokens, wv), row(lambda h: h)),
        scratch_shapes=[pltpu.VMEM((nh, dv, dk), F32), pltpu.VMEM((tokens, wk), F32)],
        compiler_params=_cparams(("parallel", "parallel", "arbitrary")),
        name="gla_attention",
    )(z_qkv, z_qkv, z_qkv, z_gg, z_a, w_a2p, b_a.reshape(1, -1), out_norm.reshape(1, dv))


def _merge_body(h_ref, y0_ref, y1_ref, y2_ref, wg0_ref, wg1_ref, wg2_ref,
                wb0_ref, wb1_ref, wb2_ref, bias_ref, o_ref):
    h = h_ref[...]
    acc = None
    branches = ((y0_ref, wg0_ref, wb0_ref), (y1_ref, wg1_ref, wb1_ref), (y2_ref, wg2_ref, wb2_ref))
    for b, (y_ref, wg_ref, wb_ref) in enumerate(branches):
        logit = jnp.dot(h, wg_ref[...], preferred_element_type=F32) + bias_ref[b:b + 1, :]
        gate = 1.0 / (1.0 + jnp.exp(-logit))
        term = gate * jnp.dot(y_ref[...], wb_ref[...], preferred_element_type=F32)
        acc = term if acc is None else acc + term
    o_ref[...] = acc.astype(o_ref.dtype)


def merge_branches(h, ys, w_gates, w_brs, bias3):
    m, d = h.shape
    bw = ys[0].shape[1]
    tm = _pick(m, 512, 16)
    tn = _pick(d, 256)
    nj = d // tn
    gate_spec = lambda b: pl.BlockSpec((d, tn), lambda j, i: (0, b * nj + j))
    return pl.pallas_call(
        _merge_body,
        out_shape=jax.ShapeDtypeStruct((m, d), BF16),
        grid=(nj, m // tm),
        in_specs=[pl.BlockSpec((tm, d), lambda j, i: (i, 0))]
        + [pl.BlockSpec((tm, bw), lambda j, i: (i, 0))] * N_BRANCHES
        + [gate_spec(b) for b in range(N_BRANCHES)]
        + [pl.BlockSpec((bw, tn), lambda j, i: (0, j))] * N_BRANCHES
        + [pl.BlockSpec((N_BRANCHES, tn), lambda j, i: (0, j))],
        out_specs=pl.BlockSpec((tm, tn), lambda j, i: (i, j)),
        compiler_params=_cparams(("parallel", "parallel")),
        name="merge_branches",
    )(h, *ys, w_gates, w_gates, w_gates, *w_brs, bias3)


def _router_body(x_ref, g_ref, w_ref, b_ref, h_ref, e_ref, wt_ref, rk_ref, cnt_ref, carry_ref,
                 *, n_exp, n_grp, per_grp):
    i = pl.program_id(0)

    @pl.when(i == 0)
    def _():
        carry_ref[...] = jnp.zeros_like(carry_ref)

    x = x_ref[...]
    tm = x.shape[0]
    hn = x * lax.rsqrt(jnp.mean(x * x, axis=-1, keepdims=True) + RMS_EPS) * g_ref[...]
    h_ref[...] = hn
    logits = jnp.dot(hn, w_ref[...], precision=HIGHEST, preferred_element_type=F32) + b_ref[...]
    lane = lax.broadcasted_iota(jnp.int32, (tm, LANES), 1)
    big = jnp.int32(LANES)

    def top1(vals):
        mx = jnp.max(vals, axis=-1, keepdims=True)
        idx = jnp.min(jnp.where(vals == mx, lane, big), axis=-1, keepdims=True)
        return mx, idx

    gl = jnp.where((lane >= n_exp) & (lane < n_exp + n_grp), logits, NEG_INF)
    gmax, gidx = top1(gl)
    g_top_p = 1.0 / jnp.sum(jnp.exp(gl - gmax), axis=-1, keepdims=True)
    grp = gidx - n_exp
    el = jnp.where((lane >= grp * per_grp) & (lane < (grp + 1) * per_grp), logits, NEG_INF)
    m1, i1 = top1(el)
    m2, i2 = top1(jnp.where(lane == i1, NEG_INF, el))
    e21 = jnp.exp(m2 - m1)
    w1 = g_top_p / (1.0 + e21)
    w2 = w1 * e21
    e_ref[...] = jnp.where(lane == 0, i1, jnp.where(lane == 1, i2, 0))[:, :8]
    wt_ref[...] = jnp.where(lane == 0, w1, jnp.where(lane == 1, w2, 0.0))[:, :8]

    oh1 = (lane == i1).astype(F32)
    oh2 = (lane == i2).astype(F32)
    ri = lax.broadcasted_iota(jnp.int32, (tm, tm), 0)
    ci = lax.broadcasted_iota(jnp.int32, (tm, tm), 1)
    strict = (ci < ri).astype(BF16)
    cum1 = jnp.dot(strict, oh1.astype(BF16), preferred_element_type=F32)
    cum2 = jnp.dot(strict, oh2.astype(BF16), preferred_element_type=F32)
    tot1 = jnp.sum(oh1, axis=0, keepdims=True)
    tot2 = jnp.sum(oh2, axis=0, keepdims=True)
    carry = carry_ref[...]
    r1 = jnp.sum(oh1 * (cum1 + carry), axis=-1, keepdims=True)
    r2 = jnp.sum(oh2 * (cum2 + carry + tot1), axis=-1, keepdims=True)
    rk_ref[...] = jnp.where(lane == 0, r1, jnp.where(lane == 1, r2, 0.0))[:, :8].astype(jnp.int32)
    carry = carry + tot1 + tot2
    carry_ref[...] = carry
    cnt_ref[...] = jnp.broadcast_to(carry, cnt_ref.shape).astype(jnp.int32)


def moe_router(x, gain, w_r, b_r, n_exp, n_grp):
    m, d = x.shape
    tm = _pick(m, 256, 8)
    row = lambda i: (i, 0)
    const = lambda i: (0, 0)
    return pl.pallas_call(
        functools.partial(_router_body, n_exp=n_exp, n_grp=n_grp, per_grp=n_exp // n_grp),
        out_shape=(jax.ShapeDtypeStruct((m, d), F32),
                   jax.ShapeDtypeStruct((m, 8), jnp.int32),
                   jax.ShapeDtypeStruct((m, 8), F32),
                   jax.ShapeDtypeStruct((m, 8), jnp.int32),
                   jax.ShapeDtypeStruct((8, LANES), jnp.int32)),
        grid=(m // tm,),
        in_specs=[pl.BlockSpec((tm, d), row), pl.BlockSpec((1, d), const),
                  pl.BlockSpec((d, LANES), const), pl.BlockSpec((1, LANES), const)],
        out_specs=(pl.BlockSpec((tm, d), row), pl.BlockSpec((tm, 8), row),
                   pl.BlockSpec((tm, 8), row), pl.BlockSpec((tm, 8), row),
                   pl.BlockSpec((8, LANES), const)),
        scratch_shapes=[pltpu.VMEM((1, LANES), F32)],
        compiler_params=_cparams(("arbitrary",)),
        name="moe_router",
    )(x, gain.reshape(1, d), w_r, b_r)


def _dispatch_body(p1_ref, p2_ref, h_ref, xs_in_ref, xs_ref, sem, *, tokens):
    del xs_in_ref
    base = pl.program_id(0) * tokens

    def copies(t):
        src = h_ref.at[pl.ds(t, 1), :]
        return (pltpu.make_async_copy(src, xs_ref.at[pl.ds(p1_ref[base + t], 1), :], sem),
                pltpu.make_async_copy(src, xs_ref.at[pl.ds(p2_ref[base + t], 1), :], sem))

    def start(t, c):
        a, b = copies(t)
        a.start()
        b.start()
        return c

    def wait(t, c):
        a, b = copies(t)
        a.wait()
        b.wait()
        return c

    lax.fori_loop(0, tokens, start, 0, unroll=8)
    lax.fori_loop(0, tokens, wait, 0, unroll=8)


def moe_dispatch(h, pos1, pos2, rows):
    m, d = h.shape
    tokens = _pick(m, 256, 8)
    xs0 = jnp.zeros((rows, d), h.dtype)
    return pl.pallas_call(
        functools.partial(_dispatch_body, tokens=tokens),
        out_shape=jax.ShapeDtypeStruct((rows, d), h.dtype),
        grid_spec=pltpu.PrefetchScalarGridSpec(
            num_scalar_prefetch=2,
            grid=(m // tokens,),
            in_specs=[pl.BlockSpec((tokens, d), lambda i, p1, p2: (i, 0)),
                      pl.BlockSpec(memory_space=pl.ANY)],
            out_specs=pl.BlockSpec(memory_space=pl.ANY),
            scratch_shapes=[pltpu.SemaphoreType.DMA(())]),
        input_output_aliases={3: 0},
        compiler_params=_cparams(("arbitrary",)),
        name="moe_dispatch",
    )(pos1, pos2, h, xs0)


def _experts_up_body(te_ref, ts_ref, tv_ref, x_ref, wg_ref, wu_ref, h_ref):
    del te_ref, ts_ref
    i = pl.program_id(1)

    @pl.when(tv_ref[i] > 0)
    def _():
        x = x_ref[...].astype(BF16)
        a = jnp.dot(x, wg_ref[...].astype(BF16), preferred_element_type=F32)
        u = jnp.dot(x, wu_ref[...].astype(BF16), preferred_element_type=F32)
        h_ref[...] = ((a / (1.0 + jnp.exp(-a))) * u).astype(h_ref.dtype)

    @pl.when(tv_ref[i] == 0)
    def _():
        h_ref[...] = jnp.zeros_like(h_ref)


def _experts_down_body(te_ref, ts_ref, tv_ref, h_ref, wd_ref, o_ref):
    del te_ref, ts_ref
    i = pl.program_id(1)

    @pl.when(tv_ref[i] > 0)
    def _():
        o_ref[...] = jnp.dot(h_ref[...], wd_ref[...].astype(BF16), preferred_element_type=F32)

    @pl.when(tv_ref[i] == 0)
    def _():
        o_ref[...] = jnp.zeros_like(o_ref)


def moe_experts(xs, tile_expert, tile_src, tile_valid, w_gate, w_up, w_down, layer, tm):
    rows = xs.shape[0]
    d, ff = w_gate.shape[2], w_gate.shape[3]
    n_tiles = rows // tm
    tf = _pick(ff, 384)
    td = _pick(d, 2048)
    prefetch = (tile_expert, tile_src, tile_valid)
    w_map = lambda j, i, te, ts, tv: (layer, te[i], 0, j)
    hid = pl.pallas_call(
        _experts_up_body,
        out_shape=jax.ShapeDtypeStruct((rows, ff), BF16),
        grid_spec=pltpu.PrefetchScalarGridSpec(
            num_scalar_prefetch=3,
            grid=(ff // tf, n_tiles),
            in_specs=[pl.BlockSpec((tm, d), lambda j, i, te, ts, tv: (ts[i], 0)),
                      pl.BlockSpec((None, None, d, tf), w_map),
                      pl.BlockSpec((None, None, d, tf), w_map)],
            out_specs=pl.BlockSpec((tm, tf), lambda j, i, te, ts, tv: (i, j))),
        compiler_params=_cparams(("arbitrary", "arbitrary")),
        name="moe_experts_up",
    )(*prefetch, xs, w_gate, w_up)
    return pl.pallas_call(
        _experts_down_body,
        out_shape=jax.ShapeDtypeStruct((rows, d), F32),
        grid_spec=pltpu.PrefetchScalarGridSpec(
            num_scalar_prefetch=3,
            grid=(d // td, n_tiles),
            in_specs=[pl.BlockSpec((tm, ff), lambda j, i, te, ts, tv: (ts[i], 0)),
                      pl.BlockSpec((None, None, ff, td), w_map)],
            out_specs=pl.BlockSpec((tm, td), lambda j, i, te, ts, tv: (i, j))),
        compiler_params=_cparams(("arbitrary", "arbitrary")),
        name="moe_experts_down",
    )(*prefetch, hid, w_down)


def _combine_body(p1_ref, p2_ref, x_ref, wt_ref, g_ref, ys_ref, *rest, tokens, emit_x):
    if emit_x:
        xo_ref, no_ref, buf_ref, sem = rest
    else:
        no_ref, buf_ref, sem = rest
    base = pl.program_id(0) * tokens

    def copies(t):
        return (pltpu.make_async_copy(ys_ref.at[pl.ds(p1_ref[base + t], 1), :],
                                      buf_ref.at[0, pl.ds(t, 1), :], sem),
                pltpu.make_async_copy(ys_ref.at[pl.ds(p2_ref[base + t], 1), :],
                                      buf_ref.at[1, pl.ds(t, 1), :], sem))

    def start(t, c):
        a, b = copies(t)
        a.start()
        b.start()
        return c

    def wait(t, c):
        a, b = copies(t)
        a.wait()
        b.wait()
        return c

    lax.fori_loop(0, tokens, start, 0, unroll=8)
    lax.fori_loop(0, tokens, wait, 0, unroll=8)
    wt = wt_ref[...]
    x = x_ref[...] + wt[:, 0:1] * buf_ref[0] + wt[:, 1:2] * buf_ref[1]
    if emit_x:
        xo_ref[...] = x
    y = x * lax.rsqrt(jnp.mean(x * x, axis=-1, keepdims=True) + RMS_EPS)
    no_ref[...] = (y * g_ref[...]).astype(no_ref.dtype)


def moe_combine(x, ys, pos1, pos2, wts, gain, norm_dtype, emit_x):
    m, d = x.shape
    tokens = _pick(m, 256, 8)
    row = lambda i, p1, p2: (i, 0)
    out_shape = [jax.ShapeDtypeStruct((m, d), norm_dtype)]
    out_specs = [pl.BlockSpec((tokens, d), row)]
    if emit_x:
        out_shape.insert(0, jax.ShapeDtypeStruct((m, d), F32))
        out_specs.insert(0, pl.BlockSpec((tokens, d), row))
    return pl.pallas_call(
        functools.partial(_combine_body, tokens=tokens, emit_x=emit_x),
        out_shape=tuple(out_shape),
        grid_spec=pltpu.PrefetchScalarGridSpec(
            num_scalar_prefetch=2,
            grid=(m // tokens,),
            in_specs=[pl.BlockSpec((tokens, d), row),
                      pl.BlockSpec((tokens, 8), row),
                      pl.BlockSpec((1, d), lambda i, p1, p2: (0, 0)),
                      pl.BlockSpec(memory_space=pl.ANY)],
            out_specs=tuple(out_specs),
            scratch_shapes=[pltpu.VMEM((2, tokens, d), F32), pltpu.SemaphoreType.DMA(())]),
        compiler_params=_cparams(("arbitrary",)),
        name="moe_combine",
    )(pos1, pos2, x, wts, gain.reshape(1, d), ys)


def _rope_partner(w_rot):
    half = MLA_ROPE_DIM // 2
    return jnp.concatenate([-w_rot[..., half:], w_rot[..., :half]], axis=-1)


def _repack_body(a_ref, o_ref):
    o_ref[...] = a_ref[0].T.astype(o_ref.dtype)


def repack_columns(wt3, layer, col_start, width):
    k = wt3.shape[2]
    assert col_start % 8 == 0 and width % LANES == 0
    tn = _pick(width, 512)
    return pl.pallas_call(
        _repack_body,
        out_shape=jax.ShapeDtypeStruct((k, width), BF16),
        grid=(width // tn,),
        in_specs=[pl.BlockSpec((pl.Element(1), pl.Element(tn), pl.Element(k)),
                               lambda j: (layer, pl.multiple_of(col_start + j * tn, 8), 0))],
        out_specs=pl.BlockSpec((k, tn), lambda j: (0, j)),
        compiler_params=_cparams(("parallel",)),
        name="repack_columns",
    )(wt3)


def _layer_weights(wt3, layer, w_uq, heads_mla):
    d = wt3.shape[2]
    gla_qk = GLA_HEADS * GLA_DK
    gla_w = GLA_HEADS * GLA_DV
    swa_q = SWA_Q_HEADS * SWA_HEAD_DIM
    swa_kv = SWA_KV_HEADS * SWA_HEAD_DIM
    sizes = (MLA_Q_LORA, MLA_KV_LORA, MLA_ROPE_DIM, gla_qk, gla_qk, gla_w, GLA_GATE_RANK, gla_w,
             swa_q, swa_kv, swa_kv, N_BRANCHES * d)
    offs = [int(o) for o in np.concatenate([[0], np.cumsum(sizes)])]
    cols = lambda a, b: wt3[layer, a:b, :]
    zcols = lambda n: jnp.zeros((n, d), wt3.dtype)
    kr = cols(offs[2], offs[3])
    half = MLA_ROPE_DIM // 2
    kr_partner = jnp.concatenate([-kr[half:], kr[:half]], axis=0)
    w_a = repack_columns(wt3, layer, 0, offs[2])
    w_r = jnp.concatenate([kr, zcols(LANES - MLA_ROPE_DIM), kr_partner, zcols(LANES - MLA_ROPE_DIM),
                           cols(offs[6], offs[7]), zcols(LANES - GLA_GATE_RANK)], axis=0).T
    w_qkv = repack_columns(wt3, layer, offs[3], offs[6] - offs[3])
    w_gg = repack_columns(wt3, layer, offs[7], offs[8] - offs[7])
    w_s = repack_columns(wt3, layer, offs[8], offs[11] - offs[8])
    w_gates = repack_columns(wt3, layer, offs[11], offs[12] - offs[11])
    qk = MLA_NOPE_DIM + MLA_ROPE_DIM
    wq = w_uq.reshape(-1, heads_mla, qk)
    pad = jnp.zeros((wq.shape[0], heads_mla, MLA_HEAD_PAD - qk), w_uq.dtype)
    w_q1 = jnp.concatenate([wq, pad], axis=-1).reshape(wq.shape[0], -1).astype(BF16)
    rot = wq[..., MLA_NOPE_DIM:]
    pad2 = jnp.zeros((wq.shape[0], heads_mla, LANES - MLA_ROPE_DIM), w_uq.dtype)
    w_q2 = jnp.concatenate([_rope_partner(rot), pad2], axis=-1).reshape(wq.shape[0], -1).astype(BF16)
    return w_a, w_r, w_qkv, w_gg, w_s, w_gates, w_q1.T, w_q2.T


def _moe_layer(x, gain, w_rg, b_rg, w_re, b_re, w_gate, w_up, w_down, layer, next_gain, norm_dtype,
               emit_x):
    m, d = x.shape
    n_exp, n_grp = w_re.shape[1], w_rg.shape[1]
    w_r = jnp.concatenate([w_re, w_rg, jnp.zeros((d, LANES - n_exp - n_grp), F32)], axis=1)
    b_r = jnp.concatenate([b_re, b_rg, jnp.zeros((LANES - n_exp - n_grp,), F32)]).reshape(1, LANES)
    hn, e_idx, wts, rank, counts = moe_router(x, gain, w_r, b_r, n_exp, n_grp)
    tm = 640 if m >= 2048 else 64
    cnt = counts[0, :n_exp]
    padded = ((cnt + tm - 1) // tm) * tm
    ends = jnp.cumsum(padded)
    offs = ends - padded
    n_tiles = -(-(TOP_K_IN_GROUP * m) // tm) + n_exp
    tile_start = jnp.arange(n_tiles, dtype=jnp.int32) * tm
    tile_valid = (tile_start < ends[-1]).astype(jnp.int32)
    last_tile = jnp.maximum(ends[-1] // tm - 1, 0).astype(jnp.int32)
    tile_src = jnp.minimum(jnp.arange(n_tiles, dtype=jnp.int32), last_tile)
    tile_expert = jnp.sum((ends[None, :] <= (tile_src * tm)[:, None]).astype(jnp.int32), axis=1)
    tile_expert = jnp.minimum(tile_expert, n_exp - 1)
    pos = offs[e_idx[:, :2]].astype(jnp.int32) + rank[:, :2]
    pos1, pos2 = pos[:, 0], pos[:, 1]
    xs = moe_dispatch(hn, pos1, pos2, n_tiles * tm)
    ys = moe_experts(xs, tile_expert, tile_src, tile_valid, w_gate, w_up, w_down, layer, tm)
    return moe_combine(x, ys, pos1, pos2, wts, next_gain, norm_dtype, emit_x)


def kernel(x, positions, attn_norm, w_in, gate_bias, mla_q_norm, mla_w_uq, mla_kv_norm, mla_w_ukv,
           gla_w_a2, gla_b_a, gla_out_norm, swa_sinks, w_br_mla, w_br_gla, w_br_swa, w_out,
           ffn_norm, w_router_group, b_router_group, w_router_expert, b_router_expert,
           w_gate, w_up, w_down, final_norm):
    batch, seq, d = x.shape
    depth = w_in.shape[0]
    m = batch * seq
    heads_mla = mla_w_uq.shape[2] // (MLA_NOPE_DIM + MLA_ROPE_DIM)
    xf = x.reshape(m, d)
    pos = positions.reshape(m, 1).astype(jnp.int32)
    half = MLA_ROPE_DIM // 2
    inv_freq = ROPE_THETA ** (-jnp.arange(0, MLA_ROPE_DIM, 2, dtype=F32) / MLA_ROPE_DIM)
    freq = jnp.concatenate([inv_freq, inv_freq, jnp.zeros((LANES - 2 * half,), F32)]).reshape(1, LANES)

    w_in_t = jnp.swapaxes(w_in, 1, 2)
    h = rmsnorm(xf, attn_norm[0], BF16)
    out = None
    for l in range(depth):
        w_a, w_r, w_qkv, w_gg, w_s, w_gates, w_q1t, w_q2t = _layer_weights(
            w_in_t, l, mla_w_uq[l], heads_mla)
        z_a = matmul(h, w_a, F32, name="in_proj_mla")
        z_r = matmul(h, w_r, F32, name="in_proj_rope_gate")
        z_qkv = matmul(h, w_qkv, BF16, name="in_proj_gla_qkv")
        z_gg = matmul(h, w_gg, BF16, name="in_proj_gla_gate")
        z_s = matmul(h, w_s, BF16, name="in_proj_swa")
        w_kv = mla_w_ukv[l].reshape(-1, heads_mla, MLA_NOPE_DIM + MLA_V_DIM)
        w_kn = w_kv[..., :MLA_NOPE_DIM].reshape(w_kv.shape[0], -1).astype(BF16)
        w_vt = w_kv[..., MLA_NOPE_DIM:].reshape(w_kv.shape[0], -1).T.astype(BF16)
        qt, k, vt = mla_prep(z_a, z_r, pos, mla_q_norm[l], mla_kv_norm[l], freq, w_q1t, w_q2t,
                             w_kn, w_vt, heads_mla)
        y_mla = mla_attention(qt, k, vt, batch, seq, heads_mla)
        w_a2p = jnp.concatenate([gla_w_a2[l], jnp.zeros((LANES - GLA_GATE_RANK, gla_w_a2.shape[2]), F32)], axis=0)
        y_gla = gla_attention(z_qkv, z_gg, z_r, 2, w_a2p, gla_b_a[l], gla_out_norm[l], batch, seq, GLA_HEADS)
        y_swa = swa_attention(z_s, swa_sinks[l], batch, seq, SWA_Q_HEADS, SWA_KV_HEADS)
        w_brs = (w_br_mla[l].astype(BF16), w_br_gla[l].astype(BF16), w_br_swa[l].astype(BF16))
        merged = merge_branches(h, (y_mla, y_gla, y_swa), w_gates, w_brs, gate_bias[l])
        xf = matmul(merged, w_out[l].astype(BF16), F32, residual=xf, name="out_proj")
        last = l == depth - 1
        res = _moe_layer(xf, ffn_norm[l], w_router_group[l], b_router_group[l],
                         w_router_expert[l], b_router_expert[l], w_gate, w_up, w_down, l,
                         final_norm if last else attn_norm[l + 1],
                         F32 if last else BF16, emit_x=not last)
        if last:
            out = res[0]
        else:
            xf, h = res
    return out.reshape(batch, seq, d)
```

```python
import functools
import math

import numpy as np
import jax
import jax.numpy as jnp
from jax import lax
from jax.experimental import pallas as pl
from jax.experimental.pallas import tpu as pltpu

F32 = jnp.float32
BF16 = jnp.bfloat16
HIGHEST = lax.Precision.HIGHEST

RMS_EPS = 1e-6
NEG_INF = -1e30
MLA_HEADS = 16
MLA_Q_LORA = 1024
MLA_KV_LORA = 512
MLA_NOPE_DIM = 128
MLA_ROPE_DIM = 64
MLA_V_DIM = 128
ROPE_THETA = 10000.0
GLA_HEADS = 8
GLA_DK = 128
GLA_DV = 256
GLA_GATE_RANK = 16
GLA_GATE_TAU = 16.0
SWA_Q_HEADS = 32
SWA_KV_HEADS = 4
SWA_HEAD_DIM = 64
SWA_WINDOW = 128
N_BRANCHES = 3
N_GROUPS = 4
EXPERTS_PER_GROUP = 8
TOP_K_IN_GROUP = 2

LANES = 128
MXU_WIDTH = 256
MLA_HEAD_PAD = 256
GLA_SUB = 16
VMEM_LIMIT = 60 * 1024 * 1024


def _cparams(sem, vmem=VMEM_LIMIT):
    return pltpu.CompilerParams(dimension_semantics=sem, vmem_limit_bytes=vmem)


def _pick(n, cap, mult=LANES):
    best = None
    for t in range(mult, min(n, cap) + 1, mult):
        if n % t == 0:
            best = t
    assert best is not None, (n, cap, mult)
    return best


def _rmsnorm_body(x_ref, g_ref, o_ref):
    x = x_ref[...]
    y = x * lax.rsqrt(jnp.mean(x * x, axis=-1, keepdims=True) + RMS_EPS)
    o_ref[...] = (y * g_ref[...]).astype(o_ref.dtype)


def rmsnorm(x, gain, out_dtype):
    m, d = x.shape
    tm = _pick(m, 512, 8)
    return pl.pallas_call(
        _rmsnorm_body,
        out_shape=jax.ShapeDtypeStruct((m, d), out_dtype),
        grid=(m // tm,),
        in_specs=[pl.BlockSpec((tm, d), lambda i: (i, 0)),
                  pl.BlockSpec((1, d), lambda i: (0, 0))],
        out_specs=pl.BlockSpec((tm, d), lambda i: (i, 0)),
        compiler_params=_cparams(("parallel",)),
        name="rmsnorm",
    )(x, gain.reshape(1, d))


def _matmul_body(a_ref, w_ref, o_ref):
    w = w_ref[...].astype(BF16)
    o_ref[...] = jnp.dot(a_ref[...], w, preferred_element_type=F32).astype(o_ref.dtype)


def _matmul_res_body(a_ref, w_ref, r_ref, o_ref):
    acc = jnp.dot(a_ref[...], w_ref[...].astype(BF16), preferred_element_type=F32)
    o_ref[...] = (r_ref[...] + acc).astype(o_ref.dtype)


def matmul(a, w, out_dtype, residual=None, tm_cap=512, tn_cap=1024, name="matmul"):
    m, k = a.shape
    n = w.shape[1]
    tm = _pick(m, tm_cap, 16)
    tn = _pick(n, tn_cap, MXU_WIDTH if n % MXU_WIDTH == 0 else LANES)
    in_specs = [pl.BlockSpec((tm, k), lambda j, i: (i, 0)),
                pl.BlockSpec((k, tn), lambda j, i: (0, j))]
    args = [a, w]
    body = _matmul_body
    if residual is not None:
        in_specs.append(pl.BlockSpec((tm, tn), lambda j, i: (i, j)))
        args.append(residual)
        body = _matmul_res_body
    return pl.pallas_call(
        body,
        out_shape=jax.ShapeDtypeStruct((m, n), out_dtype),
        grid=(n // tn, m // tm),
        in_specs=in_specs,
        out_specs=pl.BlockSpec((tm, tn), lambda j, i: (i, j)),
        compiler_params=_cparams(("parallel", "parallel")),
        name=name,
    )(*args)


def _rope_tables(pos_ref, freq_ref):
    ang = pos_ref[...].astype(F32) * freq_ref[...]
    return jnp.cos(ang), jnp.sin(ang)


def _mla_q_body(c_ref, pos_ref, g_ref, freq_ref, w1t_ref, w2t_ref, qt_ref, *, heads, scale):
    c = c_ref[...]
    cn = (c * lax.rsqrt(jnp.mean(c * c, axis=-1, keepdims=True) + RMS_EPS) * g_ref[...]).astype(BF16)
    nt = (((1,), (1,)), ((), ()))
    q = lax.dot_general(w1t_ref[...], cn, nt, preferred_element_type=F32)
    qp = lax.dot_general(w2t_ref[...], cn, nt, preferred_element_type=F32)
    ang = freq_ref[...] * pos_ref[...].astype(F32)
    cs, sn = jnp.cos(ang), jnp.sin(ang)
    for h in range(heads):
        a = h * MLA_HEAD_PAD
        qt_ref[a:a + LANES, :] = (q[a:a + LANES] * scale).astype(BF16)
        rot = q[a + LANES:a + 2 * LANES] * cs + qp[h * LANES:(h + 1) * LANES] * sn
        qt_ref[a + LANES:a + 2 * LANES, :] = (rot * scale).astype(BF16)


def _mla_kv_body(c_ref, kr_ref, krp_ref, pos_ref, g_ref, freq_ref, wk_ref, wvt_ref, k_ref, vt_ref,
                 *, heads):
    c = c_ref[...]
    cn = (c * lax.rsqrt(jnp.mean(c * c, axis=-1, keepdims=True) + RMS_EPS) * g_ref[...]).astype(BF16)
    kn = jnp.dot(cn, wk_ref[...], preferred_element_type=F32)
    vt_ref[...] = lax.dot_general(wvt_ref[...], cn, (((1,), (1,)), ((), ())),
                                  preferred_element_type=F32).astype(BF16)
    cs, sn = _rope_tables(pos_ref, freq_ref)
    krot = (kr_ref[...] * cs + krp_ref[...] * sn).astype(BF16)
    for h in range(heads):
        a = h * MLA_HEAD_PAD
        k_ref[:, a:a + LANES] = kn[:, h * LANES:(h + 1) * LANES].astype(BF16)
        k_ref[:, a + LANES:a + 2 * LANES] = krot


def mla_prep(z_a, z_r, pos, q_norm, kv_norm, freq, w_q1t, w_q2t, w_kn, w_vt, heads):
    m = z_a.shape[0]
    ql, kvl = w_q1t.shape[1], w_kn.shape[0]
    tm = _pick(m, 256, LANES)
    scale = (MLA_NOPE_DIM + MLA_ROPE_DIM) ** -0.5
    row = lambda i: (i, 0)
    const = lambda i: (0, 0)
    qt = pl.pallas_call(
        functools.partial(_mla_q_body, heads=heads, scale=scale),
        out_shape=jax.ShapeDtypeStruct((heads * MLA_HEAD_PAD, m), BF16),
        grid=(m // tm,),
        in_specs=[pl.BlockSpec((tm, ql), row),
                  pl.BlockSpec((1, tm), lambda i: (0, i)),
                  pl.BlockSpec((1, ql), const),
                  pl.BlockSpec((LANES, 1), const),
                  pl.BlockSpec(w_q1t.shape, const),
                  pl.BlockSpec(w_q2t.shape, const)],
        out_specs=pl.BlockSpec((heads * MLA_HEAD_PAD, tm), lambda i: (0, i)),
        compiler_params=_cparams(("parallel",)),
        name="mla_q_prep",
    )(z_a, pos.reshape(1, m), q_norm.reshape(1, ql), freq.reshape(LANES, 1), w_q1t, w_q2t)
    k, vt = pl.pallas_call(
        functools.partial(_mla_kv_body, heads=heads),
        out_shape=(jax.ShapeDtypeStruct((m, heads * MLA_HEAD_PAD), BF16),
                   jax.ShapeDtypeStruct((heads * MLA_V_DIM, m), BF16)),
        grid=(m // tm,),
        in_specs=[pl.BlockSpec((tm, kvl), lambda i: (i, ql // kvl)),
                  pl.BlockSpec((tm, LANES), lambda i: (i, 0)),
                  pl.BlockSpec((tm, LANES), lambda i: (i, 1)),
                  pl.BlockSpec((tm, 1), row),
                  pl.BlockSpec((1, kvl), const),
                  pl.BlockSpec((1, LANES), const),
                  pl.BlockSpec(w_kn.shape, const),
                  pl.BlockSpec(w_vt.shape, const)],
        out_specs=(pl.BlockSpec((tm, heads * MLA_HEAD_PAD), row),
                   pl.BlockSpec((heads * MLA_V_DIM, tm), lambda i: (0, i))),
        compiler_params=_cparams(("parallel",)),
        name="mla_kv_prep",
    )(z_a, z_r, z_r, pos, kv_norm.reshape(1, kvl), freq, w_kn, w_vt)
    return qt, k, vt


def _mla_attn_body(qt_ref, k_ref, vt_ref, o_ref, m_ref, l_ref, acc_ref, *, tq, sub):
    del sub
    qi = pl.program_id(2)
    n_heads = m_ref.shape[0]
    hd, vd = MLA_HEAD_PAD, MLA_V_DIM
    m_ref[...] = jnp.full_like(m_ref, NEG_INF)
    l_ref[...] = jnp.zeros_like(l_ref)
    acc_ref[...] = jnp.zeros_like(acc_ref)

    def scores(j, h):
        k0 = pl.multiple_of(j * tq, tq)
        return jnp.dot(k_ref[pl.ds(k0, tq), h * hd:(h + 1) * hd], qt_ref[h * hd:(h + 1) * hd, :],
                       preferred_element_type=F32)

    def accumulate(j, h, st):
        k0 = pl.multiple_of(j * tq, tq)
        m_old = m_ref[h]
        m_new = jnp.maximum(m_old, jnp.max(st, axis=0, keepdims=True))
        alpha = jnp.exp(m_old - m_new)
        p = jnp.exp(st - m_new)
        l_ref[h] = alpha * l_ref[h] + jnp.sum(p, axis=0, keepdims=True)
        acc_ref[h] = alpha * acc_ref[h] + jnp.dot(vt_ref[h * vd:(h + 1) * vd, pl.ds(k0, tq)],
                                                  p.astype(BF16), preferred_element_type=F32)
        m_ref[h] = m_new

    def full_tile(j, sts):
        nxt = tuple(scores(j + 1, h) for h in range(n_heads))
        for h in range(n_heads):
            accumulate(j, h, sts[h])
        return nxt

    sts = lax.fori_loop(0, qi, full_tile, tuple(scores(0, h) for h in range(n_heads)))
    tri = (lax.broadcasted_iota(jnp.int32, (tq, tq), 0)
           <= lax.broadcasted_iota(jnp.int32, (tq, tq), 1))
    for h in range(n_heads):
        accumulate(qi, h, jnp.where(tri, sts[h], NEG_INF))
        o_ref[:, h * vd:(h + 1) * vd] = (acc_ref[h] / l_ref[h]).T.astype(o_ref.dtype)


def mla_attention(qt, k, vt, batch, seq, heads):
    tq = _pick(seq, 512, LANES)
    sub = tq // 2
    nq = seq // tq
    nh = math.gcd(heads, 2)
    return pl.pallas_call(
        functools.partial(_mla_attn_body, tq=tq, sub=sub),
        out_shape=jax.ShapeDtypeStruct((batch * seq, heads * MLA_V_DIM), BF16),
        grid=(batch, heads // nh, nq),
        in_specs=[pl.BlockSpec((nh * MLA_HEAD_PAD, tq), lambda b, h, qi: (h, b * nq + qi)),
                  pl.BlockSpec((seq, nh * MLA_HEAD_PAD), lambda b, h, qi: (b, h)),
                  pl.BlockSpec((nh * MLA_V_DIM, seq), lambda b, h, qi: (h, b))],
        out_specs=pl.BlockSpec((tq, nh * MLA_V_DIM), lambda b, h, qi: (b * nq + qi, h)),
        scratch_shapes=[pltpu.VMEM((nh, 1, tq), F32), pltpu.VMEM((nh, 1, tq), F32),
                        pltpu.VMEM((nh, MLA_V_DIM, tq), F32)],
        compiler_params=_cparams(("parallel", "parallel", "parallel")),
        name="mla_attention",
    )(qt, k, vt)


def _swa_body(sink_ref, q_ref, kp_ref, kc_ref, vp_ref, vc_ref, o_ref, *, group):
    w = SWA_WINDOW
    n = pl.program_id(1)
    scale = SWA_HEAD_DIM ** -0.5
    lane = lax.broadcasted_iota(jnp.int32, (2 * w, LANES), 1)
    lo = lane < SWA_HEAD_DIM
    qlane_lo = lax.broadcasted_iota(jnp.int32, (w, LANES), 1) < SWA_HEAD_DIM
    ri = lax.broadcasted_iota(jnp.int32, (2 * w, 2 * w), 0) % w
    cj = lax.broadcasted_iota(jnp.int32, (2 * w, 2 * w), 1)
    valid = (cj > ri) & (cj <= ri + w) & ((cj >= w) | (n > 0))
    row_first = lax.broadcasted_iota(jnp.int32, (2 * w, 1), 0) < w
    kfull = jnp.concatenate([kp_ref[...], kc_ref[...]], axis=0).astype(F32)
    vfull = jnp.concatenate([vp_ref[...], vc_ref[...]], axis=0).astype(F32)
    n_chunks = kfull.shape[1] // LANES
    for c in range(n_chunks):
        kc = kfull[:, c * LANES:(c + 1) * LANES]
        vc = vfull[:, c * LANES:(c + 1) * LANES]
        kr = pltpu.roll(kc, SWA_HEAD_DIM, axis=1)
        vr = pltpu.roll(vc, SWA_HEAD_DIM, axis=1)
        for sub in range(2):
            hk = 2 * c + sub
            k2 = (jnp.where(lo, kc, kr) if sub == 0 else jnp.where(lo, kr, kc)).astype(BF16)
            v2 = (jnp.where(lo, vc, vr) if sub == 0 else jnp.where(lo, vr, vc)).astype(BF16)
            for j in range(group // 2):
                chunk = hk * (group // 2) + j
                hq = hk * group + 2 * j
                qp = q_ref[:, chunk * LANES:(chunk + 1) * LANES].astype(F32)
                qa = jnp.where(qlane_lo, qp, 0.0)
                qb = jnp.where(qlane_lo, 0.0, qp)
                qs = jnp.concatenate([qa, qb], axis=0).astype(BF16)
                s = lax.dot_general(qs, k2, (((1,), (1,)), ((), ())),
                                    preferred_element_type=F32) * scale
                s = jnp.where(valid, s, NEG_INF)
                sink = jnp.where(row_first, sink_ref[hq], sink_ref[hq + 1])
                mx = jnp.maximum(jnp.max(s, axis=-1, keepdims=True), sink)
                p = jnp.exp(s - mx)
                denom = jnp.sum(p, axis=-1, keepdims=True) + jnp.exp(sink - mx)
                o2 = jnp.dot((p / denom).astype(BF16), v2, preferred_element_type=F32)
                o_ref[:, chunk * LANES:(chunk + 1) * LANES] = jnp.where(
                    qlane_lo, o2[:w], o2[w:]).astype(o_ref.dtype)


def swa_attention(z_s, sinks, batch, seq, q_heads, kv_heads):
    w = SWA_WINDOW
    nb = seq // w
    qw = q_heads * SWA_HEAD_DIM
    kw = kv_heads * SWA_HEAD_DIM
    assert qw % kw == 0 and kw % LANES == 0 and (q_heads // kv_heads) % 2 == 0
    kb = qw // kw
    cur = lambda blk: (lambda b, n, s: (b * nb + n, blk))
    prev = lambda blk: (lambda b, n, s: (b * nb + jnp.maximum(n - 1, 0), blk))
    return pl.pallas_call(
        functools.partial(_swa_body, group=q_heads // kv_heads),
        out_shape=jax.ShapeDtypeStruct((batch * seq, qw), BF16),
        grid_spec=pltpu.PrefetchScalarGridSpec(
            num_scalar_prefetch=1,
            grid=(batch, nb),
            in_specs=[pl.BlockSpec((w, qw), cur(0)),
                      pl.BlockSpec((w, kw), prev(kb)),
                      pl.BlockSpec((w, kw), cur(kb)),
                      pl.BlockSpec((w, kw), prev(kb + 1)),
                      pl.BlockSpec((w, kw), cur(kb + 1))],
            out_specs=pl.BlockSpec((w, qw), cur(0))),
        compiler_params=_cparams(("parallel", "arbitrary")),
        name="swa_attention",
    )(sinks, z_s, z_s, z_s, z_s, z_s)


def _gla_body(q_ref, k_ref, v_ref, g_ref, a_ref, wa_ref, ba_ref, gn_ref, o_ref,
              state_ref, b_ref, *, chunk, tokens, n_heads):
    c_sz = chunk
    dk, dv = GLA_DK, GLA_DV
    t = pl.program_id(2)

    @pl.when(t == 0)
    def _():
        state_ref[...] = jnp.zeros_like(state_ref)

    z = jnp.dot(a_ref[...], wa_ref[...], precision=HIGHEST, preferred_element_type=F32) + ba_ref[...]
    log_a = (jnp.minimum(z, 0.0) - jnp.log(1.0 + jnp.exp(-jnp.abs(z)))) * (1.0 / GLA_GATE_TAU)
    ri = lax.broadcasted_iota(jnp.int32, (tokens, tokens), 0)
    ci = lax.broadcasted_iota(jnp.int32, (tokens, tokens), 1)
    tri = ((ci <= ri) & (ci // c_sz == ri // c_sz)).astype(BF16)
    hi = log_a.astype(BF16)
    r1 = log_a - hi.astype(F32)
    mid = r1.astype(BF16)
    lo = (r1 - mid.astype(F32)).astype(BF16)
    b_ref[...] = (jnp.dot(tri, hi, preferred_element_type=F32)
                  + jnp.dot(tri, mid, preferred_element_type=F32)
                  + jnp.dot(tri, lo, preferred_element_type=F32))

    qscale = GLA_DK ** -0.5
    sub = GLA_SUB
    n_sub = c_sz // sub
    rowid = lax.broadcasted_iota(jnp.int32, (sub, 1), 0)

    def head_chunk(c0, hh):
        rows = pl.ds(c0, c_sz)
        kcols = slice(hh * dk, (hh + 1) * dk)
        vcols = slice(hh * dv, (hh + 1) * dv)
        b = b_ref[rows, kcols]
        q = q_ref[rows, kcols].astype(F32) * qscale
        k = k_ref[rows, kcols].astype(F32)
        v = v_ref[rows, vcols]
        vf = v.astype(F32)
        state_t = state_ref[hh]
        o_inter = lax.dot_general((q * jnp.exp(b)).astype(BF16), state_t.astype(BF16),
                                  (((1,), (1,)), ((), ())), preferred_element_type=F32)
        pieces = []
        for i in range(n_sub):
            r0 = i * sub
            b_i, q_i, k_i, v_i = b[r0:r0 + sub], q[r0:r0 + sub], k[r0:r0 + sub], vf[r0:r0 + sub]
            acc = jnp.zeros((sub, dv), F32)
            for j in range(sub):
                wgt = jnp.exp(b_i - b_i[j:j + 1, :])
                col = jnp.sum(q_i * k_i[j:j + 1, :] * wgt, axis=-1, keepdims=True)
                col = jnp.where(rowid >= j, col, 0.0)
                acc = acc + col * v_i[j:j + 1, :]
            if i > 0:
                ref_b = b[r0 - 1:r0, :]
                qh = (q_i * jnp.exp(b_i - ref_b)).astype(BF16)
                kh = (k[:r0] * jnp.exp(ref_b - b[:r0])).astype(BF16)
                att = lax.dot_general(qh, kh, (((1,), (1,)), ((), ())), preferred_element_type=F32)
                acc = acc + jnp.dot(att.astype(BF16), v[:r0], preferred_element_type=F32)
            pieces.append(acc)
        o = o_inter + jnp.concatenate(pieces, axis=0)
        b_last = b[c_sz - 1:c_sz, :]
        k_dec = (k * jnp.exp(b_last - b)).astype(BF16)
        upd = lax.dot_general(v, k_dec, (((0,), (0,)), ((), ())), preferred_element_type=F32)
        state_ref[hh] = state_t * jnp.exp(b_last) + upd
        y = o * lax.rsqrt(jnp.mean(o * o, axis=-1, keepdims=True) + RMS_EPS) * gn_ref[...]
        g = g_ref[rows, vcols].astype(F32)
        y = y * (g / (1.0 + jnp.exp(-g)))
        o_ref[rows, vcols] = y.astype(o_ref.dtype)

    def chunk_step(c, carry):
        c0 = pl.multiple_of(c * c_sz, c_sz)
        for hh in range(n_heads):
            head_chunk(c0, hh)
        return carry

    lax.fori_loop(0, tokens // c_sz, chunk_step, 0)


def gla_attention(z_qkv, z_gg, z_a, a_blk, w_a2p, b_a, out_norm, batch, seq, heads, chunk=64):
    dk, dv = GLA_DK, GLA_DV
    tokens = _pick(seq, 256, chunk)
    nt = seq // tokens
    nh = math.gcd(heads, 8)
    wk, wv = nh * dk, nh * dv
    k_off = heads * dk // wk
    v_off = 2 * heads * dk // wv
    row = lambda blk_fn: (lambda b, h, t: (b * nt + t, blk_fn(h)))
    return pl.pallas_call(
        functools.partial(_gla_body, chunk=chunk, tokens=tokens, n_heads=nh),
        out_shape=jax.ShapeDtypeStruct((batch * seq, heads * dv), BF16),
        grid=(batch, heads // nh, nt),
        in_specs=[pl.BlockSpec((tokens, wk), row(lambda h: h)),
                  pl.BlockSpec((tokens, wk), row(lambda h: k_off + h)),
                  pl.BlockSpec((tokens, wv), row(lambda h: v_off + h)),
                  pl.BlockSpec((tokens, wv), row(lambda h: h)),
                  pl.BlockSpec((tokens, LANES), row(lambda h: a_blk)),
                  pl.BlockSpec((LANES, wk), lambda b, h, t: (0, h)),
                  pl.BlockSpec((1, wk), lambda b, h, t: (0, h)),
                  pl.BlockSpec((1, dv), lambda b, h, t: (0, 0))],
        out_specs=pl.BlockSpec((tokens, wv), row(lambda h: h)),
        scratch_shapes=[pltpu.VMEM((nh, dv, dk), F32), pltpu.VMEM((tokens, wk), F32)],
        compiler_params=_cparams(("parallel", "parallel", "arbitrary")),
        name="gla_attention",
    )(z_qkv, z_qkv, z_qkv, z_gg, z_a, w_a2p, b_a.reshape(1, -1), out_norm.reshape(1, dv))


def _merge_body(h_ref, y0_ref, y1_ref, y2_ref, wg0_ref, wg1_ref, wg2_ref,
                wb0_ref, wb1_ref, wb2_ref, bias_ref, o_ref):
    h = h_ref[...]
    acc = None
    branches = ((y0_ref, wg0_ref, wb0_ref), (y1_ref, wg1_ref, wb1_ref), (y2_ref, wg2_ref, wb2_ref))
    for b, (y_ref, wg_ref, wb_ref) in enumerate(branches):
        logit = jnp.dot(h, wg_ref[...], preferred_element_type=F32) + bias_ref[b:b + 1, :]
        gate = 1.0 / (1.0 + jnp.exp(-logit))
        term = gate * jnp.dot(y_ref[...], wb_ref[...], preferred_element_type=F32)
        acc = term if acc is None else acc + term
    o_ref[...] = acc.astype(o_ref.dtype)


def merge_branches(h, ys, w_gates, w_brs, bias3):
    m, d = h.shape
    bw = ys[0].shape[1]
    tm = _pick(m, 512, 16)
    tn = _pick(d, 256)
    nj = d // tn
    gate_spec = lambda b: pl.BlockSpec((d, tn), lambda j, i: (0, b * nj + j))
    return pl.pallas_call(
        _merge_body,
        out_shape=jax.ShapeDtypeStruct((m, d), BF16),
        grid=(nj, m // tm),
        in_specs=[pl.BlockSpec((tm, d), lambda j, i: (i, 0))]
        + [pl.BlockSpec((tm, bw), lambda j, i: (i, 0))] * N_BRANCHES
        + [gate_spec(b) for b in range(N_BRANCHES)]
        + [pl.BlockSpec((bw, tn), lambda j, i: (0, j))] * N_BRANCHES
        + [pl.BlockSpec((N_BRANCHES, tn), lambda j, i: (0, j))],
        out_specs=pl.BlockSpec((tm, tn), lambda j, i: (i, j)),
        compiler_params=_cparams(("parallel", "parallel")),
        name="merge_branches",
    )(h, *ys, w_gates, w_gates, w_gates, *w_brs, bias3)


def _router_body(x_ref, g_ref, w_ref, b_ref, h_ref, e_ref, wt_ref, rk_ref, cnt_ref, carry_ref,
                 *, n_exp, n_grp, per_grp):
    i = pl.program_id(0)

    @pl.when(i == 0)
    def _():
        carry_ref[...] = jnp.zeros_like(carry_ref)

    x = x_ref[...]
    tm = x.shape[0]
    hn = x * lax.rsqrt(jnp.mean(x * x, axis=-1, keepdims=True) + RMS_EPS) * g_ref[...]
    h_ref[...] = hn
    logits = jnp.dot(hn, w_ref[...], precision=HIGHEST, preferred_element_type=F32) + b_ref[...]
    lane = lax.broadcasted_iota(jnp.int32, (tm, LANES), 1)
    big = jnp.int32(LANES)

    def top1(vals):
        mx = jnp.max(vals, axis=-1, keepdims=True)
        idx = jnp.min(jnp.where(vals == mx, lane, big), axis=-1, keepdims=True)
        return mx, idx

    gl = jnp.where((lane >= n_exp) & (lane < n_exp + n_grp), logits, NEG_INF)
    gmax, gidx = top1(gl)
    g_top_p = 1.0 / jnp.sum(jnp.exp(gl - gmax), axis=-1, keepdims=True)
    grp = gidx - n_exp
    el = jnp.where((lane >= grp * per_grp) & (lane < (grp + 1) * per_grp), logits, NEG_INF)
    m1, i1 = top1(el)
    m2, i2 = top1(jnp.where(lane == i1, NEG_INF, el))
    e21 = jnp.exp(m2 - m1)
    w1 = g_top_p / (1.0 + e21)
    w2 = w1 * e21
    e_ref[...] = jnp.where(lane == 0, i1, jnp.where(lane == 1, i2, 0))[:, :8]
    wt_ref[...] = jnp.where(lane == 0, w1, jnp.where(lane == 1, w2, 0.0))[:, :8]

    oh1 = (lane == i1).astype(F32)
    oh2 = (lane == i2).astype(F32)
    ri = lax.broadcasted_iota(jnp.int32, (tm, tm), 0)
    ci = lax.broadcasted_iota(jnp.int32, (tm, tm), 1)
    strict = (ci < ri).astype(BF16)
    cum1 = jnp.dot(strict, oh1.astype(BF16), preferred_element_type=F32)
    cum2 = jnp.dot(strict, oh2.astype(BF16), preferred_element_type=F32)
    tot1 = jnp.sum(oh1, axis=0, keepdims=True)
    tot2 = jnp.sum(oh2, axis=0, keepdims=True)
    carry = carry_ref[...]
    r1 = jnp.sum(oh1 * (cum1 + carry), axis=-1, keepdims=True)
    r2 = jnp.sum(oh2 * (cum2 + carry + tot1), axis=-1, keepdims=True)
    rk_ref[...] = jnp.where(lane == 0, r1, jnp.where(lane == 1, r2, 0.0))[:, :8].astype(jnp.int32)
    carry = carry + tot1 + tot2
    carry_ref[...] = carry
    cnt_ref[...] = jnp.broadcast_to(carry, cnt_ref.shape).astype(jnp.int32)


def moe_router(x, gain, w_r, b_r, n_exp, n_grp):
    m, d = x.shape
    tm = _pick(m, 256, 8)
    row = lambda i: (i, 0)
    const = lambda i: (0, 0)
    return pl.pallas_call(
        functools.partial(_router_body, n_exp=n_exp, n_grp=n_grp, per_grp=n_exp // n_grp),
        out_shape=(jax.ShapeDtypeStruct((m, d), F32),
                   jax.ShapeDtypeStruct((m, 8), jnp.int32),
                   jax.ShapeDtypeStruct((m, 8), F32),
                   jax.ShapeDtypeStruct((m, 8), jnp.int32),
                   jax.ShapeDtypeStruct((8, LANES), jnp.int32)),
        grid=(m // tm,),
        in_specs=[pl.BlockSpec((tm, d), row), pl.BlockSpec((1, d), const),
                  pl.BlockSpec((d, LANES), const), pl.BlockSpec((1, LANES), const)],
        out_specs=(pl.BlockSpec((tm, d), row), pl.BlockSpec((tm, 8), row),
                   pl.BlockSpec((tm, 8), row), pl.BlockSpec((tm, 8), row),
                   pl.BlockSpec((8, LANES), const)),
        scratch_shapes=[pltpu.VMEM((1, LANES), F32)],
        compiler_params=_cparams(("arbitrary",)),
        name="moe_router",
    )(x, gain.reshape(1, d), w_r, b_r)


def _dispatch_body(p1_ref, p2_ref, h_ref, xs_in_ref, xs_ref, sem, *, tokens):
    del xs_in_ref
    base = pl.program_id(0) * tokens

    def copies(t):
        src = h_ref.at[pl.ds(t, 1), :]
        return (pltpu.make_async_copy(src, xs_ref.at[pl.ds(p1_ref[base + t], 1), :], sem),
                pltpu.make_async_copy(src, xs_ref.at[pl.ds(p2_ref[base + t], 1), :], sem))

    def start(t, c):
        a, b = copies(t)
        a.start(priority=0)
        b.start(priority=1)
        return c

    def wait(t, c):
        a, b = copies(t)
        a.wait()
        b.wait()
        return c

    lax.fori_loop(0, tokens, start, 0, unroll=8)
    lax.fori_loop(0, tokens, wait, 0, unroll=8)


def moe_dispatch(h, pos1, pos2, rows):
    m, d = h.shape
    tokens = _pick(m, 256, 8)
    xs0 = jnp.zeros((rows, d), h.dtype)
    return pl.pallas_call(
        functools.partial(_dispatch_body, tokens=tokens),
        out_shape=jax.ShapeDtypeStruct((rows, d), h.dtype),
        grid_spec=pltpu.PrefetchScalarGridSpec(
            num_scalar_prefetch=2,
            grid=(m // tokens,),
            in_specs=[pl.BlockSpec((tokens, d), lambda i, p1, p2: (i, 0)),
                      pl.BlockSpec(memory_space=pl.ANY)],
            out_specs=pl.BlockSpec(memory_space=pl.ANY),
            scratch_shapes=[pltpu.SemaphoreType.DMA(())]),
        input_output_aliases={3: 0},
        compiler_params=_cparams(("arbitrary",)),
        name="moe_dispatch",
    )(pos1, pos2, h, xs0)


def _experts_up_body(te_ref, ts_ref, tv_ref, x_ref, wg_ref, wu_ref, h_ref):
    del te_ref, ts_ref
    i = pl.program_id(1)

    @pl.when(tv_ref[i] > 0)
    def _():
        x = x_ref[...].astype(BF16)
        a = jnp.dot(x, wg_ref[...].astype(BF16), preferred_element_type=F32)
        u = jnp.dot(x, wu_ref[...].astype(BF16), preferred_element_type=F32)
        h_ref[...] = ((a / (1.0 + jnp.exp(-a))) * u).astype(h_ref.dtype)

    @pl.when(tv_ref[i] == 0)
    def _():
        h_ref[...] = jnp.zeros_like(h_ref)


def _experts_down_body(te_ref, ts_ref, tv_ref, h_ref, wd_ref, o_ref):
    del te_ref, ts_ref
    i = pl.program_id(1)

    @pl.when(tv_ref[i] > 0)
    def _():
        o_ref[...] = jnp.dot(h_ref[...], wd_ref[...].astype(BF16), preferred_element_type=F32)

    @pl.when(tv_ref[i] == 0)
    def _():
        o_ref[...] = jnp.zeros_like(o_ref)


def moe_experts(xs, tile_expert, tile_src, tile_valid, w_gate, w_up, w_down, layer, tm):
    rows = xs.shape[0]
    d, ff = w_gate.shape[2], w_gate.shape[3]
    n_tiles = rows // tm
    tf = _pick(ff, 384)
    td = _pick(d, 2048)
    prefetch = (tile_expert, tile_src, tile_valid)
    w_map = lambda j, i, te, ts, tv: (layer, te[i], 0, j)
    hid = pl.pallas_call(
        _experts_up_body,
        out_shape=jax.ShapeDtypeStruct((rows, ff), BF16),
        grid_spec=pltpu.PrefetchScalarGridSpec(
            num_scalar_prefetch=3,
            grid=(ff // tf, n_tiles),
            in_specs=[pl.BlockSpec((tm, d), lambda j, i, te, ts, tv: (ts[i], 0)),
                      pl.BlockSpec((None, None, d, tf), w_map),
                      pl.BlockSpec((None, None, d, tf), w_map)],
            out_specs=pl.BlockSpec((tm, tf), lambda j, i, te, ts, tv: (i, j))),
        compiler_params=_cparams(("arbitrary", "arbitrary")),
        name="moe_experts_up",
    )(*prefetch, xs, w_gate, w_up)
    return pl.pallas_call(
        _experts_down_body,
        out_shape=jax.ShapeDtypeStruct((rows, d), F32),
        grid_spec=pltpu.PrefetchScalarGridSpec(
            num_scalar_prefetch=3,
            grid=(d // td, n_tiles),
            in_specs=[pl.BlockSpec((tm, ff), lambda j, i, te, ts, tv: (ts[i], 0)),
                      pl.BlockSpec((None, None, ff, td), w_map)],
            out_specs=pl.BlockSpec((tm, td), lambda j, i, te, ts, tv: (i, j))),
        compiler_params=_cparams(("arbitrary", "arbitrary")),
        name="moe_experts_down",
    )(*prefetch, hid, w_down)


def _combine_body(p1_ref, p2_ref, x_ref, wt_ref, g_ref, ys_ref, *rest, tokens, emit_x):
    if emit_x:
        xo_ref, no_ref, buf_ref, sem = rest
    else:
        no_ref, buf_ref, sem = rest
    base = pl.program_id(0) * tokens

    def copies(t):
        return (pltpu.make_async_copy(ys_ref.at[pl.ds(p1_ref[base + t], 1), :],
                                      buf_ref.at[0, pl.ds(t, 1), :], sem),
                pltpu.make_async_copy(ys_ref.at[pl.ds(p2_ref[base + t], 1), :],
                                      buf_ref.at[1, pl.ds(t, 1), :], sem))

    def start(t, c):
        a, b = copies(t)
        a.start(priority=0)
        b.start(priority=1)
        return c

    def wait(t, c):
        a, b = copies(t)
        a.wait()
        b.wait()
        return c

    lax.fori_loop(0, tokens, start, 0, unroll=8)
    lax.fori_loop(0, tokens, wait, 0, unroll=8)
    wt = wt_ref[...]
    x = x_ref[...] + wt[:, 0:1] * buf_ref[0] + wt[:, 1:2] * buf_ref[1]
    if emit_x:
        xo_ref[...] = x
    y = x * lax.rsqrt(jnp.mean(x * x, axis=-1, keepdims=True) + RMS_EPS)
    no_ref[...] = (y * g_ref[...]).astype(no_ref.dtype)


def moe_combine(x, ys, pos1, pos2, wts, gain, norm_dtype, emit_x):
    m, d = x.shape
    tokens = _pick(m, 256, 8)
    row = lambda i, p1, p2: (i, 0)
    out_shape = [jax.ShapeDtypeStruct((m, d), norm_dtype)]
    out_specs = [pl.BlockSpec((tokens, d), row)]
    if emit_x:
        out_shape.insert(0, jax.ShapeDtypeStruct((m, d), F32))
        out_specs.insert(0, pl.BlockSpec((tokens, d), row))
    return pl.pallas_call(
        functools.partial(_combine_body, tokens=tokens, emit_x=emit_x),
        out_shape=tuple(out_shape),
        grid_spec=pltpu.PrefetchScalarGridSpec(
            num_scalar_prefetch=2,
            grid=(m // tokens,),
            in_specs=[pl.BlockSpec((tokens, d), row),
                      pl.BlockSpec((tokens, 8), row),
                      pl.BlockSpec((1, d), lambda i, p1, p2: (0, 0)),
                      pl.BlockSpec(memory_space=pl.ANY)],
            out_specs=tuple(out_specs),
            scratch_shapes=[pltpu.VMEM((2, tokens, d), F32), pltpu.SemaphoreType.DMA(())]),
        compiler_params=_cparams(("arbitrary",)),
        name="moe_combine",
    )(pos1, pos2, x, wts, gain.reshape(1, d), ys)


def _rope_partner(w_rot):
    half = MLA_ROPE_DIM // 2
    return jnp.concatenate([-w_rot[..., half:], w_rot[..., :half]], axis=-1)


def _repack_body(a_ref, o_ref):
    o_ref[...] = a_ref[0].T.astype(o_ref.dtype)


def repack_columns(wt3, layer, col_start, width):
    k = wt3.shape[2]
    assert col_start % 8 == 0 and width % LANES == 0
    tn = _pick(width, 512)
    return pl.pallas_call(
        _repack_body,
        out_shape=jax.ShapeDtypeStruct((k, width), BF16),
        grid=(width // tn,),
        in_specs=[pl.BlockSpec((pl.Element(1), pl.Element(tn), pl.Element(k)),
                               lambda j: (layer, pl.multiple_of(col_start + j * tn, 8), 0))],
        out_specs=pl.BlockSpec((k, tn), lambda j: (0, j)),
        compiler_params=_cparams(("parallel",)),
        name="repack_columns",
    )(wt3)


def _layer_weights(wt3, layer, w_uq, heads_mla):
    d = wt3.shape[2]
    gla_qk = GLA_HEADS * GLA_DK
    gla_w = GLA_HEADS * GLA_DV
    swa_q = SWA_Q_HEADS * SWA_HEAD_DIM
    swa_kv = SWA_KV_HEADS * SWA_HEAD_DIM
    sizes = (MLA_Q_LORA, MLA_KV_LORA, MLA_ROPE_DIM, gla_qk, gla_qk, gla_w, GLA_GATE_RANK, gla_w,
             swa_q, swa_kv, swa_kv, N_BRANCHES * d)
    offs = [int(o) for o in np.concatenate([[0], np.cumsum(sizes)])]
    cols = lambda a, b: wt3[layer, a:b, :]
    zcols = lambda n: jnp.zeros((n, d), wt3.dtype)
    kr = cols(offs[2], offs[3])
    half = MLA_ROPE_DIM // 2
    kr_partner = jnp.concatenate([-kr[half:], kr[:half]], axis=0)
    w_a = repack_columns(wt3, layer, 0, offs[2])
    w_r = jnp.concatenate([kr, zcols(LANES - MLA_ROPE_DIM), kr_partner, zcols(LANES - MLA_ROPE_DIM),
                           cols(offs[6], offs[7]), zcols(LANES - GLA_GATE_RANK)], axis=0).T
    w_qkv = repack_columns(wt3, layer, offs[3], offs[6] - offs[3])
    w_gg = repack_columns(wt3, layer, offs[7], offs[8] - offs[7])
    w_s = repack_columns(wt3, layer, offs[8], offs[11] - offs[8])
    w_gates = repack_columns(wt3, layer, offs[11], offs[12] - offs[11])
    qk = MLA_NOPE_DIM + MLA_ROPE_DIM
    wq = w_uq.reshape(-1, heads_mla, qk)
    pad = jnp.zeros((wq.shape[0], heads_mla, MLA_HEAD_PAD - qk), w_uq.dtype)
    w_q1 = jnp.concatenate([wq, pad], axis=-1).reshape(wq.shape[0], -1).astype(BF16)
    rot = wq[..., MLA_NOPE_DIM:]
    pad2 = jnp.zeros((wq.shape[0], heads_mla, LANES - MLA_ROPE_DIM), w_uq.dtype)
    w_q2 = jnp.concatenate([_rope_partner(rot), pad2], axis=-1).reshape(wq.shape[0], -1).astype(BF16)
    return w_a, w_r, w_qkv, w_gg, w_s, w_gates, w_q1.T, w_q2.T


def _moe_layer(x, gain, w_rg, b_rg, w_re, b_re, w_gate, w_up, w_down, layer, next_gain, norm_dtype,
               emit_x):
    m, d = x.shape
    n_exp, n_grp = w_re.shape[1], w_rg.shape[1]
    w_r = jnp.concatenate([w_re, w_rg, jnp.zeros((d, LANES - n_exp - n_grp), F32)], axis=1)
    b_r = jnp.concatenate([b_re, b_rg, jnp.zeros((LANES - n_exp - n_grp,), F32)]).reshape(1, LANES)
    hn, e_idx, wts, rank, counts = moe_router(x, gain, w_r, b_r, n_exp, n_grp)
    tm = 640 if m >= 2048 else 64
    cnt = counts[0, :n_exp]
    padded = ((cnt + tm - 1) // tm) * tm
    ends = jnp.cumsum(padded)
    offs = ends - padded
    n_tiles = -(-(TOP_K_IN_GROUP * m) // tm) + n_exp
    tile_start = jnp.arange(n_tiles, dtype=jnp.int32) * tm
    tile_valid = (tile_start < ends[-1]).astype(jnp.int32)
    last_tile = jnp.maximum(ends[-1] // tm - 1, 0).astype(jnp.int32)
    tile_src = jnp.minimum(jnp.arange(n_tiles, dtype=jnp.int32), last_tile)
    tile_expert = jnp.sum((ends[None, :] <= (tile_src * tm)[:, None]).astype(jnp.int32), axis=1)
    tile_expert = jnp.minimum(tile_expert, n_exp - 1)
    pos = offs[e_idx[:, :2]].astype(jnp.int32) + rank[:, :2]
    pos1, pos2 = pos[:, 0], pos[:, 1]
    xs = moe_dispatch(hn, pos1, pos2, n_tiles * tm)
    ys = moe_experts(xs, tile_expert, tile_src, tile_valid, w_gate, w_up, w_down, layer, tm)
    return moe_combine(x, ys, pos1, pos2, wts, next_gain, norm_dtype, emit_x)


def kernel(x, positions, attn_norm, w_in, gate_bias, mla_q_norm, mla_w_uq, mla_kv_norm, mla_w_ukv,
           gla_w_a2, gla_b_a, gla_out_norm, swa_sinks, w_br_mla, w_br_gla, w_br_swa, w_out,
           ffn_norm, w_router_group, b_router_group, w_router_expert, b_router_expert,
           w_gate, w_up, w_down, final_norm):
    batch, seq, d = x.shape
    depth = w_in.shape[0]
    m = batch * seq
    heads_mla = mla_w_uq.shape[2] // (MLA_NOPE_DIM + MLA_ROPE_DIM)
    xf = x.reshape(m, d)
    pos = positions.reshape(m, 1).astype(jnp.int32)
    half = MLA_ROPE_DIM // 2
    inv_freq = ROPE_THETA ** (-jnp.arange(0, MLA_ROPE_DIM, 2, dtype=F32) / MLA_ROPE_DIM)
    freq = jnp.concatenate([inv_freq, inv_freq, jnp.zeros((LANES - 2 * half,), F32)]).reshape(1, LANES)

    w_in_t = jnp.swapaxes(w_in, 1, 2)
    h = rmsnorm(xf, attn_norm[0], BF16)
    out = None
    for l in range(depth):
        w_a, w_r, w_qkv, w_gg, w_s, w_gates, w_q1t, w_q2t = _layer_weights(
            w_in_t, l, mla_w_uq[l], heads_mla)
        z_a = matmul(h, w_a, F32, name="in_proj_mla")
        z_r = matmul(h, w_r, F32, name="in_proj_rope_gate")
        z_qkv = matmul(h, w_qkv, BF16, name="in_proj_gla_qkv")
        z_gg = matmul(h, w_gg, BF16, name="in_proj_gla_gate")
        z_s = matmul(h, w_s, BF16, name="in_proj_swa")
        w_kv = mla_w_ukv[l].reshape(-1, heads_mla, MLA_NOPE_DIM + MLA_V_DIM)
        w_kn = w_kv[..., :MLA_NOPE_DIM].reshape(w_kv.shape[0], -1).astype(BF16)
        w_vt = w_kv[..., MLA_NOPE_DIM:].reshape(w_kv.shape[0], -1).T.astype(BF16)
        qt, k, vt = mla_prep(z_a, z_r, pos, mla_q_norm[l], mla_kv_norm[l], freq, w_q1t, w_q2t,
                             w_kn, w_vt, heads_mla)
        y_mla = mla_attention(qt, k, vt, batch, seq, heads_mla)
        w_a2p = jnp.concatenate([gla_w_a2[l], jnp.zeros((LANES - GLA_GATE_RANK, gla_w_a2.shape[2]), F32)], axis=0)
        y_gla = gla_attention(z_qkv, z_gg, z_r, 2, w_a2p, gla_b_a[l], gla_out_norm[l], batch, seq, GLA_HEADS)
        y_swa = swa_attention(z_s, swa_sinks[l], batch, seq, SWA_Q_HEADS, SWA_KV_HEADS)
        w_brs = (w_br_mla[l].astype(BF16), w_br_gla[l].astype(BF16), w_br_swa[l].astype(BF16))
        merged = merge_branches(h, (y_mla, y_gla, y_swa), w_gates, w_brs, gate_bias[l])
        xf = matmul(merged, w_out[l].astype(BF16), F32, residual=xf, name="out_proj")
        last = l == depth - 1
        res = _moe_layer(xf, ffn_norm[l], w_router_group[l], b_router_group[l],
                         w_router_expert[l], b_router_expert[l], w_gate, w_up, w_down, l,
                         final_norm if last else attn_norm[l + 1],
                         F32 if last else BF16, emit_x=not last)
        if last:
            out = res[0]
        else:
            xf, h = res
    return out.reshape(batch, seq, d)
```
